```python
import jax, jax.numpy as jnp
from jax import lax
import numpy as np

D_MODEL = 1024
BATCH = 16
SEQ = 4096
DEPTH = 1

D_MIX = D_MODEL
ATTN_WIDTH = D_MIX // 2
ATTN_HEAD_DIM = 64
ATTN_HEADS = ATTN_WIDTH // ATTN_HEAD_DIM
DILATED_PATTERNS = ((128, 1), (512, 4), (2048, 16))
ATTN_BLOCK = 128
GMLP_WIDTH = D_MIX - ATTN_WIDTH
GMLP_CHUNK = 128
GMLP_GROUP = 128
GMLP_GROUPS = GMLP_WIDTH // GMLP_GROUP
IN_WIDTH = 3 * ATTN_WIDTH + 2 * GMLP_WIDTH
PEER_HEADS = 8
PEER_TOPK = 16
N_KEYS = 128
N_EXPERTS = N_KEYS * N_KEYS
D_KEY = 256
PEER_TOKEN_BLOCK = 128
RMS_EPS = 1e-6
NEG = -1e30

kernel_name = "hymba_dilated_gmlp_peer_layer"


def rms_norm(x, g):
    xf = x.astype(jnp.float32)
    y = xf * lax.rsqrt(jnp.mean(xf * xf, axis=-1, keepdims=True) + RMS_EPS)
    return (y * g.astype(jnp.float32)).astype(x.dtype)


def dilated_window_attention(q, k, v, window, dilation):
    b, s, h, c = q.shape
    n_back = window // dilation
    L = s // dilation
    nb = -(-L // ATTN_BLOCK)
    Lp = nb * ATTN_BLOCK

    def to_residue(t):
        return t.reshape(b, L, dilation, h, c).transpose(0, 2, 1, 3, 4)

    qr, kr, vr = to_residue(q), to_residue(k), to_residue(v)
    qb = jnp.pad(qr, ((0, 0), (0, 0), (0, Lp - L), (0, 0), (0, 0))).reshape(b, dilation, nb, ATTN_BLOCK, h, c)

    def key_blocks(t):
        tp = jnp.pad(t, ((0, 0), (0, 0), (ATTN_BLOCK, Lp - L), (0, 0), (0, 0)))
        tp = tp.reshape(b, dilation, nb + 1, ATTN_BLOCK, h, c)
        return jnp.concatenate([tp[:, :, :-1], tp[:, :, 1:]], axis=3)

    kb, vb = key_blocks(kr), key_blocks(vr)
    scores = jnp.einsum('bdnqhc,bdnkhc->bdnhqk', qb, kb).astype(jnp.float32) * (c ** -0.5)
    qi = jnp.arange(ATTN_BLOCK)[:, None]
    kj = jnp.arange(2 * ATTN_BLOCK)[None, :]
    rel = qi + ATTN_BLOCK - kj
    key_pos = jnp.arange(nb)[:, None, None] * ATTN_BLOCK + kj[None] - ATTN_BLOCK
    mask = (rel >= 0) & (rel <= n_back) & (key_pos >= 0)
    scores = jnp.where(mask[:, None], scores, NEG)
    m = jnp.max(scores, axis=-1, keepdims=True)
    p = jnp.exp(scores - m)
    den = jnp.sum(p, axis=-1, keepdims=True)
    out = jnp.einsum('bdnhqk,bdnkhc->bdnqhc', (p / den).astype(v.dtype), vb)
    lse = (m + jnp.log(den))[..., 0]
    out = out.reshape(b, dilation, Lp, h, c)[:, :, :L].transpose(0, 2, 1, 3, 4).reshape(b, s, h, c)
    lse = lse.transpose(0, 1, 2, 4, 3).reshape(b, dilation, Lp, h)[:, :, :L]
    lse = lse.transpose(0, 2, 1, 3).reshape(b, s, h)
    return out, lse


def dilated_mixture_attention(q, k, v):
    outs, lses = [], []
    for window, dilation in DILATED_PATTERNS:
        o, l = dilated_window_attention(q, k, v, window, dilation)
        outs.append(o)
        lses.append(l)
    wts = jax.nn.softmax(jnp.stack(lses, axis=-1), axis=-1)
    return jnp.einsum('bshpc,bshp->bshc', jnp.stack(outs, axis=3), wts.astype(q.dtype))


def chunked_spatial_gating(u, v, v_norm_g, w_spatial, b_spatial):
    b, s, _ = u.shape
    u = jax.nn.gelu(u, approximate=False)
    v = jax.nn.gelu(v, approximate=False)
    v = v.reshape(b, s // GMLP_CHUNK, GMLP_CHUNK, GMLP_GROUPS, GMLP_GROUP)
    v = rms_norm(v, v_norm_g)
    causal = jnp.tril(jnp.ones((GMLP_CHUNK, GMLP_CHUNK), dtype=bool))
    w = jnp.where(causal, w_spatial, jnp.zeros_like(w_spatial)).astype(v.dtype)
    z = jnp.einsum('gij,bnjgc->bnigc', w, v) + b_spatial.T[:, :, None].astype(v.dtype)
    return u * z.reshape(b, s, GMLP_WIDTH)


def peer_ffn(xn, w_query, sub_keys_a, sub_keys_b, expert_u, expert_v):
    b, s, d = xn.shape
    tokens = xn.reshape(-1, PEER_TOKEN_BLOCK, d)
    half = D_KEY // 2

    def block(xb):
        qh = (xb @ w_query).reshape(-1, PEER_HEADS, D_KEY)
        sa = jnp.einsum('thc,kc->thk', qh[..., :half], sub_keys_a).astype(jnp.float32)
        sb = jnp.einsum('thc,kc->thk', qh[..., half:], sub_keys_b).astype(jnp.float32)
        va, ia = lax.top_k(sa, PEER_TOPK)
        vb, ib = lax.top_k(sb, PEER_TOPK)
        cand = (va[..., :, None] + vb[..., None, :]).reshape(-1, PEER_HEADS, PEER_TOPK * PEER_TOPK)
        cand_idx = (ia[..., :, None] * N_KEYS + ib[..., None, :]).reshape(-1, PEER_HEADS, PEER_TOPK * PEER_TOPK)
        top_s, pos = lax.top_k(cand, PEER_TOPK)
        idx = jnp.take_along_axis(cand_idx, pos, axis=-1)
        gate = jax.nn.softmax(top_s, axis=-1)
        u = expert_u[idx]
        act = jax.nn.gelu(jnp.einsum('thkd,td->thk', u, xb), approximate=False)
        coef = (gate * act.astype(jnp.float32)).astype(xb.dtype)
        return jnp.einsum('thk,thkd->td', coef, expert_v[idx])

    return lax.map(block, tokens).reshape(b, s, d)


def setup_inputs(seed: int = 0) -> dict:
    key = jax.random.key(seed)
    ks = jax.random.split(key, 17)

    def nrm(k, shape, scale):
        return jax.random.normal(k, shape, jnp.float32) * scale

    def gain(k, shape):
        return 1.0 + 0.01 * jax.random.normal(k, shape, jnp.float32)

    return {
        "x": nrm(ks[0], (BATCH, SEQ, D_MODEL), 1.0),
        "mix_norm_g": gain(ks[1], (DEPTH, D_MODEL)),
        "w_in": nrm(ks[2], (DEPTH, D_MODEL, IN_WIDTH), D_MODEL ** -0.5),
        "q_norm_g": gain(ks[3], (DEPTH, ATTN_HEAD_DIM)),
        "k_norm_g": gain(ks[4], (DEPTH, ATTN_HEAD_DIM)),
        "v_gate_norm_g": gain(ks[5], (DEPTH, GMLP_GROUPS, GMLP_GROUP)),
        "w_spatial": nrm(ks[6], (DEPTH, GMLP_GROUPS, GMLP_CHUNK, GMLP_CHUNK), GMLP_CHUNK ** -0.5),
        "b_spatial": 1.0 + nrm(ks[7], (DEPTH, GMLP_GROUPS, GMLP_CHUNK), 0.1),
        "attn_out_g": gain(ks[8], (DEPTH, ATTN_WIDTH)),
        "gate_out_g": gain(ks[9], (DEPTH, GMLP_WIDTH)),
        "w_out": nrm(ks[10], (DEPTH, D_MIX, D_MODEL), D_MIX ** -0.5),
        "ffn_norm_g": gain(ks[11], (DEPTH, D_MODEL)),
        "w_query": nrm(ks[12], (DEPTH, D_MODEL, PEER_HEADS * D_KEY), D_MODEL ** -0.5),
        "sub_keys_a": nrm(ks[13], (DEPTH, N_KEYS, D_KEY // 2), (D_KEY // 2) ** -0.5),
        "sub_keys_b": nrm(ks[14], (DEPTH, N_KEYS, D_KEY // 2), (D_KEY // 2) ** -0.5),
        "expert_u": nrm(ks[15], (DEPTH, N_EXPERTS, D_MODEL), D_MODEL ** -0.5),
        "expert_v": nrm(ks[16], (DEPTH, N_EXPERTS, D_MODEL), PEER_HEADS ** -0.5),
    }


def reference(x, mix_norm_g, w_in, q_norm_g, k_norm_g, v_gate_norm_g, w_spatial, b_spatial,
              attn_out_g, gate_out_g, w_out, ffn_norm_g, w_query, sub_keys_a, sub_keys_b,
              expert_u, expert_v):
    b, s, _ = x.shape
    splits = [ATTN_WIDTH, 2 * ATTN_WIDTH, 3 * ATTN_WIDTH, 3 * ATTN_WIDTH + GMLP_WIDTH]
    for l in range(DEPTH):
        hn = rms_norm(x, mix_norm_g[l])
        proj = hn @ w_in[l]
        q, k, v, gu, gv = jnp.split(proj, splits, axis=-1)
        q = rms_norm(q.reshape(b, s, ATTN_HEADS, ATTN_HEAD_DIM), q_norm_g[l])
        k = rms_norm(k.reshape(b, s, ATTN_HEADS, ATTN_HEAD_DIM), k_norm_g[l])
        v = v.reshape(b, s, ATTN_HEADS, ATTN_HEAD_DIM)
        attn = dilated_mixture_attention(q, k, v).reshape(b, s, ATTN_WIDTH)
        gated = chunked_spatial_gating(gu, gv, v_gate_norm_g[l], w_spatial[l], b_spatial[l])
        merged = jnp.concatenate([rms_norm(attn, attn_out_g[l]), rms_norm(gated, gate_out_g[l])], axis=-1)
        x = x + merged @ w_out[l]
        x = x + peer_ffn(rms_norm(x, ffn_norm_g[l]), w_query[l], sub_keys_a[l], sub_keys_b[l],
                         expert_u[l], expert_v[l])
    return x
```

```python
import functools

import jax
import jax.numpy as jnp
from jax import lax
from jax.experimental import pallas as pl
from jax.experimental.pallas import tpu as pltpu

D_MODEL = 1024
ATTN_WIDTH = 512
HEAD_DIM = 64
HEAD_PAIRS = ATTN_WIDTH // 128
PATTERNS = ((128, 1), (512, 4), (2048, 16))
N_BACK = 128
ATTN_BLOCK = 128
GMLP_WIDTH = 512
GMLP_CHUNK = 128
GMLP_GROUPS = 4
IN_WIDTH = 3 * ATTN_WIDTH + 2 * GMLP_WIDTH
PEER_HEADS = 8
PEER_TOPK = 16
N_KEYS = 128
D_KEY = 256
NJ = PEER_HEADS * PEER_TOPK
RMS_EPS = 1e-6
NEG = -1e30

LANES = 128
SUBLANES = 8
VMEM_LIMIT = 56 * 2 ** 20

TM_PROJ = 512
TQ_ATTN = 256
TM_ROUTE = 256
TB_PEER = 256


def _gelu(x):
    return 0.5 * x * (1.0 + lax.erf(x * (2.0 ** -0.5)))


def _split_bf16(x):
    hi = x.astype(jnp.bfloat16)
    lo = (x - hi.astype(jnp.float32)).astype(jnp.bfloat16)
    return hi, lo


def _dot(a, b):
    return jnp.dot(a, b, preferred_element_type=jnp.float32)


def _dot_nt(a, b):
    return lax.dot_general(a, b, (((1,), (1,)), ((), ())), preferred_element_type=jnp.float32)


def _in_proj_kernel(x_ref, gmix_ref, win_ref, gq_ref, gk_ref, g64_ref, gvg_ref, wsp_ref, bsp_ref, gout_ref,
                    q_ref, k_ref, v_ref, gated_ref, gs_ref):
    x = x_ref[...]
    ms = jnp.mean(x * x, axis=-1, keepdims=True)
    hn = (x * lax.rsqrt(ms + RMS_EPS) * gmix_ref[...]).astype(jnp.bfloat16)

    def head_norm(t, g):
        hi, lo = _split_bf16(t * t)
        msq = (_dot(hi, g64_ref[...]) + _dot(lo, g64_ref[...])) * (1.0 / HEAD_DIM)
        return t * lax.rsqrt(msq + RMS_EPS) * g

    q = _dot(hn, win_ref[:, 0:ATTN_WIDTH])
    q_ref[...] = head_norm(q, gq_ref[...]).astype(jnp.bfloat16)
    k = _dot(hn, win_ref[:, ATTN_WIDTH:2 * ATTN_WIDTH])
    k_ref[...] = head_norm(k, gk_ref[...]).astype(jnp.bfloat16)
    v_ref[...] = _dot(hn, win_ref[:, 2 * ATTN_WIDTH:3 * ATTN_WIDTH]).astype(jnp.bfloat16)

    u = _gelu(_dot(hn, win_ref[:, 3 * ATTN_WIDTH:3 * ATTN_WIDTH + GMLP_WIDTH]))
    gv = _gelu(_dot(hn, win_ref[:, 3 * ATTN_WIDTH + GMLP_WIDTH:IN_WIDTH]))
    row = lax.broadcasted_iota(jnp.int32, (GMLP_CHUNK, GMLP_CHUNK), 0)
    col = lax.broadcasted_iota(jnp.int32, (GMLP_CHUNK, GMLP_CHUNK), 1)
    causal = col <= row
    n_chunks = x.shape[0] // GMLP_CHUNK
    for g in range(GMLP_GROUPS):
        cs = slice(g * LANES, (g + 1) * LANES)
        vg = gv[:, cs]
        msg = jnp.mean(vg * vg, axis=-1, keepdims=True)
        vn = (vg * lax.rsqrt(msg + RMS_EPS) * gvg_ref[:, cs]).astype(jnp.bfloat16)
        w = jnp.where(causal, wsp_ref[g], 0.0).astype(jnp.bfloat16)
        for c in range(n_chunks):
            rs = slice(c * GMLP_CHUNK, (c + 1) * GMLP_CHUNK)
            z = _dot(w, vn[rs, :]) + bsp_ref[:, cs]
            gs_ref[rs, cs] = u[rs, cs] * z
    gated = gs_ref[...]
    msg = jnp.mean(gated * gated, axis=-1, keepdims=True)
    gated_ref[...] = (gated * lax.rsqrt(msg + RMS_EPS) * gout_ref[...]).astype(jnp.bfloat16)


def _in_proj(x2d, gmix, win, gq, gk, g64, gvg, wsp, bsp, gout):
    n = x2d.shape[0]
    tm = TM_PROJ
    full = lambda shape: pl.BlockSpec(shape, lambda i: (0,) * len(shape))
    tok = lambda w: pl.BlockSpec((tm, w), lambda i: (i, 0))
    bf = lambda: jax.ShapeDtypeStruct((n, ATTN_WIDTH), jnp.bfloat16)
    return pl.pallas_call(
        _in_proj_kernel,
        grid=(n // tm,),
        in_specs=[tok(D_MODEL), full((1, D_MODEL)), full((D_MODEL, IN_WIDTH)), full((1, ATTN_WIDTH)),
                  full((1, ATTN_WIDTH)), full((ATTN_WIDTH, ATTN_WIDTH)), full((1, GMLP_WIDTH)),
                  full((GMLP_GROUPS, GMLP_CHUNK, GMLP_CHUNK)), full((GMLP_CHUNK, GMLP_WIDTH)), full((1, GMLP_WIDTH))],
        out_specs=[tok(ATTN_WIDTH)] * 4,
        out_shape=[bf(), bf(), bf(), bf()],
        scratch_shapes=[pltpu.VMEM((tm, GMLP_WIDTH), jnp.float32)],
        compiler_params=pltpu.CompilerParams(dimension_semantics=("arbitrary",), vmem_limit_bytes=VMEM_LIMIT),
        name="in_proj",
    )(x2d, gmix, win, gq, gk, g64, gvg, wsp, bsp, gout)


def _attn_kernel(q_ref, kp_ref, kc_ref, vp_ref, vc_ref, o_ref, l_ref):
    tq = q_ref.shape[1]
    have_prev = pl.program_id(2) > 0
    qi = lax.broadcasted_iota(jnp.int32, (ATTN_BLOCK, 2 * ATTN_BLOCK), 0)
    kj = lax.broadcasted_iota(jnp.int32, (ATTN_BLOCK, 2 * ATTN_BLOCK), 1)
    rel = qi + ATTN_BLOCK - kj
    band = (rel >= 0) & (rel <= N_BACK)
    lane = lax.broadcasted_iota(jnp.int32, (ATTN_BLOCK, LANES), 1)
    for qb in range(tq // ATTN_BLOCK):
        rs = slice(qb * ATTN_BLOCK, (qb + 1) * ATTN_BLOCK)
        if qb == 0:
            mask = band & ((kj >= ATTN_BLOCK) | have_prev)
        else:
            mask = band
        for p in range(HEAD_PAIRS):
            cs = slice(p * LANES, (p + 1) * LANES)
            qp = q_ref[0, rs, cs]
            if qb == 0:
                kprev, vprev = kp_ref[0, :, cs], vp_ref[0, :, cs]
            else:
                ps = slice((qb - 1) * ATTN_BLOCK, qb * ATTN_BLOCK)
                kprev, vprev = kc_ref[0, ps, cs], vc_ref[0, ps, cs]
            keys = jnp.concatenate([kprev, kc_ref[0, rs, cs]], axis=0)
            vals = jnp.concatenate([vprev, vc_ref[0, rs, cs]], axis=0)
            out_pair = jnp.zeros((ATTN_BLOCK, LANES), jnp.float32)
            lse_pair = jnp.zeros((ATTN_BLOCK, LANES), jnp.float32)
            for hh in range(2):
                in_head = (lane >= hh * HEAD_DIM) & (lane < (hh + 1) * HEAD_DIM)
                s = _dot_nt(jnp.where(in_head, qp, jnp.zeros_like(qp)), keys)
                s = jnp.where(mask, s, NEG)
                m = jnp.max(s, axis=-1, keepdims=True)
                e = jnp.exp(s - m)
                den = jnp.sum(e, axis=-1, keepdims=True)
                o = _dot(e.astype(jnp.bfloat16), vals) / den
                out_pair = jnp.where(in_head, o, out_pair)
                lse_pair = jnp.where(in_head, m + jnp.log(den), lse_pair)
            o_ref[0, rs, cs] = out_pair
            l_ref[0, rs, cs] = lse_pair


def _attention(q, k, v, batch, seq, dilation):
    L = seq // dilation
    tq = min(TQ_ATTN, L)
    view = lambda t: t.reshape(batch, L, dilation * ATTN_WIDTH)
    cur = pl.BlockSpec((1, tq, ATTN_WIDTH), lambda b, r, i: (b, i, r))
    prev = pl.BlockSpec((1, ATTN_BLOCK, ATTN_WIDTH),
                        lambda b, r, i: (b, jnp.maximum(i * (tq // ATTN_BLOCK) - 1, 0), r))
    out = jax.ShapeDtypeStruct((batch, L, dilation * ATTN_WIDTH), jnp.float32)
    o, l = pl.pallas_call(
        _attn_kernel,
        grid=(batch, dilation, L // tq),
        in_specs=[cur, prev, cur, prev, cur],
        out_specs=[cur, cur],
        out_shape=[out, out],
        compiler_params=pltpu.CompilerParams(dimension_semantics=("arbitrary",) * 3, vmem_limit_bytes=VMEM_LIMIT),
        name=f"attention_d{dilation}",
    )(view(q), view(k), view(k), view(v), view(v))
    return o.reshape(batch * seq, ATTN_WIDTH), l.reshape(batch * seq, ATTN_WIDTH)


def _out_proj_kernel(o1_ref, o2_ref, o3_ref, l1_ref, l2_ref, l3_ref, gated_ref, x_ref, wout_ref, gattn_ref, x2_ref):
    l1, l2, l3 = l1_ref[...], l2_ref[...], l3_ref[...]
    m = jnp.maximum(jnp.maximum(l1, l2), l3)
    e1, e2, e3 = jnp.exp(l1 - m), jnp.exp(l2 - m), jnp.exp(l3 - m)
    attn = (e1 * o1_ref[...] + e2 * o2_ref[...] + e3 * o3_ref[...]) / (e1 + e2 + e3)
    ms = jnp.mean(attn * attn, axis=-1, keepdims=True)
    attn_n = (attn * lax.rsqrt(ms + RMS_EPS) * gattn_ref[...]).astype(jnp.bfloat16)
    y = _dot(attn_n, wout_ref[0:ATTN_WIDTH, :]) + _dot(gated_ref[...], wout_ref[ATTN_WIDTH:, :])
    x2_ref[...] = x_ref[...] + y


def _out_proj(outs, lses, gated, x2d, wout, gattn):
    n = x2d.shape[0]
    tm = TM_PROJ
    tok = lambda w: pl.BlockSpec((tm, w), lambda i: (i, 0))
    full = lambda shape: pl.BlockSpec(shape, lambda i: (0,) * len(shape))
    return pl.pallas_call(
        _out_proj_kernel,
        grid=(n // tm,),
        in_specs=[tok(ATTN_WIDTH)] * 7 + [tok(D_MODEL), full((D_MODEL, D_MODEL)), full((1, ATTN_WIDTH))],
        out_specs=tok(D_MODEL),
        out_shape=jax.ShapeDtypeStruct((n, D_MODEL), jnp.float32),
        compiler_params=pltpu.CompilerParams(dimension_semantics=("arbitrary",), vmem_limit_bytes=VMEM_LIMIT),
        name="out_proj",
    )(*outs, *lses, gated, x2d, wout, gattn)


def _top16(s):
    r = s.shape[0]
    rowid = lax.broadcasted_iota(jnp.int32, s.shape, 0)
    vals, idxs = [], []
    for _ in range(PEER_TOPK):
        m = jnp.max(s, axis=0, keepdims=True)
        pos = jnp.min(jnp.where(s == m, rowid, r), axis=0, keepdims=True)
        vals.append(m)
        idxs.append(pos)
        s = jnp.where(rowid == pos, -jnp.inf, s)
    return jnp.concatenate(vals, axis=0), jnp.concatenate(idxs, axis=0)


def _route_kernel(x2_ref, gffn_ref, wq_ref, ka_ref, kb_ref, xn_ref, idx_ref, gate_ref):
    x = x2_ref[...]
    ms = jnp.mean(x * x, axis=-1, keepdims=True)
    xn = x * lax.rsqrt(ms + RMS_EPS) * gffn_ref[...]
    xn_ref[...] = xn
    qh = _dot(xn.astype(jnp.bfloat16), wq_ref[...])
    half = D_KEY // 2
    k16 = lax.broadcasted_iota(jnp.int32, (PEER_TOPK, LANES), 0)
    for lt in range(x.shape[0] // LANES):
        ts = slice(lt * LANES, (lt + 1) * LANES)
        idx_rows, gate_rows = [], []
        for h in range(PEER_HEADS):
            qa = qh[ts, h * D_KEY:h * D_KEY + half].astype(jnp.bfloat16)
            qb = qh[ts, h * D_KEY + half:(h + 1) * D_KEY].astype(jnp.bfloat16)
            va, ia = _top16(_dot_nt(ka_ref[...], qa))
            vb, ib = _top16(_dot_nt(kb_ref[...], qb))
            cand = jnp.concatenate([va[i:i + 1, :] + vb for i in range(PEER_TOPK)], axis=0)
            top_s, pos = _top16(cand)
            pa, pb = pos >> 4, pos & (PEER_TOPK - 1)
            sel = []
            for kk in range(PEER_TOPK):
                ea = jnp.sum(jnp.where(k16 == pa[kk:kk + 1, :], ia, 0), axis=0, keepdims=True)
                eb = jnp.sum(jnp.where(k16 == pb[kk:kk + 1, :], ib, 0), axis=0, keepdims=True)
                sel.append(ea * N_KEYS + eb)
            idx_rows.append(jnp.concatenate(sel, axis=0))
            e = jnp.exp(top_s - top_s[0:1, :])
            gate_rows.append(e / jnp.sum(e, axis=0, keepdims=True))
        idx_all = jnp.concatenate(idx_rows, axis=0)
        gate_all = jnp.concatenate(gate_rows, axis=0)
        idx_ref[ts, :] = idx_all.T
        gate_ref[ts, :] = gate_all.T


def _peer_route(x2, gffn, wq, ka, kb):
    n = x2.shape[0]
    tm = TM_ROUTE
    tok = lambda w: pl.BlockSpec((tm, w), lambda i: (i, 0))
    full = lambda shape: pl.BlockSpec(shape, lambda i: (0,) * len(shape))
    return pl.pallas_call(
        _route_kernel,
        grid=(n // tm,),
        in_specs=[tok(D_MODEL), full((1, D_MODEL)), full((D_MODEL, PEER_HEADS * D_KEY)),
                  full((N_KEYS, D_KEY // 2)), full((N_KEYS, D_KEY // 2))],
        out_specs=[tok(D_MODEL), tok(NJ), tok(NJ)],
        out_shape=[jax.ShapeDtypeStruct((n, D_MODEL), jnp.float32),
                   jax.ShapeDtypeStruct((n, NJ), jnp.int32),
                   jax.ShapeDtypeStruct((n, NJ), jnp.float32)],
        compiler_params=pltpu.CompilerParams(dimension_semantics=("arbitrary",), vmem_limit_bytes=VMEM_LIMIT),
        name="peer_route",
    )(x2, gffn, wq, ka, kb)


def _pack_table(tab):
    bits = lax.bitcast_convert_type(tab.astype(jnp.bfloat16), jnp.uint16).astype(jnp.uint32)
    half = D_MODEL // 2
    word = bits[:, :half] | (bits[:, half:] << 16)
    return word.reshape(tab.shape[0], 4, LANES)


def _unpack(w):
    lo = lax.bitcast_convert_type(w << 16, jnp.float32)
    hi = lax.bitcast_convert_type(w & jnp.uint32(0xFFFF0000), jnp.float32)
    return lo, hi


def _gather_rows(idx_ref, base, tab_ref, w_ref):
    for j in range(NJ):
        w_ref[pl.ds((j // SUBLANES) * 32 + j % SUBLANES, 4, stride=SUBLANES), :] = tab_ref[idx_ref[base + j]]


def _peer_up_kernel(idx_ref, gate_ref, xn_ref, tab_ref, coef_ref, w_ref, act_ref, *, tb):
    def body(t, carry):
        _gather_rows(idx_ref, t * NJ, tab_ref, w_ref)
        xt = xn_ref[t]
        xb = [jnp.broadcast_to(xt[c:c + 1, :], (SUBLANES, LANES)) for c in range(SUBLANES)]
        parts = []
        for g in range(NJ // SUBLANES):
            acc = None
            for s in range(4):
                lo, hi = _unpack(w_ref[pl.ds(g * 32 + s * SUBLANES, SUBLANES), :])
                term = lo * xb[s] + hi * xb[4 + s]
                acc = term if acc is None else acc + term
            parts.append(acc)
        p = jnp.concatenate(parts, axis=0)
        act_ref[pl.ds(t, 1), :] = jnp.sum(p.T, axis=0, keepdims=True)
        return carry

    lax.fori_loop(0, tb, body, 0)
    coef_ref[...] = gate_ref[...] * _gelu(act_ref[...])


def _peer_down_kernel(idx_ref, coef_ref, x2_ref, tab_ref, o_ref, w_ref, *, tb):
    def body(t, carry):
        _gather_rows(idx_ref, t * NJ, tab_ref, w_ref)
        c = coef_ref[pl.ds(t, 1), :]
        cb = jnp.broadcast_to(c, (NJ, LANES)).T
        acc_lo, acc_hi = [None] * 4, [None] * 4
        for g in range(NJ // SUBLANES):
            cg = cb[g * SUBLANES:(g + 1) * SUBLANES, :]
            for s in range(4):
                lo, hi = _unpack(w_ref[pl.ds(g * 32 + s * SUBLANES, SUBLANES), :])
                acc_lo[s] = lo * cg if acc_lo[s] is None else acc_lo[s] + lo * cg
                acc_hi[s] = hi * cg if acc_hi[s] is None else acc_hi[s] + hi * cg
        rows = [jnp.sum(a, axis=0, keepdims=True) for a in acc_lo + acc_hi]
        o_ref[t] = x2_ref[t] + jnp.concatenate(rows, axis=0)
        return carry

    lax.fori_loop(0, tb, body, 0)


def _peer_specs(n, e, tb):
    idx_spec = pl.BlockSpec((tb * NJ,), lambda i: (i,), memory_space=pltpu.SMEM)
    row_spec = pl.BlockSpec((tb, NJ), lambda i: (i, 0))
    tok_spec = pl.BlockSpec((tb, SUBLANES, LANES), lambda i: (i, 0, 0))
    tab_spec = pl.BlockSpec((e, 4, LANES), lambda i: (0, 0, 0), pipeline_mode=pl.Buffered(1))
    return idx_spec, row_spec, tok_spec, tab_spec


def _peer_up(idx, gate, xn, tab):
    n, e, tb = xn.shape[0], tab.shape[0], TB_PEER
    idx_spec, row_spec, tok_spec, tab_spec = _peer_specs(n, e, tb)
    return pl.pallas_call(
        functools.partial(_peer_up_kernel, tb=tb),
        grid=(n // tb,),
        in_specs=[idx_spec, row_spec, tok_spec, tab_spec],
        out_specs=row_spec,
        out_shape=jax.ShapeDtypeStruct((n, NJ), jnp.float32),
        scratch_shapes=[pltpu.VMEM((NJ * 4, LANES), jnp.uint32), pltpu.VMEM((tb, NJ), jnp.float32)],
        compiler_params=pltpu.CompilerParams(dimension_semantics=("arbitrary",), vmem_limit_bytes=VMEM_LIMIT),
        name="peer_up",
    )(idx.reshape(-1), gate, xn.reshape(n, SUBLANES, LANES), tab)


def _peer_down(idx, coef, x2, tab):
    n, e, tb = x2.shape[0], tab.shape[0], TB_PEER
    idx_spec, row_spec, tok_spec, tab_spec = _peer_specs(n, e, tb)
    return pl.pallas_call(
        functools.partial(_peer_down_kernel, tb=tb),
        grid=(n // tb,),
        in_specs=[idx_spec, row_spec, tok_spec, tab_spec],
        out_specs=tok_spec,
        out_shape=jax.ShapeDtypeStruct((n, SUBLANES, LANES), jnp.float32),
        scratch_shapes=[pltpu.VMEM((NJ * 4, LANES), jnp.uint32)],
        compiler_params=pltpu.CompilerParams(dimension_semantics=("arbitrary",), vmem_limit_bytes=VMEM_LIMIT),
        name="peer_down",
    )(idx.reshape(-1), coef, x2.reshape(n, SUBLANES, LANES), tab).reshape(n, D_MODEL)


def _layer(x2d, batch, seq, mix_norm_g, w_in, q_norm_g, k_norm_g, v_gate_norm_g, w_spatial, b_spatial, attn_out_g,
           gate_out_g, w_out, ffn_norm_g, w_query, sub_keys_a, sub_keys_b, expert_u, expert_v):
    bf16 = jnp.bfloat16
    heads = ATTN_WIDTH // HEAD_DIM
    lane = jnp.arange(ATTN_WIDTH)
    g64 = (lane[:, None] // HEAD_DIM == lane[None, :] // HEAD_DIM).astype(bf16)
    gq = (jnp.tile(q_norm_g, heads) * HEAD_DIM ** -0.5)[None, :]
    gk = jnp.tile(k_norm_g, heads)[None, :]
    bsp = jnp.repeat(b_spatial.T, GMLP_WIDTH // GMLP_GROUPS, axis=1)
    q, k, v, gated = _in_proj(x2d, mix_norm_g[None, :], w_in.astype(bf16), gq, gk, g64,
                              v_gate_norm_g.reshape(1, GMLP_WIDTH), w_spatial, bsp, gate_out_g[None, :])
    outs, lses = [], []
    for window, dilation in PATTERNS:
        assert window // dilation == N_BACK
        o, l = _attention(q, k, v, batch, seq, dilation)
        outs.append(o)
        lses.append(l)
    x2 = _out_proj(outs, lses, gated, x2d, w_out.astype(bf16), attn_out_g[None, :])
    xn, idx, gate = _peer_route(x2, ffn_norm_g[None, :], w_query.astype(bf16), sub_keys_a.astype(bf16),
                                sub_keys_b.astype(bf16))
    coef = _peer_up(idx, gate, xn, _pack_table(expert_u))
    return _peer_down(idx, coef, x2, _pack_table(expert_v))


def kernel(x, mix_norm_g, w_in, q_norm_g, k_norm_g, v_gate_norm_g, w_spatial, b_spatial, attn_out_g, gate_out_g,
           w_out, ffn_norm_g, w_query, sub_keys_a, sub_keys_b, expert_u, expert_v):
    batch, seq, d = x.shape
    assert d == D_MODEL and seq % (TQ_ATTN * PATTERNS[-1][1]) == 0 and (batch * seq) % TM_PROJ == 0
    x2d = x.reshape(batch * seq, d)
    for l in range(mix_norm_g.shape[0]):
        x2d = _layer(x2d, batch, seq, mix_norm_g[l], w_in[l], q_norm_g[l], k_norm_g[l], v_gate_norm_g[l],
                     w_spatial[l], b_spatial[l], attn_out_g[l], gate_out_g[l], w_out[l], ffn_norm_g[l],
                     w_query[l], sub_keys_a[l], sub_keys_b[l], expert_u[l], expert_v[l])
    return x2d.reshape(batch, seq, d)
```

```python
import functools

import jax
import jax.numpy as jnp
from jax import lax
from jax.experimental import pallas as pl
from jax.experimental.pallas import tpu as pltpu

D_MODEL = 1024
ATTN_WIDTH = 512
HEAD_DIM = 64
HEAD_PAIRS = ATTN_WIDTH // 128
PATTERNS = ((128, 1), (512, 4), (2048, 16))
N_BACK = 128
ATTN_BLOCK = 128
GMLP_WIDTH = 512
GMLP_CHUNK = 128
GMLP_GROUPS = 4
IN_WIDTH = 3 * ATTN_WIDTH + 2 * GMLP_WIDTH
PEER_HEADS = 8
PEER_TOPK = 16
N_KEYS = 128
D_KEY = 256
NJ = PEER_HEADS * PEER_TOPK
TABLE_ROWS_PER_EXPERT = 4
RMS_EPS = 1e-6
NEG = -1e30

LANES = 128
SUBLANES = 8
VMEM_LIMIT = 56 * 2 ** 20

TM_PROJ = 512
TQ_ATTN = 256
TM_ROUTE = 256
TB_PEER = 256


def _gelu(x):
    return 0.5 * x * (1.0 + lax.erf(x * (2.0 ** -0.5)))


def _split_bf16(x):
    hi = x.astype(jnp.bfloat16)
    lo = (x - hi.astype(jnp.float32)).astype(jnp.bfloat16)
    return hi, lo


def _dot(a, b):
    return jnp.dot(a, b, preferred_element_type=jnp.float32)


def _dot_nt(a, b):
    return lax.dot_general(a, b, (((1,), (1,)), ((), ())), preferred_element_type=jnp.float32)


def _in_proj_kernel(x_ref, gmix_ref, win_ref, gq_ref, gk_ref, g64_ref, gvg_ref, wsp_ref, bsp_ref, gout_ref,
                    q_ref, k_ref, v_ref, gated_ref, gs_ref):
    x = x_ref[...]
    ms = jnp.mean(x * x, axis=-1, keepdims=True)
    hn = (x * lax.rsqrt(ms + RMS_EPS) * gmix_ref[...]).astype(jnp.bfloat16)

    def head_norm(t, g):
        hi, lo = _split_bf16(t * t)
        msq = (_dot(hi, g64_ref[...]) + _dot(lo, g64_ref[...])) * (1.0 / HEAD_DIM)
        return t * lax.rsqrt(msq + RMS_EPS) * g

    q = _dot(hn, win_ref[:, 0:ATTN_WIDTH])
    q_ref[...] = head_norm(q, gq_ref[...]).astype(jnp.bfloat16)
    k = _dot(hn, win_ref[:, ATTN_WIDTH:2 * ATTN_WIDTH])
    k_ref[...] = head_norm(k, gk_ref[...]).astype(jnp.bfloat16)
    v_ref[...] = _dot(hn, win_ref[:, 2 * ATTN_WIDTH:3 * ATTN_WIDTH]).astype(jnp.bfloat16)

    u = _gelu(_dot(hn, win_ref[:, 3 * ATTN_WIDTH:3 * ATTN_WIDTH + GMLP_WIDTH]))
    gv = _gelu(_dot(hn, win_ref[:, 3 * ATTN_WIDTH + GMLP_WIDTH:IN_WIDTH]))
    row = lax.broadcasted_iota(jnp.int32, (GMLP_CHUNK, GMLP_CHUNK), 0)
    col = lax.broadcasted_iota(jnp.int32, (GMLP_CHUNK, GMLP_CHUNK), 1)
    causal = col <= row
    n_chunks = x.shape[0] // GMLP_CHUNK
    for g in range(GMLP_GROUPS):
        cs = slice(g * LANES, (g + 1) * LANES)
        vg = gv[:, cs]
        msg = jnp.mean(vg * vg, axis=-1, keepdims=True)
        vn = (vg * lax.rsqrt(msg + RMS_EPS) * gvg_ref[:, cs]).astype(jnp.bfloat16)
        w = jnp.where(causal, wsp_ref[g], 0.0).astype(jnp.bfloat16)
        for c in range(n_chunks):
            rs = slice(c * GMLP_CHUNK, (c + 1) * GMLP_CHUNK)
            z = _dot(w, vn[rs, :]) + bsp_ref[:, cs]
            gs_ref[rs, cs] = u[rs, cs] * z
    gated = gs_ref[...]
    msg = jnp.mean(gated * gated, axis=-1, keepdims=True)
    gated_ref[...] = (gated * lax.rsqrt(msg + RMS_EPS) * gout_ref[...]).astype(jnp.bfloat16)


def _in_proj(x2d, gmix, win, gq, gk, g64, gvg, wsp, bsp, gout):
    n = x2d.shape[0]
    tm = TM_PROJ
    full = lambda shape: pl.BlockSpec(shape, lambda i: (0,) * len(shape))
    tok = lambda w: pl.BlockSpec((tm, w), lambda i: (i, 0))
    bf = lambda: jax.ShapeDtypeStruct((n, ATTN_WIDTH), jnp.bfloat16)
    return pl.pallas_call(
        _in_proj_kernel,
        grid=(n // tm,),
        in_specs=[tok(D_MODEL), full((1, D_MODEL)), full((D_MODEL, IN_WIDTH)), full((1, ATTN_WIDTH)),
                  full((1, ATTN_WIDTH)), full((ATTN_WIDTH, ATTN_WIDTH)), full((1, GMLP_WIDTH)),
                  full((GMLP_GROUPS, GMLP_CHUNK, GMLP_CHUNK)), full((GMLP_CHUNK, GMLP_WIDTH)), full((1, GMLP_WIDTH))],
        out_specs=[tok(ATTN_WIDTH)] * 4,
        out_shape=[bf(), bf(), bf(), bf()],
        scratch_shapes=[pltpu.VMEM((tm, GMLP_WIDTH), jnp.float32)],
        compiler_params=pltpu.CompilerParams(dimension_semantics=("arbitrary",), vmem_limit_bytes=VMEM_LIMIT),
        name="in_proj",
    )(x2d, gmix, win, gq, gk, g64, gvg, wsp, bsp, gout)


def _attn_kernel(q_ref, kp_ref, kc_ref, vp_ref, vc_ref, o_ref, l_ref):
    tq = q_ref.shape[1]
    have_prev = pl.program_id(2) > 0
    qi = lax.broadcasted_iota(jnp.int32, (ATTN_BLOCK, 2 * ATTN_BLOCK), 0)
    kj = lax.broadcasted_iota(jnp.int32, (ATTN_BLOCK, 2 * ATTN_BLOCK), 1)
    rel = qi + ATTN_BLOCK - kj
    band = (rel >= 0) & (rel <= N_BACK)
    lane = lax.broadcasted_iota(jnp.int32, (ATTN_BLOCK, LANES), 1)
    for qb in range(tq // ATTN_BLOCK):
        rs = slice(qb * ATTN_BLOCK, (qb + 1) * ATTN_BLOCK)
        if qb == 0:
            mask = band & ((kj >= ATTN_BLOCK) | have_prev)
        else:
            mask = band
        for p in range(HEAD_PAIRS):
            cs = slice(p * LANES, (p + 1) * LANES)
            qp = q_ref[0, rs, cs]
            if qb == 0:
                kprev, vprev = kp_ref[0, :, cs], vp_ref[0, :, cs]
            else:
                ps = slice((qb - 1) * ATTN_BLOCK, qb * ATTN_BLOCK)
                kprev, vprev = kc_ref[0, ps, cs], vc_ref[0, ps, cs]
            keys = jnp.concatenate([kprev, kc_ref[0, rs, cs]], axis=0)
            vals = jnp.concatenate([vprev, vc_ref[0, rs, cs]], axis=0)
            out_pair = jnp.zeros((ATTN_BLOCK, LANES), jnp.float32)
            lse_pair = jnp.zeros((ATTN_BLOCK, LANES), jnp.float32)
            for hh in range(2):
                in_head = (lane >= hh * HEAD_DIM) & (lane < (hh + 1) * HEAD_DIM)
                s = _dot_nt(jnp.where(in_head, qp, jnp.zeros_like(qp)), keys)
                s = jnp.where(mask, s, NEG)
                m = jnp.max(s, axis=-1, keepdims=True)
                e = jnp.exp(s - m)
                den = jnp.sum(e, axis=-1, keepdims=True)
                o = _dot(e.astype(jnp.bfloat16), vals) / den
                out_pair = jnp.where(in_head, o, out_pair)
                lse_pair = jnp.where(in_head, m + jnp.log(den), lse_pair)
            o_ref[0, rs, cs] = out_pair
            l_ref[0, rs, cs] = lse_pair


def _attention(q, k, v, batch, seq, dilation):
    L = seq // dilation
    tq = min(TQ_ATTN, L)
    view = lambda t: t.reshape(batch, L, dilation * ATTN_WIDTH)
    cur = pl.BlockSpec((1, tq, ATTN_WIDTH), lambda b, r, i: (b, i, r))
    prev = pl.BlockSpec((1, ATTN_BLOCK, ATTN_WIDTH),
                        lambda b, r, i: (b, jnp.maximum(i * (tq // ATTN_BLOCK) - 1, 0), r))
    out = jax.ShapeDtypeStruct((batch, L, dilation * ATTN_WIDTH), jnp.float32)
    o, l = pl.pallas_call(
        _attn_kernel,
        grid=(batch, dilation, L // tq),
        in_specs=[cur, prev, cur, prev, cur],
        out_specs=[cur, cur],
        out_shape=[out, out],
        compiler_params=pltpu.CompilerParams(dimension_semantics=("arbitrary",) * 3, vmem_limit_bytes=VMEM_LIMIT),
        name=f"attention_d{dilation}",
    )(view(q), view(k), view(k), view(v), view(v))
    return o.reshape(batch * seq, ATTN_WIDTH), l.reshape(batch * seq, ATTN_WIDTH)


def _out_proj_kernel(o1_ref, o2_ref, o3_ref, l1_ref, l2_ref, l3_ref, gated_ref, x_ref, wout_ref, gattn_ref, x2_ref):
    l1, l2, l3 = l1_ref[...], l2_ref[...], l3_ref[...]
    m = jnp.maximum(jnp.maximum(l1, l2), l3)
    e1, e2, e3 = jnp.exp(l1 - m), jnp.exp(l2 - m), jnp.exp(l3 - m)
    attn = (e1 * o1_ref[...] + e2 * o2_ref[...] + e3 * o3_ref[...]) / (e1 + e2 + e3)
    ms = jnp.mean(attn * attn, axis=-1, keepdims=True)
    attn_n = (attn * lax.rsqrt(ms + RMS_EPS) * gattn_ref[...]).astype(jnp.bfloat16)
    y = _dot(attn_n, wout_ref[0:ATTN_WIDTH, :]) + _dot(gated_ref[...], wout_ref[ATTN_WIDTH:, :])
    x2_ref[...] = x_ref[...] + y


def _out_proj(outs, lses, gated, x2d, wout, gattn):
    n = x2d.shape[0]
    tm = TM_PROJ
    tok = lambda w: pl.BlockSpec((tm, w), lambda i: (i, 0))
    full = lambda shape: pl.BlockSpec(shape, lambda i: (0,) * len(shape))
    return pl.pallas_call(
        _out_proj_kernel,
        grid=(n // tm,),
        in_specs=[tok(ATTN_WIDTH)] * 7 + [tok(D_MODEL), full((D_MODEL, D_MODEL)), full((1, ATTN_WIDTH))],
        out_specs=tok(D_MODEL),
        out_shape=jax.ShapeDtypeStruct((n, D_MODEL), jnp.float32),
        compiler_params=pltpu.CompilerParams(dimension_semantics=("arbitrary",), vmem_limit_bytes=VMEM_LIMIT),
        name="out_proj",
    )(*outs, *lses, gated, x2d, wout, gattn)


CAND_J_COUNT = (16, 8, 5, 4, 3, 2, 2, 2)


def _top16_groups(vals, ids):
    vals = list(vals)
    out_v, out_i = [], []
    for _ in range(PEER_TOPK):
        bv, bi = vals[0], ids[0]
        for v, i in zip(vals[1:], ids[1:]):
            gt = v > bv
            bv = jnp.where(gt, v, bv)
            bi = jnp.where(gt, i, bi)
        for sh in (4, 2, 1):
            rv, ri = pltpu.roll(bv, sh, axis=0), pltpu.roll(bi, sh, axis=0)
            better = (rv > bv) | ((rv == bv) & (ri < bi))
            bv = jnp.where(better, rv, bv)
            bi = jnp.where(better, ri, bi)
        out_v.append(bv)
        out_i.append(bi)
        vals = [jnp.where(i == bi, -jnp.inf, v) for v, i in zip(vals, ids)]
    return out_v, out_i


def _stack(reps, start):
    row = lax.broadcasted_iota(jnp.int32, reps[0].shape, 0)
    out = reps[start + SUBLANES - 1]
    for r in range(SUBLANES - 2, -1, -1):
        out = jnp.where(row == r, reps[start + r], out)
    return out


def _select_experts(sa, sb):
    t = sa.shape[1]
    row = lax.broadcasted_iota(jnp.int32, (SUBLANES, t), 0)
    groups = lambda s: [s[g * SUBLANES:(g + 1) * SUBLANES, :] for g in range(N_KEYS // SUBLANES)]
    key_ids = [row + g * SUBLANES for g in range(N_KEYS // SUBLANES)]
    va, ia = _top16_groups(groups(sa), key_ids)
    vb, ib = _top16_groups(groups(sb), key_ids)
    vb_lo, vb_hi, va_hi = _stack(vb, 0), _stack(vb, SUBLANES), _stack(va, SUBLANES)
    cand = [va[0] + vb_lo, va[0] + vb_hi]
    cand_ids = [row, row + SUBLANES]
    for i in range(1, SUBLANES):
        cand.append(jnp.where(row < CAND_J_COUNT[i], va[i] + vb_lo, -jnp.inf))
        cand_ids.append(row + i * PEER_TOPK)
    cand.append(va_hi + vb[0])
    cand_ids.append((row + SUBLANES) * PEER_TOPK)
    top_s, pos = _top16_groups(cand, cand_ids)
    idx_halves, e_halves = [], []
    for h in range(2):
        p = _stack(pos, h * SUBLANES)
        pa, pb = p >> 4, p & (PEER_TOPK - 1)
        ea, eb = jnp.zeros_like(p), jnp.zeros_like(p)
        for i in range(PEER_TOPK):
            ea = jnp.where(pa == i, ia[i], ea)
            eb = jnp.where(pb == i, ib[i], eb)
        idx_halves.append((ea * N_KEYS + eb) * TABLE_ROWS_PER_EXPERT)
        e_halves.append(jnp.exp(_stack(top_s, h * SUBLANES) - top_s[0]))
    den = e_halves[0] + e_halves[1]
    for sh in (4, 2, 1):
        den = den + pltpu.roll(den, sh, axis=0)
    return jnp.concatenate(idx_halves, axis=0), jnp.concatenate([e / den for e in e_halves], axis=0)


def _route_kernel(x2_ref, gffn_ref, wq_ref, ka_ref, kb_ref, xn_ref, idx_ref, gate_ref):
    x = x2_ref[...]
    ms = jnp.mean(x * x, axis=-1, keepdims=True)
    xn = x * lax.rsqrt(ms + RMS_EPS) * gffn_ref[...]
    xn_ref[...] = xn
    qh = _dot(xn.astype(jnp.bfloat16), wq_ref[...])
    half = D_KEY // 2
    for lt in range(x.shape[0] // LANES):
        ts = slice(lt * LANES, (lt + 1) * LANES)
        idx_rows, gate_rows = [], []
        for h in range(PEER_HEADS):
            qa = qh[ts, h * D_KEY:h * D_KEY + half].astype(jnp.bfloat16)
            qb = qh[ts, h * D_KEY + half:(h + 1) * D_KEY].astype(jnp.bfloat16)
            idx_h, gate_h = _select_experts(_dot_nt(ka_ref[...], qa), _dot_nt(kb_ref[...], qb))
            idx_rows.append(idx_h)
            gate_rows.append(gate_h)
        idx_ref[ts, :] = jnp.concatenate(idx_rows, axis=0).T
        gate_ref[ts, :] = jnp.concatenate(gate_rows, axis=0).T


def _peer_route(x2, gffn, wq, ka, kb):
    n = x2.shape[0]
    tm = TM_ROUTE
    tok = lambda w: pl.BlockSpec((tm, w), lambda i: (i, 0))
    full = lambda shape: pl.BlockSpec(shape, lambda i: (0,) * len(shape))
    return pl.pallas_call(
        _route_kernel,
        grid=(n // tm,),
        in_specs=[tok(D_MODEL), full((1, D_MODEL)), full((D_MODEL, PEER_HEADS * D_KEY)),
                  full((N_KEYS, D_KEY // 2)), full((N_KEYS, D_KEY // 2))],
        out_specs=[tok(D_MODEL), tok(NJ), tok(NJ)],
        out_shape=[jax.ShapeDtypeStruct((n, D_MODEL), jnp.float32),
                   jax.ShapeDtypeStruct((n, NJ), jnp.int32),
                   jax.ShapeDtypeStruct((n, NJ), jnp.float32)],
        compiler_params=pltpu.CompilerParams(dimension_semantics=("arbitrary",), vmem_limit_bytes=VMEM_LIMIT),
        name="peer_route",
    )(x2, gffn, wq, ka, kb)


def _pack_table(tab):
    bits = lax.bitcast_convert_type(tab.astype(jnp.bfloat16), jnp.uint16).astype(jnp.uint32)
    half = D_MODEL // 2
    word = bits[:, :half] | (bits[:, half:] << 16)
    return word.reshape(tab.shape[0] * TABLE_ROWS_PER_EXPERT, LANES)


def _unpack(w):
    lo = lax.bitcast_convert_type(w << 16, jnp.float32)
    hi = lax.bitcast_convert_type(w & jnp.uint32(0xFFFF0000), jnp.float32)
    return lo, hi


def _gather_rows(idx_ref, base, tab_ref, w_ref):
    for g in range(NJ // SUBLANES):
        window = idx_ref.at[pl.ds(base + g * SUBLANES, SUBLANES)]
        for r in range(SUBLANES):
            off = pl.multiple_of(window[r], TABLE_ROWS_PER_EXPERT)
            w_ref[pl.ds(g * 32 + r, TABLE_ROWS_PER_EXPERT, stride=SUBLANES), :] = (
                tab_ref[pl.ds(off, TABLE_ROWS_PER_EXPERT), :])


def _peer_up_kernel(idx_ref, gate_ref, xn_ref, tab_ref, coef_ref, w_ref, pa_ref, pb_ref, act_ref, *, tb):
    def partial_dots(t, p_ref):
        _gather_rows(idx_ref, t * NJ, tab_ref, w_ref)
        xt = xn_ref[t]
        xb = [jnp.broadcast_to(xt[c:c + 1, :], (SUBLANES, LANES)) for c in range(SUBLANES)]
        for g in range(NJ // SUBLANES):
            acc = None
            for s in range(4):
                lo, hi = _unpack(w_ref[pl.ds(g * 32 + s * SUBLANES, SUBLANES), :])
                term = lo * xb[s] + hi * xb[4 + s]
                acc = term if acc is None else acc + term
            p_ref[pl.ds(g * SUBLANES, SUBLANES), :] = acc

    def lane_sums(p_ref, t):
        act_ref[pl.ds(t, 1), :] = jnp.sum(p_ref[...].T, axis=0, keepdims=True)

    pb_ref[...] = jnp.zeros_like(pb_ref)

    def body(i, carry):
        t0 = 2 * i
        lane_sums(pb_ref, jnp.maximum(t0 - 1, 0))
        partial_dots(t0, pa_ref)
        lane_sums(pa_ref, t0)
        partial_dots(t0 + 1, pb_ref)
        return carry

    lax.fori_loop(0, tb // 2, body, 0)
    lane_sums(pb_ref, tb - 1)
    coef_ref[...] = gate_ref[...] * _gelu(act_ref[...])


def _peer_down_kernel(idx_ref, coef_ref, x2_ref, tab_ref, o_ref, w_ref, ca_ref, cb_ref, *, tb):
    def spread(t, c_ref):
        c_ref[...] = jnp.broadcast_to(coef_ref[pl.ds(t, 1), :], (NJ, LANES)).T

    def weighted_sum(t, c_ref):
        _gather_rows(idx_ref, t * NJ, tab_ref, w_ref)
        acc_lo, acc_hi = [None] * 4, [None] * 4
        for g in range(NJ // SUBLANES):
            cg = c_ref[pl.ds(g * SUBLANES, SUBLANES), :]
            for s in range(4):
                lo, hi = _unpack(w_ref[pl.ds(g * 32 + s * SUBLANES, SUBLANES), :])
                acc_lo[s] = lo * cg if acc_lo[s] is None else acc_lo[s] + lo * cg
                acc_hi[s] = hi * cg if acc_hi[s] is None else acc_hi[s] + hi * cg
        rows = [jnp.sum(a, axis=0, keepdims=True) for a in acc_lo + acc_hi]
        o_ref[t] = x2_ref[t] + jnp.concatenate(rows, axis=0)

    spread(0, ca_ref)

    def body(i, carry):
        t0 = 2 * i
        spread(t0 + 1, cb_ref)
        weighted_sum(t0, ca_ref)
        spread(jnp.minimum(t0 + 2, tb - 1), ca_ref)
        weighted_sum(t0 + 1, cb_ref)
        return carry

    lax.fori_loop(0, tb // 2, body, 0)


def _peer_specs(n, table_rows, tb):
    idx_spec = pl.BlockSpec((tb * NJ,), lambda i: (i,), memory_space=pltpu.SMEM)
    row_spec = pl.BlockSpec((tb, NJ), lambda i: (i, 0))
    tok_spec = pl.BlockSpec((tb, SUBLANES, LANES), lambda i: (i, 0, 0))
    tab_spec = pl.BlockSpec((table_rows, LANES), lambda i: (0, 0), pipeline_mode=pl.Buffered(1))
    return idx_spec, row_spec, tok_spec, tab_spec


_W_SCRATCH = pltpu.VMEM((NJ * TABLE_ROWS_PER_EXPERT, LANES), jnp.uint32)
_SQUARE_SCRATCH = pltpu.VMEM((NJ, LANES), jnp.float32)


def _peer_up(idx, gate, xn, tab):
    n, tb = xn.shape[0], TB_PEER
    idx_spec, row_spec, tok_spec, tab_spec = _peer_specs(n, tab.shape[0], tb)
    return pl.pallas_call(
        functools.partial(_peer_up_kernel, tb=tb),
        grid=(n // tb,),
        in_specs=[idx_spec, row_spec, tok_spec, tab_spec],
        out_specs=row_spec,
        out_shape=jax.ShapeDtypeStruct((n, NJ), jnp.float32),
        scratch_shapes=[_W_SCRATCH, _SQUARE_SCRATCH, _SQUARE_SCRATCH, pltpu.VMEM((tb, NJ), jnp.float32)],
        compiler_params=pltpu.CompilerParams(dimension_semantics=("arbitrary",), vmem_limit_bytes=VMEM_LIMIT),
        name="peer_up",
    )(idx.reshape(-1), gate, xn.reshape(n, SUBLANES, LANES), tab)


def _peer_down(idx, coef, x2, tab):
    n, tb = x2.shape[0], TB_PEER
    idx_spec, row_spec, tok_spec, tab_spec = _peer_specs(n, tab.shape[0], tb)
    return pl.pallas_call(
        functools.partial(_peer_down_kernel, tb=tb),
        grid=(n // tb,),
        in_specs=[idx_spec, row_spec, tok_spec, tab_spec],
        out_specs=tok_spec,
        out_shape=jax.ShapeDtypeStruct((n, SUBLANES, LANES), jnp.float32),
        scratch_shapes=[_W_SCRATCH, _SQUARE_SCRATCH, _SQUARE_SCRATCH],
        compiler_params=pltpu.CompilerParams(dimension_semantics=("arbitrary",), vmem_limit_bytes=VMEM_LIMIT),
        name="peer_down",
    )(idx.reshape(-1), coef, x2.reshape(n, SUBLANES, LANES), tab).reshape(n, D_MODEL)


def _layer(x2d, batch, seq, mix_norm_g, w_in, q_norm_g, k_norm_g, v_gate_norm_g, w_spatial, b_spatial, attn_out_g,
           gate_out_g, w_out, ffn_norm_g, w_query, sub_keys_a, sub_keys_b, expert_u, expert_v):
    bf16 = jnp.bfloat16
    heads = ATTN_WIDTH // HEAD_DIM
    lane = jnp.arange(ATTN_WIDTH)
    g64 = (lane[:, None] // HEAD_DIM == lane[None, :] // HEAD_DIM).astype(bf16)
    gq = (jnp.tile(q_norm_g, heads) * HEAD_DIM ** -0.5)[None, :]
    gk = jnp.tile(k_norm_g, heads)[None, :]
    bsp = jnp.repeat(b_spatial.T, GMLP_WIDTH // GMLP_GROUPS, axis=1)
    q, k, v, gated = _in_proj(x2d, mix_norm_g[None, :], w_in.astype(bf16), gq, gk, g64,
                              v_gate_norm_g.reshape(1, GMLP_WIDTH), w_spatial, bsp, gate_out_g[None, :])
    outs, lses = [], []
    for window, dilation in PATTERNS:
        assert window // dilation == N_BACK
        o, l = _attention(q, k, v, batch, seq, dilation)
        outs.append(o)
        lses.append(l)
    x2 = _out_proj(outs, lses, gated, x2d, w_out.astype(bf16), attn_out_g[None, :])
    xn, idx, gate = _peer_route(x2, ffn_norm_g[None, :], w_query.astype(bf16), sub_keys_a.astype(bf16),
                                sub_keys_b.astype(bf16))
    coef = _peer_up(idx, gate, xn, _pack_table(expert_u))
    return _peer_down(idx, coef, x2, _pack_table(expert_v))


def kernel(x, mix_norm_g, w_in, q_norm_g, k_norm_g, v_gate_norm_g, w_spatial, b_spatial, attn_out_g, gate_out_g,
           w_out, ffn_norm_g, w_query, sub_keys_a, sub_keys_b, expert_u, expert_v):
    batch, seq, d = x.shape
    assert d == D_MODEL and seq % (TQ_ATTN * PATTERNS[-1][1]) == 0 and (batch * seq) % TM_PROJ == 0
    x2d = x.reshape(batch * seq, d)
    for l in range(mix_norm_g.shape[0]):
        x2d = _layer(x2d, batch, seq, mix_norm_g[l], w_in[l], q_norm_g[l], k_norm_g[l], v_gate_norm_g[l],
                     w_spatial[l], b_spatial[l], attn_out_g[l], gate_out_g[l], w_out[l], ffn_norm_g[l],
                     w_query[l], sub_keys_a[l], sub_keys_b[l], expert_u[l], expert_v[l])
    return x2d.reshape(batch, seq, d)
```

```python
import functools

import jax
import jax.numpy as jnp
from jax import lax
from jax.experimental import pallas as pl
from jax.experimental.pallas import tpu as pltpu

D_MODEL = 1024
ATTN_WIDTH = 512
HEAD_DIM = 64
HEAD_PAIRS = ATTN_WIDTH // 128
PATTERNS = ((128, 1), (512, 4), (2048, 16))
N_BACK = 128
ATTN_BLOCK = 128
GMLP_WIDTH = 512
GMLP_CHUNK = 128
GMLP_GROUPS = 4
IN_WIDTH = 3 * ATTN_WIDTH + 2 * GMLP_WIDTH
PEER_HEADS = 8
PEER_TOPK = 16
N_KEYS = 128
D_KEY = 256
NJ = PEER_HEADS * PEER_TOPK
TABLE_ROWS_PER_EXPERT = 4
RMS_EPS = 1e-6
NEG = -1e30

LANES = 128
SUBLANES = 8
VMEM_LIMIT = 56 * 2 ** 20

TM_PROJ = 512
TQ_ATTN = 256
TM_ROUTE = 256
TB_PEER = 256


def _gelu(x):
    return 0.5 * x * (1.0 + lax.erf(x * (2.0 ** -0.5)))


def _split_bf16(x):
    hi = x.astype(jnp.bfloat16)
    lo = (x - hi.astype(jnp.float32)).astype(jnp.bfloat16)
    return hi, lo


def _dot(a, b):
    return jnp.dot(a, b, preferred_element_type=jnp.float32)


def _dot_nt(a, b):
    return lax.dot_general(a, b, (((1,), (1,)), ((), ())), preferred_element_type=jnp.float32)


def _store_views(val, stage_ref, out_refs):
    chunks = ATTN_WIDTH // LANES
    for c in range(chunks):
        stage_ref[c] = val[:, c * LANES:(c + 1) * LANES]
    for (_, dilation), o_ref in zip(PATTERNS, out_refs):
        rows = val.shape[0] // dilation
        for r in range(dilation):
            for c in range(chunks):
                lanes = pl.ds(r * ATTN_WIDTH + c * LANES, LANES)
                o_ref[0, :, lanes] = stage_ref[c, pl.ds(r, rows, stride=dilation), :].astype(jnp.bfloat16)


def _in_proj_kernel(x_ref, gmix_ref, win_ref, gq_ref, gk_ref, g64_ref, gvg_ref, wsp_ref, bsp_ref, gout_ref,
                    q1_ref, q2_ref, q3_ref, k1_ref, k2_ref, k3_ref, v1_ref, v2_ref, v3_ref, gated_ref,
                    gs_ref, stage_ref):
    x = x_ref[...]
    ms = jnp.mean(x * x, axis=-1, keepdims=True)
    hn = (x * lax.rsqrt(ms + RMS_EPS) * gmix_ref[...]).astype(jnp.bfloat16)

    def head_norm(t, g):
        hi, lo = _split_bf16(t * t)
        msq = (_dot(hi, g64_ref[...]) + _dot(lo, g64_ref[...])) * (1.0 / HEAD_DIM)
        return t * lax.rsqrt(msq + RMS_EPS) * g

    q = _dot(hn, win_ref[:, 0:ATTN_WIDTH])
    _store_views(head_norm(q, gq_ref[...]), stage_ref, (q1_ref, q2_ref, q3_ref))
    k = _dot(hn, win_ref[:, ATTN_WIDTH:2 * ATTN_WIDTH])
    _store_views(head_norm(k, gk_ref[...]), stage_ref, (k1_ref, k2_ref, k3_ref))
    _store_views(_dot(hn, win_ref[:, 2 * ATTN_WIDTH:3 * ATTN_WIDTH]), stage_ref, (v1_ref, v2_ref, v3_ref))

    u = _gelu(_dot(hn, win_ref[:, 3 * ATTN_WIDTH:3 * ATTN_WIDTH + GMLP_WIDTH]))
    gv = _gelu(_dot(hn, win_ref[:, 3 * ATTN_WIDTH + GMLP_WIDTH:IN_WIDTH]))
    row = lax.broadcasted_iota(jnp.int32, (GMLP_CHUNK, GMLP_CHUNK), 0)
    col = lax.broadcasted_iota(jnp.int32, (GMLP_CHUNK, GMLP_CHUNK), 1)
    causal = col <= row
    n_chunks = x.shape[0] // GMLP_CHUNK
    for g in range(GMLP_GROUPS):
        cs = slice(g * LANES, (g + 1) * LANES)
        vg = gv[:, cs]
        msg = jnp.mean(vg * vg, axis=-1, keepdims=True)
        vn = (vg * lax.rsqrt(msg + RMS_EPS) * gvg_ref[:, cs]).astype(jnp.bfloat16)
        w = jnp.where(causal, wsp_ref[g], 0.0).astype(jnp.bfloat16)
        for c in range(n_chunks):
            rs = slice(c * GMLP_CHUNK, (c + 1) * GMLP_CHUNK)
            z = _dot(w, vn[rs, :]) + bsp_ref[:, cs]
            gs_ref[rs, cs] = u[rs, cs] * z
    gated = gs_ref[...]
    msg = jnp.mean(gated * gated, axis=-1, keepdims=True)
    gated_ref[...] = (gated * lax.rsqrt(msg + RMS_EPS) * gout_ref[...]).astype(jnp.bfloat16)


def _view_spec(seq, tm, dilation):
    blocks_per_row = seq // tm
    return pl.BlockSpec((1, tm // dilation, dilation * ATTN_WIDTH),
                        lambda i: (i // blocks_per_row, i % blocks_per_row, 0))


def _view_shape(batch, seq, dilation, dtype):
    return jax.ShapeDtypeStruct((batch, seq // dilation, dilation * ATTN_WIDTH), dtype)


def _in_proj(x2d, batch, seq, gmix, win, gq, gk, g64, gvg, wsp, bsp, gout):
    n = x2d.shape[0]
    tm = TM_PROJ
    full = lambda shape: pl.BlockSpec(shape, lambda i: (0,) * len(shape))
    tok = lambda w: pl.BlockSpec((tm, w), lambda i: (i, 0))
    view_specs = [_view_spec(seq, tm, d) for _, d in PATTERNS] * 3
    view_shapes = [_view_shape(batch, seq, d, jnp.bfloat16) for _, d in PATTERNS] * 3
    return pl.pallas_call(
        _in_proj_kernel,
        grid=(n // tm,),
        in_specs=[tok(D_MODEL), full((1, D_MODEL)), full((D_MODEL, IN_WIDTH)), full((1, ATTN_WIDTH)),
                  full((1, ATTN_WIDTH)), full((ATTN_WIDTH, ATTN_WIDTH)), full((1, GMLP_WIDTH)),
                  full((GMLP_GROUPS, GMLP_CHUNK, GMLP_CHUNK)), full((GMLP_CHUNK, GMLP_WIDTH)), full((1, GMLP_WIDTH))],
        out_specs=view_specs + [tok(GMLP_WIDTH)],
        out_shape=view_shapes + [jax.ShapeDtypeStruct((n, GMLP_WIDTH), jnp.bfloat16)],
        scratch_shapes=[pltpu.VMEM((tm, GMLP_WIDTH), jnp.float32),
                        pltpu.VMEM((ATTN_WIDTH // LANES, tm, LANES), jnp.float32)],
        compiler_params=pltpu.CompilerParams(dimension_semantics=("arbitrary",), vmem_limit_bytes=VMEM_LIMIT),
        name="in_proj",
    )(x2d, gmix, win, gq, gk, g64, gvg, wsp, bsp, gout)


def _attn_kernel(q_ref, kp_ref, kc_ref, vp_ref, vc_ref, o_ref, l_ref):
    tq = q_ref.shape[1]
    have_prev = pl.program_id(2) > 0
    qi = lax.broadcasted_iota(jnp.int32, (ATTN_BLOCK, 2 * ATTN_BLOCK), 0)
    kj = lax.broadcasted_iota(jnp.int32, (ATTN_BLOCK, 2 * ATTN_BLOCK), 1)
    rel = qi + ATTN_BLOCK - kj
    band = (rel >= 0) & (rel <= N_BACK)
    lane = lax.broadcasted_iota(jnp.int32, (ATTN_BLOCK, LANES), 1)
    for qb in range(tq // ATTN_BLOCK):
        rs = slice(qb * ATTN_BLOCK, (qb + 1) * ATTN_BLOCK)
        if qb == 0:
            mask = band & ((kj >= ATTN_BLOCK) | have_prev)
        else:
            mask = band
        for p in range(HEAD_PAIRS):
            cs = slice(p * LANES, (p + 1) * LANES)
            qp = q_ref[0, rs, cs]
            if qb == 0:
                kprev, vprev = kp_ref[0, :, cs], vp_ref[0, :, cs]
            else:
                ps = slice((qb - 1) * ATTN_BLOCK, qb * ATTN_BLOCK)
                kprev, vprev = kc_ref[0, ps, cs], vc_ref[0, ps, cs]
            keys = jnp.concatenate([kprev, kc_ref[0, rs, cs]], axis=0)
            vals = jnp.concatenate([vprev, vc_ref[0, rs, cs]], axis=0)
            out_pair = jnp.zeros((ATTN_BLOCK, LANES), jnp.float32)
            lse_pair = jnp.zeros((ATTN_BLOCK, LANES), jnp.float32)
            for hh in range(2):
                in_head = (lane >= hh * HEAD_DIM) & (lane < (hh + 1) * HEAD_DIM)
                s = _dot_nt(jnp.where(in_head, qp, jnp.zeros_like(qp)), keys)
                s = jnp.where(mask, s, NEG)
                m = jnp.max(s, axis=-1, keepdims=True)
                e = jnp.exp(s - m)
                den = jnp.sum(e, axis=-1, keepdims=True)
                o = _dot(e.astype(jnp.bfloat16), vals) / den
                out_pair = jnp.where(in_head, o, out_pair)
                lse_pair = jnp.where(in_head, m + jnp.log(den), lse_pair)
            o_ref[0, rs, cs] = out_pair
            l_ref[0, rs, cs] = lse_pair


def _attention(q, k, v, dilation):
    batch, L, _ = q.shape
    tq = min(TQ_ATTN, L)
    cur = pl.BlockSpec((1, tq, ATTN_WIDTH), lambda b, r, i: (b, i, r))
    prev = pl.BlockSpec((1, ATTN_BLOCK, ATTN_WIDTH),
                        lambda b, r, i: (b, jnp.maximum(i * (tq // ATTN_BLOCK) - 1, 0), r))
    out = jax.ShapeDtypeStruct((batch, L, dilation * ATTN_WIDTH), jnp.float32)
    return pl.pallas_call(
        _attn_kernel,
        grid=(batch, dilation, L // tq),
        in_specs=[cur, prev, cur, prev, cur],
        out_specs=[cur, cur],
        out_shape=[out, out],
        compiler_params=pltpu.CompilerParams(dimension_semantics=("arbitrary",) * 3, vmem_limit_bytes=VMEM_LIMIT),
        name=f"attention_d{dilation}",
    )(q, k, k, v, v)


def _load_view(v_ref, stage_ref, dilation):
    if dilation == 1:
        return v_ref[0]
    rows = v_ref.shape[1]
    chunks = ATTN_WIDTH // LANES
    for r in range(dilation):
        for c in range(chunks):
            stage_ref[c, pl.ds(r, rows, stride=dilation), :] = v_ref[0, :, pl.ds(r * ATTN_WIDTH + c * LANES, LANES)]
    return jnp.concatenate([stage_ref[c] for c in range(chunks)], axis=1)


def _out_proj_kernel(o1_ref, o2_ref, o3_ref, l1_ref, l2_ref, l3_ref, gated_ref, x_ref, wout_ref, gattn_ref, x2_ref,
                     so2_ref, so3_ref, sl2_ref, sl3_ref):
    dil = [d for _, d in PATTERNS]
    l1 = _load_view(l1_ref, None, dil[0])
    l2 = _load_view(l2_ref, sl2_ref, dil[1])
    l3 = _load_view(l3_ref, sl3_ref, dil[2])
    o1 = _load_view(o1_ref, None, dil[0])
    o2 = _load_view(o2_ref, so2_ref, dil[1])
    o3 = _load_view(o3_ref, so3_ref, dil[2])
    m = jnp.maximum(jnp.maximum(l1, l2), l3)
    e1, e2, e3 = jnp.exp(l1 - m), jnp.exp(l2 - m), jnp.exp(l3 - m)
    attn = (e1 * o1 + e2 * o2 + e3 * o3) / (e1 + e2 + e3)
    ms = jnp.mean(attn * attn, axis=-1, keepdims=True)
    attn_n = (attn * lax.rsqrt(ms + RMS_EPS) * gattn_ref[...]).astype(jnp.bfloat16)
    y = _dot(attn_n, wout_ref[0:ATTN_WIDTH, :]) + _dot(gated_ref[...], wout_ref[ATTN_WIDTH:, :])
    x2_ref[...] = x_ref[...] + y


def _out_proj(outs, lses, gated, x2d, seq, wout, gattn):
    n = x2d.shape[0]
    tm = TM_PROJ
    tok = lambda w: pl.BlockSpec((tm, w), lambda i: (i, 0))
    full = lambda shape: pl.BlockSpec(shape, lambda i: (0,) * len(shape))
    view_specs = [_view_spec(seq, tm, d) for _, d in PATTERNS] * 2
    stage = pltpu.VMEM((ATTN_WIDTH // LANES, tm, LANES), jnp.float32)
    return pl.pallas_call(
        _out_proj_kernel,
        grid=(n // tm,),
        in_specs=view_specs + [tok(GMLP_WIDTH), tok(D_MODEL), full((D_MODEL, D_MODEL)), full((1, ATTN_WIDTH))],
        out_specs=tok(D_MODEL),
        out_shape=jax.ShapeDtypeStruct((n, D_MODEL), jnp.float32),
        scratch_shapes=[stage] * 4,
        compiler_params=pltpu.CompilerParams(dimension_semantics=("arbitrary",), vmem_limit_bytes=VMEM_LIMIT),
        name="out_proj",
    )(*outs, *lses, gated, x2d, wout, gattn)


CAND_J_COUNT = (16, 8, 5, 4, 3, 2, 2, 2)


def _top16_groups(vals, ids):
    vals = list(vals)
    out_v, out_i = [], []
    for _ in range(PEER_TOPK):
        bv, bi = vals[0], ids[0]
        for v, i in zip(vals[1:], ids[1:]):
            gt = v > bv
            bv = jnp.where(gt, v, bv)
            bi = jnp.where(gt, i, bi)
        for sh in (4, 2, 1):
            rv, ri = pltpu.roll(bv, sh, axis=0), pltpu.roll(bi, sh, axis=0)
            better = (rv > bv) | ((rv == bv) & (ri < bi))
            bv = jnp.where(better, rv, bv)
            bi = jnp.where(better, ri, bi)
        out_v.append(bv)
        out_i.append(bi)
        vals = [jnp.where(i == bi, -jnp.inf, v) for v, i in zip(vals, ids)]
    return out_v, out_i


def _stack(reps, start):
    row = lax.broadcasted_iota(jnp.int32, reps[0].shape, 0)
    out = reps[start + SUBLANES - 1]
    for r in range(SUBLANES - 2, -1, -1):
        out = jnp.where(row == r, reps[start + r], out)
    return out


def _select_experts(sa, sb):
    t = sa.shape[1]
    row = lax.broadcasted_iota(jnp.int32, (SUBLANES, t), 0)
    groups = lambda s: [s[g * SUBLANES:(g + 1) * SUBLANES, :] for g in range(N_KEYS // SUBLANES)]
    key_ids = [row + g * SUBLANES for g in range(N_KEYS // SUBLANES)]
    va, ia = _top16_groups(groups(sa), key_ids)
    vb, ib = _top16_groups(groups(sb), key_ids)
    vb_lo, vb_hi, va_hi = _stack(vb, 0), _stack(vb, SUBLANES), _stack(va, SUBLANES)
    cand = [va[0] + vb_lo, va[0] + vb_hi]
    cand_ids = [row, row + SUBLANES]
    for i in range(1, SUBLANES):
        cand.append(jnp.where(row < CAND_J_COUNT[i], va[i] + vb_lo, -jnp.inf))
        cand_ids.append(row + i * PEER_TOPK)
    cand.append(va_hi + vb[0])
    cand_ids.append((row + SUBLANES) * PEER_TOPK)
    top_s, pos = _top16_groups(cand, cand_ids)
    idx_halves, e_halves = [], []
    for h in range(2):
        p = _stack(pos, h * SUBLANES)
        pa, pb = p >> 4, p & (PEER_TOPK - 1)
        ea, eb = jnp.zeros_like(p), jnp.zeros_like(p)
        for i in range(PEER_TOPK):
            ea = jnp.where(pa == i, ia[i], ea)
            eb = jnp.where(pb == i, ib[i], eb)
        idx_halves.append((ea * N_KEYS + eb) * TABLE_ROWS_PER_EXPERT)
        e_halves.append(jnp.exp(_stack(top_s, h * SUBLANES) - top_s[0]))
    den = e_halves[0] + e_halves[1]
    for sh in (4, 2, 1):
        den = den + pltpu.roll(den, sh, axis=0)
    return jnp.concatenate(idx_halves, axis=0), jnp.concatenate([e / den for e in e_halves], axis=0)


def _route_kernel(x2_ref, gffn_ref, wq_ref, ka_ref, kb_ref, xn_ref, idx_ref, gate_ref):
    x = x2_ref[...]
    ms = jnp.mean(x * x, axis=-1, keepdims=True)
    xn = x * lax.rsqrt(ms + RMS_EPS) * gffn_ref[...]
    xn_ref[...] = xn
    qh = _dot(xn.astype(jnp.bfloat16), wq_ref[...])
    half = D_KEY // 2
    for lt in range(x.shape[0] // LANES):
        ts = slice(lt * LANES, (lt + 1) * LANES)
        idx_rows, gate_rows = [], []
        for h in range(PEER_HEADS):
            qa = qh[ts, h * D_KEY:h * D_KEY + half].astype(jnp.bfloat16)
            qb = qh[ts, h * D_KEY + half:(h + 1) * D_KEY].astype(jnp.bfloat16)
            idx_h, gate_h = _select_experts(_dot_nt(ka_ref[...], qa), _dot_nt(kb_ref[...], qb))
            idx_rows.append(idx_h)
            gate_rows.append(gate_h)
        idx_ref[ts, :] = jnp.concatenate(idx_rows, axis=0).T
        gate_ref[ts, :] = jnp.concatenate(gate_rows, axis=0).T


def _peer_route(x2, gffn, wq, ka, kb):
    n = x2.shape[0]
    tm = TM_ROUTE
    tok = lambda w: pl.BlockSpec((tm, w), lambda i: (i, 0))
    full = lambda shape: pl.BlockSpec(shape, lambda i: (0,) * len(shape))
    return pl.pallas_call(
        _route_kernel,
        grid=(n // tm,),
        in_specs=[tok(D_MODEL), full((1, D_MODEL)), full((D_MODEL, PEER_HEADS * D_KEY)),
                  full((N_KEYS, D_KEY // 2)), full((N_KEYS, D_KEY // 2))],
        out_specs=[tok(D_MODEL), tok(NJ), tok(NJ)],
        out_shape=[jax.ShapeDtypeStruct((n, D_MODEL), jnp.float32),
                   jax.ShapeDtypeStruct((n, NJ), jnp.int32),
                   jax.ShapeDtypeStruct((n, NJ), jnp.float32)],
        compiler_params=pltpu.CompilerParams(dimension_semantics=("arbitrary",), vmem_limit_bytes=VMEM_LIMIT),
        name="peer_route",
    )(x2, gffn, wq, ka, kb)


def _pack_table(tab):
    bits = lax.bitcast_convert_type(tab.astype(jnp.bfloat16), jnp.uint16).astype(jnp.uint32)
    half = D_MODEL // 2
    word = bits[:, :half] | (bits[:, half:] << 16)
    return word.reshape(tab.shape[0] * TABLE_ROWS_PER_EXPERT, LANES)


def _unpack(w):
    lo = lax.bitcast_convert_type(w << 16, jnp.float32)
    hi = lax.bitcast_convert_type(w & jnp.uint32(0xFFFF0000), jnp.float32)
    return lo, hi


def _gather_rows(idx_ref, base, tab_ref, w_ref):
    for g in range(NJ // SUBLANES):
        window = idx_ref.at[pl.ds(base + g * SUBLANES, SUBLANES)]
        for r in range(SUBLANES):
            off = pl.multiple_of(window[r], TABLE_ROWS_PER_EXPERT)
            w_ref[pl.ds(g * 32 + r, TABLE_ROWS_PER_EXPERT, stride=SUBLANES), :] = (
                tab_ref[pl.ds(off, TABLE_ROWS_PER_EXPERT), :])


def _peer_up_kernel(idx_ref, gate_ref, xn_ref, tab_ref, coef_ref, w_ref, pa_ref, pb_ref, act_ref, *, tb):
    def partial_dots(t, p_ref):
        _gather_rows(idx_ref, t * NJ, tab_ref, w_ref)
        xt = xn_ref[t]
        xb = [jnp.broadcast_to(xt[c:c + 1, :], (SUBLANES, LANES)) for c in range(SUBLANES)]
        for g in range(NJ // SUBLANES):
            acc = None
            for s in range(4):
                lo, hi = _unpack(w_ref[pl.ds(g * 32 + s * SUBLANES, SUBLANES), :])
                term = lo * xb[s] + hi * xb[4 + s]
                acc = term if acc is None else acc + term
            p_ref[pl.ds(g * SUBLANES, SUBLANES), :] = acc

    def lane_sums(p_ref, t):
        act_ref[pl.ds(t, 1), :] = jnp.sum(p_ref[...].T, axis=0, keepdims=True)

    pb_ref[...] = jnp.zeros_like(pb_ref)

    def body(i, carry):
        t0 = 2 * i
        lane_sums(pb_ref, jnp.maximum(t0 - 1, 0))
        partial_dots(t0, pa_ref)
        lane_sums(pa_ref, t0)
        partial_dots(t0 + 1, pb_ref)
        return carry

    lax.fori_loop(0, tb // 2, body, 0)
    lane_sums(pb_ref, tb - 1)
    coef_ref[...] = gate_ref[...] * _gelu(act_ref[...])


def _peer_down_kernel(idx_ref, coef_ref, x2_ref, tab_ref, o_ref, w_ref, ca_ref, cb_ref, *, tb):
    def spread(t, c_ref):
        c_ref[...] = jnp.broadcast_to(coef_ref[pl.ds(t, 1), :], (NJ, LANES)).T

    def weighted_sum(t, c_ref):
        _gather_rows(idx_ref, t * NJ, tab_ref, w_ref)
        acc_lo, acc_hi = [None] * 4, [None] * 4
        for g in range(NJ // SUBLANES):
            cg = c_ref[pl.ds(g * SUBLANES, SUBLANES), :]
            for s in range(4):
                lo, hi = _unpack(w_ref[pl.ds(g * 32 + s * SUBLANES, SUBLANES), :])
                acc_lo[s] = lo * cg if acc_lo[s] is None else acc_lo[s] + lo * cg
                acc_hi[s] = hi * cg if acc_hi[s] is None else acc_hi[s] + hi * cg
        rows = [jnp.sum(a, axis=0, keepdims=True) for a in acc_lo + acc_hi]
        o_ref[t] = x2_ref[t] + jnp.concatenate(rows, axis=0)

    spread(0, ca_ref)

    def body(i, carry):
        t0 = 2 * i
        spread(t0 + 1, cb_ref)
        weighted_sum(t0, ca_ref)
        spread(jnp.minimum(t0 + 2, tb - 1), ca_ref)
        weighted_sum(t0 + 1, cb_ref)
        return carry

    lax.fori_loop(0, tb // 2, body, 0)


def _peer_specs(n, table_rows, tb):
    idx_spec = pl.BlockSpec((tb * NJ,), lambda i: (i,), memory_space=pltpu.SMEM)
    row_spec = pl.BlockSpec((tb, NJ), lambda i: (i, 0))
    tok_spec = pl.BlockSpec((tb, SUBLANES, LANES), lambda i: (i, 0, 0))
    tab_spec = pl.BlockSpec((table_rows, LANES), lambda i: (0, 0), pipeline_mode=pl.Buffered(1))
    return idx_spec, row_spec, tok_spec, tab_spec


_W_SCRATCH = pltpu.VMEM((NJ * TABLE_ROWS_PER_EXPERT, LANES), jnp.uint32)
_SQUARE_SCRATCH = pltpu.VMEM((NJ, LANES), jnp.float32)


def _peer_up(idx, gate, xn, tab):
    n, tb = xn.shape[0], TB_PEER
    idx_spec, row_spec, tok_spec, tab_spec = _peer_specs(n, tab.shape[0], tb)
    return pl.pallas_call(
        functools.partial(_peer_up_kernel, tb=tb),
        grid=(n // tb,),
        in_specs=[idx_spec, row_spec, tok_spec, tab_spec],
        out_specs=row_spec,
        out_shape=jax.ShapeDtypeStruct((n, NJ), jnp.float32),
        scratch_shapes=[_W_SCRATCH, _SQUARE_SCRATCH, _SQUARE_SCRATCH, pltpu.VMEM((tb, NJ), jnp.float32)],
        compiler_params=pltpu.CompilerParams(dimension_semantics=("arbitrary",), vmem_limit_bytes=VMEM_LIMIT),
        name="peer_up",
    )(idx.reshape(-1), gate, xn.reshape(n, SUBLANES, LANES), tab)


def _peer_down(idx, coef, x2, tab):
    n, tb = x2.shape[0], TB_PEER
    idx_spec, row_spec, tok_spec, tab_spec = _peer_specs(n, tab.shape[0], tb)
    return pl.pallas_call(
        functools.partial(_peer_down_kernel, tb=tb),
        grid=(n // tb,),
        in_specs=[idx_spec, row_spec, tok_spec, tab_spec],
        out_specs=tok_spec,
        out_shape=jax.ShapeDtypeStruct((n, SUBLANES, LANES), jnp.float32),
        scratch_shapes=[_W_SCRATCH, _SQUARE_SCRATCH, _SQUARE_SCRATCH],
        compiler_params=pltpu.CompilerParams(dimension_semantics=("arbitrary",), vmem_limit_bytes=VMEM_LIMIT),
        name="peer_down",
    )(idx.reshape(-1), coef, x2.reshape(n, SUBLANES, LANES), tab).reshape(n, D_MODEL)


def _layer(x2d, batch, seq, mix_norm_g, w_in, q_norm_g, k_norm_g, v_gate_norm_g, w_spatial, b_spatial, attn_out_g,
           gate_out_g, w_out, ffn_norm_g, w_query, sub_keys_a, sub_keys_b, expert_u, expert_v):
    bf16 = jnp.bfloat16
    heads = ATTN_WIDTH // HEAD_DIM
    lane = jnp.arange(ATTN_WIDTH)
    g64 = (lane[:, None] // HEAD_DIM == lane[None, :] // HEAD_DIM).astype(bf16)
    gq = (jnp.tile(q_norm_g, heads) * HEAD_DIM ** -0.5)[None, :]
    gk = jnp.tile(k_norm_g, heads)[None, :]
    bsp = jnp.repeat(b_spatial.T, GMLP_WIDTH // GMLP_GROUPS, axis=1)
    *qkv, gated = _in_proj(x2d, batch, seq, mix_norm_g[None, :], w_in.astype(bf16), gq, gk, g64,
                           v_gate_norm_g.reshape(1, GMLP_WIDTH), w_spatial, bsp, gate_out_g[None, :])
    n_pat = len(PATTERNS)
    outs, lses = [], []
    for p, (window, dilation) in enumerate(PATTERNS):
        assert window // dilation == N_BACK
        o, l = _attention(qkv[p], qkv[n_pat + p], qkv[2 * n_pat + p], dilation)
        outs.append(o)
        lses.append(l)
    x2 = _out_proj(outs, lses, gated, x2d, seq, w_out.astype(bf16), attn_out_g[None, :])
    xn, idx, gate = _peer_route(x2, ffn_norm_g[None, :], w_query.astype(bf16), sub_keys_a.astype(bf16),
                                sub_keys_b.astype(bf16))
    coef = _peer_up(idx, gate, xn, _pack_table(expert_u))
    return _peer_down(idx, coef, x2, _pack_table(expert_v))


def kernel(x, mix_norm_g, w_in, q_norm_g, k_norm_g, v_gate_norm_g, w_spatial, b_spatial, attn_out_g, gate_out_g,
           w_out, ffn_norm_g, w_query, sub_keys_a, sub_keys_b, expert_u, expert_v):
    batch, seq, d = x.shape
    assert d == D_MODEL and seq % (TQ_ATTN * PATTERNS[-1][1]) == 0 and (batch * seq) % TM_PROJ == 0
    x2d = x.reshape(batch * seq, d)
    for l in range(mix_norm_g.shape[0]):
        x2d = _layer(x2d, batch, seq, mix_norm_g[l], w_in[l], q_norm_g[l], k_norm_g[l], v_gate_norm_g[l],
                     w_spatial[l], b_spatial[l], attn_out_g[l], gate_out_g[l], w_out[l], ffn_norm_g[l],
                     w_query[l], sub_keys_a[l], sub_keys_b[l], expert_u[l], expert_v[l])
    return x2d.reshape(batch, seq, d)
```

```python
import functools

import jax
import jax.numpy as jnp
from jax import lax
from jax.experimental import pallas as pl
from jax.experimental.pallas import tpu as pltpu

D_MODEL = 1024
ATTN_WIDTH = 512
HEAD_DIM = 64
HEAD_PAIRS = ATTN_WIDTH // 128
PATTERNS = ((128, 1), (512, 4), (2048, 16))
N_BACK = 128
ATTN_BLOCK = 128
GMLP_WIDTH = 512
GMLP_CHUNK = 128
GMLP_GROUPS = 4
IN_WIDTH = 3 * ATTN_WIDTH + 2 * GMLP_WIDTH
PEER_HEADS = 8
PEER_TOPK = 16
N_KEYS = 128
D_KEY = 256
NJ = PEER_HEADS * PEER_TOPK
TABLE_ROWS_PER_EXPERT = 4
RMS_EPS = 1e-6
NEG = -1e30

LANES = 128
SUBLANES = 8
VMEM_LIMIT = 56 * 2 ** 20

TM_PROJ = 512
TQ_ATTN = 512
TM_ROUTE = 256
TB_PEER = 256


def _gelu(x):
    return 0.5 * x * (1.0 + lax.erf(x * (2.0 ** -0.5)))


def _split_bf16(x):
    hi = x.astype(jnp.bfloat16)
    lo = (x - hi.astype(jnp.float32)).astype(jnp.bfloat16)
    return hi, lo


def _dot(a, b):
    return jnp.dot(a, b, preferred_element_type=jnp.float32)


def _dot_nt(a, b):
    return lax.dot_general(a, b, (((1,), (1,)), ((), ())), preferred_element_type=jnp.float32)


def _store_views(val, stage_ref, out_refs):
    chunks = ATTN_WIDTH // LANES
    for c in range(chunks):
        stage_ref[c] = val[:, c * LANES:(c + 1) * LANES]
    for (_, dilation), o_ref in zip(PATTERNS, out_refs):
        rows = val.shape[0] // dilation
        for r in range(dilation):
            for c in range(chunks):
                lanes = pl.ds(r * ATTN_WIDTH + c * LANES, LANES)
                o_ref[0, :, lanes] = stage_ref[c, pl.ds(r, rows, stride=dilation), :].astype(jnp.bfloat16)


def _in_proj_kernel(x_ref, gmix_ref, win_ref, gq_ref, gk_ref, g64_ref, gvg_ref, wsp_ref, bsp_ref, gout_ref,
                    q1_ref, q2_ref, q3_ref, k1_ref, k2_ref, k3_ref, v1_ref, v2_ref, v3_ref, gated_ref,
                    gs_ref, stage_ref):
    x = x_ref[...]
    ms = jnp.mean(x * x, axis=-1, keepdims=True)
    hn = (x * lax.rsqrt(ms + RMS_EPS) * gmix_ref[...]).astype(jnp.bfloat16)

    def head_norm(t, g):
        hi, lo = _split_bf16(t * t)
        msq = (_dot(hi, g64_ref[...]) + _dot(lo, g64_ref[...])) * (1.0 / HEAD_DIM)
        return t * lax.rsqrt(msq + RMS_EPS) * g

    q = _dot(hn, win_ref[:, 0:ATTN_WIDTH])
    _store_views(head_norm(q, gq_ref[...]), stage_ref, (q1_ref, q2_ref, q3_ref))
    k = _dot(hn, win_ref[:, ATTN_WIDTH:2 * ATTN_WIDTH])
    _store_views(head_norm(k, gk_ref[...]), stage_ref, (k1_ref, k2_ref, k3_ref))
    _store_views(_dot(hn, win_ref[:, 2 * ATTN_WIDTH:3 * ATTN_WIDTH]), stage_ref, (v1_ref, v2_ref, v3_ref))

    u = _gelu(_dot(hn, win_ref[:, 3 * ATTN_WIDTH:3 * ATTN_WIDTH + GMLP_WIDTH]))
    gv = _gelu(_dot(hn, win_ref[:, 3 * ATTN_WIDTH + GMLP_WIDTH:IN_WIDTH]))
    row = lax.broadcasted_iota(jnp.int32, (GMLP_CHUNK, GMLP_CHUNK), 0)
    col = lax.broadcasted_iota(jnp.int32, (GMLP_CHUNK, GMLP_CHUNK), 1)
    causal = col <= row
    n_chunks = x.shape[0] // GMLP_CHUNK
    for g in range(GMLP_GROUPS):
        cs = slice(g * LANES, (g + 1) * LANES)
        vg = gv[:, cs]
        msg = jnp.mean(vg * vg, axis=-1, keepdims=True)
        vn = (vg * lax.rsqrt(msg + RMS_EPS) * gvg_ref[:, cs]).astype(jnp.bfloat16)
        w = jnp.where(causal, wsp_ref[g], 0.0).astype(jnp.bfloat16)
        for c in range(n_chunks):
            rs = slice(c * GMLP_CHUNK, (c + 1) * GMLP_CHUNK)
            z = _dot(w, vn[rs, :]) + bsp_ref[:, cs]
            gs_ref[rs, cs] = u[rs, cs] * z
    gated = gs_ref[...]
    msg = jnp.mean(gated * gated, axis=-1, keepdims=True)
    gated_ref[...] = (gated * lax.rsqrt(msg + RMS_EPS) * gout_ref[...]).astype(jnp.bfloat16)


def _view_spec(seq, tm, dilation):
    blocks_per_row = seq // tm
    return pl.BlockSpec((1, tm // dilation, dilation * ATTN_WIDTH),
                        lambda i: (i // blocks_per_row, i % blocks_per_row, 0))


def _view_shape(batch, seq, dilation, dtype):
    return jax.ShapeDtypeStruct((batch, seq // dilation, dilation * ATTN_WIDTH), dtype)


def _in_proj(x2d, batch, seq, gmix, win, gq, gk, g64, gvg, wsp, bsp, gout):
    n = x2d.shape[0]
    tm = TM_PROJ
    full = lambda shape: pl.BlockSpec(shape, lambda i: (0,) * len(shape))
    tok = lambda w: pl.BlockSpec((tm, w), lambda i: (i, 0))
    view_specs = [_view_spec(seq, tm, d) for _, d in PATTERNS] * 3
    view_shapes = [_view_shape(batch, seq, d, jnp.bfloat16) for _, d in PATTERNS] * 3
    return pl.pallas_call(
        _in_proj_kernel,
        grid=(n // tm,),
        in_specs=[tok(D_MODEL), full((1, D_MODEL)), full((D_MODEL, IN_WIDTH)), full((1, ATTN_WIDTH)),
                  full((1, ATTN_WIDTH)), full((ATTN_WIDTH, ATTN_WIDTH)), full((1, GMLP_WIDTH)),
                  full((GMLP_GROUPS, GMLP_CHUNK, GMLP_CHUNK)), full((GMLP_CHUNK, GMLP_WIDTH)), full((1, GMLP_WIDTH))],
        out_specs=view_specs + [tok(GMLP_WIDTH)],
        out_shape=view_shapes + [jax.ShapeDtypeStruct((n, GMLP_WIDTH), jnp.bfloat16)],
        scratch_shapes=[pltpu.VMEM((tm, GMLP_WIDTH), jnp.float32),
                        pltpu.VMEM((ATTN_WIDTH // LANES, tm, LANES), jnp.float32)],
        compiler_params=pltpu.CompilerParams(dimension_semantics=("arbitrary",), vmem_limit_bytes=VMEM_LIMIT),
        name="in_proj",
    )(x2d, gmix, win, gq, gk, g64, gvg, wsp, bsp, gout)


def _attn_kernel(q_ref, kp_ref, kc_ref, vp_ref, vc_ref, o_ref, l_ref):
    tq = q_ref.shape[1]
    have_prev = pl.program_id(2) > 0
    qi = lax.broadcasted_iota(jnp.int32, (ATTN_BLOCK, 2 * ATTN_BLOCK), 0)
    kj = lax.broadcasted_iota(jnp.int32, (ATTN_BLOCK, 2 * ATTN_BLOCK), 1)
    rel = qi + ATTN_BLOCK - kj
    band = (rel >= 0) & (rel <= N_BACK)
    lane = lax.broadcasted_iota(jnp.int32, (ATTN_BLOCK, LANES), 1)
    for qb in range(tq // ATTN_BLOCK):
        rs = slice(qb * ATTN_BLOCK, (qb + 1) * ATTN_BLOCK)
        if qb == 0:
            mask = band & ((kj >= ATTN_BLOCK) | have_prev)
        else:
            mask = band
        for p in range(HEAD_PAIRS):
            cs = slice(p * LANES, (p + 1) * LANES)
            qp = q_ref[0, rs, cs]
            if qb == 0:
                kprev, vprev = kp_ref[0, :, cs], vp_ref[0, :, cs]
            else:
                ps = slice((qb - 1) * ATTN_BLOCK, qb * ATTN_BLOCK)
                kprev, vprev = kc_ref[0, ps, cs], vc_ref[0, ps, cs]
            keys = jnp.concatenate([kprev, kc_ref[0, rs, cs]], axis=0)
            vals = jnp.concatenate([vprev, vc_ref[0, rs, cs]], axis=0)
            out_pair = jnp.zeros((ATTN_BLOCK, LANES), jnp.float32)
            lse_pair = jnp.zeros((ATTN_BLOCK, LANES), jnp.float32)
            for hh in range(2):
                in_head = (lane >= hh * HEAD_DIM) & (lane < (hh + 1) * HEAD_DIM)
                s = _dot_nt(jnp.where(in_head, qp, jnp.zeros_like(qp)), keys)
                s = jnp.where(mask, s, NEG)
                m = jnp.max(s, axis=-1, keepdims=True)
                e = jnp.exp(s - m)
                den = jnp.sum(e, axis=-1, keepdims=True)
                o = _dot(e.astype(jnp.bfloat16), vals) / den
                out_pair = jnp.where(in_head, o, out_pair)
                lse_pair = jnp.where(in_head, m + jnp.log(den), lse_pair)
            o_ref[0, rs, cs] = out_pair
            l_ref[0, rs, cs] = lse_pair


def _attention(q, k, v, dilation):
    batch, L, _ = q.shape
    tq = min(TQ_ATTN, L)
    cur = pl.BlockSpec((1, tq, ATTN_WIDTH), lambda b, r, i: (b, i, r))
    prev = pl.BlockSpec((1, ATTN_BLOCK, ATTN_WIDTH),
                        lambda b, r, i: (b, jnp.maximum(i * (tq // ATTN_BLOCK) - 1, 0), r))
    out = jax.ShapeDtypeStruct((batch, L, dilation * ATTN_WIDTH), jnp.float32)
    return pl.pallas_call(
        _attn_kernel,
        grid=(batch, dilation, L // tq),
        in_specs=[cur, prev, cur, prev, cur],
        out_specs=[cur, cur],
        out_shape=[out, out],
        compiler_params=pltpu.CompilerParams(dimension_semantics=("arbitrary",) * 3, vmem_limit_bytes=VMEM_LIMIT),
        name=f"attention_d{dilation}",
    )(q, k, k, v, v)


def _load_view(v_ref, stage_ref, dilation):
    if dilation == 1:
        return v_ref[0]
    rows = v_ref.shape[1]
    chunks = ATTN_WIDTH // LANES
    for r in range(dilation):
        for c in range(chunks):
            stage_ref[c, pl.ds(r, rows, stride=dilation), :] = v_ref[0, :, pl.ds(r * ATTN_WIDTH + c * LANES, LANES)]
    return jnp.concatenate([stage_ref[c] for c in range(chunks)], axis=1)


def _out_proj_kernel(o1_ref, o2_ref, o3_ref, l1_ref, l2_ref, l3_ref, gated_ref, x_ref, wout_ref, gattn_ref, x2_ref,
                     so2_ref, so3_ref, sl2_ref, sl3_ref):
    dil = [d for _, d in PATTERNS]
    l1 = _load_view(l1_ref, None, dil[0])
    l2 = _load_view(l2_ref, sl2_ref, dil[1])
    l3 = _load_view(l3_ref, sl3_ref, dil[2])
    o1 = _load_view(o1_ref, None, dil[0])
    o2 = _load_view(o2_ref, so2_ref, dil[1])
    o3 = _load_view(o3_ref, so3_ref, dil[2])
    m = jnp.maximum(jnp.maximum(l1, l2), l3)
    e1, e2, e3 = jnp.exp(l1 - m), jnp.exp(l2 - m), jnp.exp(l3 - m)
    attn = (e1 * o1 + e2 * o2 + e3 * o3) / (e1 + e2 + e3)
    ms = jnp.mean(attn * attn, axis=-1, keepdims=True)
    attn_n = (attn * lax.rsqrt(ms + RMS_EPS) * gattn_ref[...]).astype(jnp.bfloat16)
    y = _dot(attn_n, wout_ref[0:ATTN_WIDTH, :]) + _dot(gated_ref[...], wout_ref[ATTN_WIDTH:, :])
    x2_ref[...] = x_ref[...] + y


def _out_proj(outs, lses, gated, x2d, seq, wout, gattn):
    n = x2d.shape[0]
    tm = TM_PROJ
    tok = lambda w: pl.BlockSpec((tm, w), lambda i: (i, 0))
    full = lambda shape: pl.BlockSpec(shape, lambda i: (0,) * len(shape))
    view_specs = [_view_spec(seq, tm, d) for _, d in PATTERNS] * 2
    stage = pltpu.VMEM((ATTN_WIDTH // LANES, tm, LANES), jnp.float32)
    return pl.pallas_call(
        _out_proj_kernel,
        grid=(n // tm,),
        in_specs=view_specs + [tok(GMLP_WIDTH), tok(D_MODEL), full((D_MODEL, D_MODEL)), full((1, ATTN_WIDTH))],
        out_specs=tok(D_MODEL),
        out_shape=jax.ShapeDtypeStruct((n, D_MODEL), jnp.float32),
        scratch_shapes=[stage] * 4,
        compiler_params=pltpu.CompilerParams(dimension_semantics=("arbitrary",), vmem_limit_bytes=VMEM_LIMIT),
        name="out_proj",
    )(*outs, *lses, gated, x2d, wout, gattn)


CAND_J_COUNT = (16, 8, 5, 4, 3, 2, 2, 2)


def _top16_groups(vals, ids):
    vals = list(vals)
    out_v, out_i = [], []
    for _ in range(PEER_TOPK):
        bv, bi = vals[0], ids[0]
        for v, i in zip(vals[1:], ids[1:]):
            gt = v > bv
            bv = jnp.where(gt, v, bv)
            bi = jnp.where(gt, i, bi)
        for sh in (4, 2, 1):
            rv, ri = pltpu.roll(bv, sh, axis=0), pltpu.roll(bi, sh, axis=0)
            better = (rv > bv) | ((rv == bv) & (ri < bi))
            bv = jnp.where(better, rv, bv)
            bi = jnp.where(better, ri, bi)
        out_v.append(bv)
        out_i.append(bi)
        vals = [jnp.where(i == bi, -jnp.inf, v) for v, i in zip(vals, ids)]
    return out_v, out_i


def _stack(reps, start):
    row = lax.broadcasted_iota(jnp.int32, reps[0].shape, 0)
    out = reps[start + SUBLANES - 1]
    for r in range(SUBLANES - 2, -1, -1):
        out = jnp.where(row == r, reps[start + r], out)
    return out


def _select_experts(sa, sb):
    t = sa.shape[1]
    row = lax.broadcasted_iota(jnp.int32, (SUBLANES, t), 0)
    groups = lambda s: [s[g * SUBLANES:(g + 1) * SUBLANES, :] for g in range(N_KEYS // SUBLANES)]
    key_ids = [row + g * SUBLANES for g in range(N_KEYS // SUBLANES)]
    va, ia = _top16_groups(groups(sa), key_ids)
    vb, ib = _top16_groups(groups(sb), key_ids)
    vb_lo, vb_hi, va_hi = _stack(vb, 0), _stack(vb, SUBLANES), _stack(va, SUBLANES)
    cand = [va[0] + vb_lo, va[0] + vb_hi]
    cand_ids = [row, row + SUBLANES]
    for i in range(1, SUBLANES):
        cand.append(jnp.where(row < CAND_J_COUNT[i], va[i] + vb_lo, -jnp.inf))
        cand_ids.append(row + i * PEER_TOPK)
    cand.append(va_hi + vb[0])
    cand_ids.append((row + SUBLANES) * PEER_TOPK)
    top_s, pos = _top16_groups(cand, cand_ids)
    idx_halves, e_halves = [], []
    for h in range(2):
        p = _stack(pos, h * SUBLANES)
        pa, pb = p >> 4, p & (PEER_TOPK - 1)
        ea, eb = jnp.zeros_like(p), jnp.zeros_like(p)
        for i in range(PEER_TOPK):
            ea = jnp.where(pa == i, ia[i], ea)
            eb = jnp.where(pb == i, ib[i], eb)
        idx_halves.append((ea * N_KEYS + eb) * TABLE_ROWS_PER_EXPERT)
        e_halves.append(jnp.exp(_stack(top_s, h * SUBLANES) - top_s[0]))
    den = e_halves[0] + e_halves[1]
    for sh in (4, 2, 1):
        den = den + pltpu.roll(den, sh, axis=0)
    return jnp.concatenate(idx_halves, axis=0), jnp.concatenate([e / den for e in e_halves], axis=0)


def _route_kernel(x2_ref, gffn_ref, wq_ref, ka_ref, kb_ref, xn_ref, idx_ref, gate_ref):
    x = x2_ref[...]
    ms = jnp.mean(x * x, axis=-1, keepdims=True)
    xn = x * lax.rsqrt(ms + RMS_EPS) * gffn_ref[...]
    xn_ref[...] = xn
    qh = _dot(xn.astype(jnp.bfloat16), wq_ref[...])
    half = D_KEY // 2
    for lt in range(x.shape[0] // LANES):
        ts = slice(lt * LANES, (lt + 1) * LANES)
        idx_rows, gate_rows = [], []
        for h in range(PEER_HEADS):
            qa = qh[ts, h * D_KEY:h * D_KEY + half].astype(jnp.bfloat16)
            qb = qh[ts, h * D_KEY + half:(h + 1) * D_KEY].astype(jnp.bfloat16)
            idx_h, gate_h = _select_experts(_dot_nt(ka_ref[...], qa), _dot_nt(kb_ref[...], qb))
            idx_rows.append(idx_h)
            gate_rows.append(gate_h)
        idx_ref[ts, :] = jnp.concatenate(idx_rows, axis=0).T
        gate_ref[ts, :] = jnp.concatenate(gate_rows, axis=0).T


def _peer_route(x2, gffn, wq, ka, kb):
    n = x2.shape[0]
    tm = TM_ROUTE
    tok = lambda w: pl.BlockSpec((tm, w), lambda i: (i, 0))
    full = lambda shape: pl.BlockSpec(shape, lambda i: (0,) * len(shape))
    return pl.pallas_call(
        _route_kernel,
        grid=(n // tm,),
        in_specs=[tok(D_MODEL), full((1, D_MODEL)), full((D_MODEL, PEER_HEADS * D_KEY)),
                  full((N_KEYS, D_KEY // 2)), full((N_KEYS, D_KEY // 2))],
        out_specs=[tok(D_MODEL), tok(NJ), tok(NJ)],
        out_shape=[jax.ShapeDtypeStruct((n, D_MODEL), jnp.float32),
                   jax.ShapeDtypeStruct((n, NJ), jnp.int32),
                   jax.ShapeDtypeStruct((n, NJ), jnp.float32)],
        compiler_params=pltpu.CompilerParams(dimension_semantics=("arbitrary",), vmem_limit_bytes=VMEM_LIMIT),
        name="peer_route",
    )(x2, gffn, wq, ka, kb)


def _pack_table(tab):
    bits = lax.bitcast_convert_type(tab.astype(jnp.bfloat16), jnp.uint16).astype(jnp.uint32)
    half = D_MODEL // 2
    word = bits[:, :half] | (bits[:, half:] << 16)
    return word.reshape(tab.shape[0] * TABLE_ROWS_PER_EXPERT, LANES)


def _unpack(w):
    lo = lax.bitcast_convert_type(w << 16, jnp.float32)
    hi = lax.bitcast_convert_type(w & jnp.uint32(0xFFFF0000), jnp.float32)
    return lo, hi


def _gather_rows(idx_ref, base, tab_ref, w_ref):
    for g in range(NJ // SUBLANES):
        window = idx_ref.at[pl.ds(base + g * SUBLANES, SUBLANES)]
        for r in range(SUBLANES):
            off = pl.multiple_of(window[r], TABLE_ROWS_PER_EXPERT)
            w_ref[pl.ds(g * 32 + r, TABLE_ROWS_PER_EXPERT, stride=SUBLANES), :] = (
                tab_ref[pl.ds(off, TABLE_ROWS_PER_EXPERT), :])


def _gather_rows_dense(idx_ref, base, tab_ref, w_ref):
    for g in range(NJ // SUBLANES):
        window = idx_ref.at[pl.ds(base + g * SUBLANES, SUBLANES)]
        for r in range(SUBLANES):
            off = pl.multiple_of(window[r], TABLE_ROWS_PER_EXPERT)
            w_ref[pl.ds((g * SUBLANES + r) * TABLE_ROWS_PER_EXPERT, TABLE_ROWS_PER_EXPERT), :] = (
                tab_ref[pl.ds(off, TABLE_ROWS_PER_EXPERT), :])


def _peer_up_kernel(idx_ref, gate_ref, xn_ref, tab_ref, coef_ref, w_ref, pa_ref, pb_ref, act_ref, *, tb):
    def partial_dots(t, p_ref):
        _gather_rows(idx_ref, t * NJ, tab_ref, w_ref)
        xt = xn_ref[t]
        xb = [jnp.broadcast_to(xt[c:c + 1, :], (SUBLANES, LANES)) for c in range(SUBLANES)]
        for g in range(NJ // SUBLANES):
            acc = None
            for s in range(4):
                lo, hi = _unpack(w_ref[pl.ds(g * 32 + s * SUBLANES, SUBLANES), :])
                term = lo * xb[s] + hi * xb[4 + s]
                acc = term if acc is None else acc + term
            p_ref[pl.ds(g * SUBLANES, SUBLANES), :] = acc

    def lane_sums(p_ref, t):
        act_ref[pl.ds(t, 1), :] = jnp.sum(p_ref[...].T, axis=0, keepdims=True)

    pb_ref[...] = jnp.zeros_like(pb_ref)

    def body(i, carry):
        t0 = 2 * i
        lane_sums(pb_ref, jnp.maximum(t0 - 1, 0))
        partial_dots(t0, pa_ref)
        lane_sums(pa_ref, t0)
        partial_dots(t0 + 1, pb_ref)
        return carry

    lax.fori_loop(0, tb // 2, body, 0)
    lane_sums(pb_ref, tb - 1)
    coef_ref[...] = gate_ref[...] * _gelu(act_ref[...])


def _peer_down_kernel(idx_ref, coef_ref, x2_ref, tab_ref, o_ref, w_ref, ca_ref, cb_ref, *, tb):
    def spread(t, c_ref):
        c_ref[...] = jnp.broadcast_to(coef_ref[pl.ds(t, 1), :], (NJ, LANES)).T

    def weighted_sum(t, c_ref):
        _gather_rows_dense(idx_ref, t * NJ, tab_ref, w_ref)
        acc_lo, acc_hi = [None] * 4, [None] * 4
        for g in range(NJ // SUBLANES):
            cg = c_ref[pl.ds(g * SUBLANES, SUBLANES), :]
            for s in range(4):
                lo, hi = _unpack(w_ref[pl.ds(g * 32 + s, SUBLANES, stride=TABLE_ROWS_PER_EXPERT), :])
                acc_lo[s] = lo * cg if acc_lo[s] is None else acc_lo[s] + lo * cg
                acc_hi[s] = hi * cg if acc_hi[s] is None else acc_hi[s] + hi * cg
        rows = [jnp.sum(a, axis=0, keepdims=True) for a in acc_lo + acc_hi]
        o_ref[t] = x2_ref[t] + jnp.concatenate(rows, axis=0)

    spread(0, ca_ref)

    def body(i, carry):
        t0 = 2 * i
        spread(t0 + 1, cb_ref)
        weighted_sum(t0, ca_ref)
        spread(jnp.minimum(t0 + 2, tb - 1), ca_ref)
        weighted_sum(t0 + 1, cb_ref)
        return carry

    lax.fori_loop(0, tb // 2, body, 0)


def _peer_specs(n, table_rows, tb):
    idx_spec = pl.BlockSpec((tb * NJ,), lambda i: (i,), memory_space=pltpu.SMEM)
    row_spec = pl.BlockSpec((tb, NJ), lambda i: (i, 0))
    tok_spec = pl.BlockSpec((tb, SUBLANES, LANES), lambda i: (i, 0, 0))
    tab_spec = pl.BlockSpec((table_rows, LANES), lambda i: (0, 0), pipeline_mode=pl.Buffered(1))
    return idx_spec, row_spec, tok_spec, tab_spec


_W_SCRATCH = pltpu.VMEM((NJ * TABLE_ROWS_PER_EXPERT, LANES), jnp.uint32)
_SQUARE_SCRATCH = pltpu.VMEM((NJ, LANES), jnp.float32)


def _peer_up(idx, gate, xn, tab):
    n, tb = xn.shape[0], TB_PEER
    idx_spec, row_spec, tok_spec, tab_spec = _peer_specs(n, tab.shape[0], tb)
    return pl.pallas_call(
        functools.partial(_peer_up_kernel, tb=tb),
        grid=(n // tb,),
        in_specs=[idx_spec, row_spec, tok_spec, tab_spec],
        out_specs=row_spec,
        out_shape=jax.ShapeDtypeStruct((n, NJ), jnp.float32),
        scratch_shapes=[_W_SCRATCH, _SQUARE_SCRATCH, _SQUARE_SCRATCH, pltpu.VMEM((tb, NJ), jnp.float32)],
        compiler_params=pltpu.CompilerParams(dimension_semantics=("arbitrary",), vmem_limit_bytes=VMEM_LIMIT),
        name="peer_up",
    )(idx.reshape(-1), gate, xn.reshape(n, SUBLANES, LANES), tab)


def _peer_down(idx, coef, x2, tab):
    n, tb = x2.shape[0], TB_PEER
    idx_spec, row_spec, tok_spec, tab_spec = _peer_specs(n, tab.shape[0], tb)
    return pl.pallas_call(
        functools.partial(_peer_down_kernel, tb=tb),
        grid=(n // tb,),
        in_specs=[idx_spec, row_spec, tok_spec, tab_spec],
        out_specs=tok_spec,
        out_shape=jax.ShapeDtypeStruct((n, SUBLANES, LANES), jnp.float32),
        scratch_shapes=[_W_SCRATCH, _SQUARE_SCRATCH, _SQUARE_SCRATCH],
        compiler_params=pltpu.CompilerParams(dimension_semantics=("arbitrary",), vmem_limit_bytes=VMEM_LIMIT),
        name="peer_down",
    )(idx.reshape(-1), coef, x2.reshape(n, SUBLANES, LANES), tab).reshape(n, D_MODEL)


def _layer(x2d, batch, seq, mix_norm_g, w_in, q_norm_g, k_norm_g, v_gate_norm_g, w_spatial, b_spatial, attn_out_g,
           gate_out_g, w_out, ffn_norm_g, w_query, sub_keys_a, sub_keys_b, expert_u, expert_v):
    bf16 = jnp.bfloat16
    heads = ATTN_WIDTH // HEAD_DIM
    lane = jnp.arange(ATTN_WIDTH)
    g64 = (lane[:, None] // HEAD_DIM == lane[None, :] // HEAD_DIM).astype(bf16)
    gq = (jnp.tile(q_norm_g, heads) * HEAD_DIM ** -0.5)[None, :]
    gk = jnp.tile(k_norm_g, heads)[None, :]
    bsp = jnp.repeat(b_spatial.T, GMLP_WIDTH // GMLP_GROUPS, axis=1)
    *qkv, gated = _in_proj(x2d, batch, seq, mix_norm_g[None, :], w_in.astype(bf16), gq, gk, g64,
                           v_gate_norm_g.reshape(1, GMLP_WIDTH), w_spatial, bsp, gate_out_g[None, :])
    n_pat = len(PATTERNS)
    outs, lses = [], []
    for p, (window, dilation) in enumerate(PATTERNS):
        assert window // dilation == N_BACK
        o, l = _attention(qkv[p], qkv[n_pat + p], qkv[2 * n_pat + p], dilation)
        outs.append(o)
        lses.append(l)
    x2 = _out_proj(outs, lses, gated, x2d, seq, w_out.astype(bf16), attn_out_g[None, :])
    xn, idx, gate = _peer_route(x2, ffn_norm_g[None, :], w_query.astype(bf16), sub_keys_a.astype(bf16),
                                sub_keys_b.astype(bf16))
    coef = _peer_up(idx, gate, xn, _pack_table(expert_u))
    return _peer_down(idx, coef, x2, _pack_table(expert_v))


def kernel(x, mix_norm_g, w_in, q_norm_g, k_norm_g, v_gate_norm_g, w_spatial, b_spatial, attn_out_g, gate_out_g,
           w_out, ffn_norm_g, w_query, sub_keys_a, sub_keys_b, expert_u, expert_v):
    batch, seq, d = x.shape
    assert d == D_MODEL and seq % TM_PROJ == 0 and (batch * seq) % TB_PEER == 0 and TB_PEER % 2 == 0
    for _, dilation in PATTERNS:
        sub_len = seq // dilation
        assert seq % dilation == 0 and sub_len % ATTN_BLOCK == 0 and sub_len % min(TQ_ATTN, sub_len) == 0
    x2d = x.reshape(batch * seq, d)
    for l in range(mix_norm_g.shape[0]):
        x2d = _layer(x2d, batch, seq, mix_norm_g[l], w_in[l], q_norm_g[l], k_norm_g[l], v_gate_norm_g[l],
                     w_spatial[l], b_spatial[l], attn_out_g[l], gate_out_g[l], w_out[l], ffn_norm_g[l],
                     w_query[l], sub_keys_a[l], sub_keys_b[l], expert_u[l], expert_v[l])
    return x2d.reshape(batch, seq, d)
```

```python
import functools

import jax
import jax.numpy as jnp
from jax import lax
from jax.experimental import pallas as pl
from jax.experimental.pallas import tpu as pltpu

D_MODEL = 1024
ATTN_WIDTH = 512
HEAD_DIM = 64
HEAD_PAIRS = ATTN_WIDTH // 128
PATTERNS = ((128, 1), (512, 4), (2048, 16))
N_BACK = 128
ATTN_BLOCK = 128
GMLP_WIDTH = 512
GMLP_CHUNK = 128
GMLP_GROUPS = 4
IN_WIDTH = 3 * ATTN_WIDTH + 2 * GMLP_WIDTH
PEER_HEADS = 8
PEER_TOPK = 16
N_KEYS = 128
D_KEY = 256
NJ = PEER_HEADS * PEER_TOPK
TABLE_ROWS_PER_EXPERT = 4
RMS_EPS = 1e-6
NEG = -1e30

LANES = 128
SUBLANES = 8
VMEM_LIMIT = 56 * 2 ** 20

TM_PROJ = 512
TQ_ATTN = 512
TM_ROUTE = 256
TB_PEER = 256


def _gelu(x):
    return 0.5 * x * (1.0 + lax.erf(x * (2.0 ** -0.5)))


def _split_bf16(x):
    hi = x.astype(jnp.bfloat16)
    lo = (x - hi.astype(jnp.float32)).astype(jnp.bfloat16)
    return hi, lo


def _dot(a, b):
    return jnp.dot(a, b, preferred_element_type=jnp.float32)


def _dot_nt(a, b):
    return lax.dot_general(a, b, (((1,), (1,)), ((), ())), preferred_element_type=jnp.float32)


def _store_views(val, stage_ref, out_refs):
    chunks = ATTN_WIDTH // LANES
    for c in range(chunks):
        stage_ref[c] = val[:, c * LANES:(c + 1) * LANES]
    for (_, dilation), o_ref in zip(PATTERNS, out_refs):
        rows = val.shape[0] // dilation
        for r in range(dilation):
            for c in range(chunks):
                lanes = pl.ds(r * ATTN_WIDTH + c * LANES, LANES)
                o_ref[0, :, lanes] = stage_ref[c, pl.ds(r, rows, stride=dilation), :].astype(jnp.bfloat16)


def _in_proj_kernel(x_ref, gmix_ref, win_ref, gq_ref, gk_ref, g64_ref, gvg_ref, wsp_ref, bsp_ref, gout_ref,
                    q1_ref, q2_ref, q3_ref, k1_ref, k2_ref, k3_ref, v1_ref, v2_ref, v3_ref, gated_ref,
                    gs_ref, stage_ref):
    x = x_ref[...]
    ms = jnp.mean(x * x, axis=-1, keepdims=True)
    hn = (x * lax.rsqrt(ms + RMS_EPS) * gmix_ref[...]).astype(jnp.bfloat16)

    def head_norm(t, g):
        hi, lo = _split_bf16(t * t)
        msq = (_dot(hi, g64_ref[...]) + _dot(lo, g64_ref[...])) * (1.0 / HEAD_DIM)
        return t * lax.rsqrt(msq + RMS_EPS) * g

    q = _dot(hn, win_ref[:, 0:ATTN_WIDTH])
    _store_views(head_norm(q, gq_ref[...]), stage_ref, (q1_ref, q2_ref, q3_ref))
    k = _dot(hn, win_ref[:, ATTN_WIDTH:2 * ATTN_WIDTH])
    _store_views(head_norm(k, gk_ref[...]), stage_ref, (k1_ref, k2_ref, k3_ref))
    _store_views(_dot(hn, win_ref[:, 2 * ATTN_WIDTH:3 * ATTN_WIDTH]), stage_ref, (v1_ref, v2_ref, v3_ref))

    u = _gelu(_dot(hn, win_ref[:, 3 * ATTN_WIDTH:3 * ATTN_WIDTH + GMLP_WIDTH]))
    gv = _gelu(_dot(hn, win_ref[:, 3 * ATTN_WIDTH + GMLP_WIDTH:IN_WIDTH]))
    row = lax.broadcasted_iota(jnp.int32, (GMLP_CHUNK, GMLP_CHUNK), 0)
    col = lax.broadcasted_iota(jnp.int32, (GMLP_CHUNK, GMLP_CHUNK), 1)
    causal = col <= row
    n_chunks = x.shape[0] // GMLP_CHUNK
    for g in range(GMLP_GROUPS):
        cs = slice(g * LANES, (g + 1) * LANES)
        vg = gv[:, cs]
        msg = jnp.mean(vg * vg, axis=-1, keepdims=True)
        vn = (vg * lax.rsqrt(msg + RMS_EPS) * gvg_ref[:, cs]).astype(jnp.bfloat16)
        w = jnp.where(causal, wsp_ref[g], 0.0).astype(jnp.bfloat16)
        for c in range(n_chunks):
            rs = slice(c * GMLP_CHUNK, (c + 1) * GMLP_CHUNK)
            z = _dot(w, vn[rs, :]) + bsp_ref[:, cs]
            gs_ref[rs, cs] = u[rs, cs] * z
    gated = gs_ref[...]
    msg = jnp.mean(gated * gated, axis=-1, keepdims=True)
    gated_ref[...] = (gated * lax.rsqrt(msg + RMS_EPS) * gout_ref[...]).astype(jnp.bfloat16)


def _view_spec(seq, tm, dilation):
    blocks_per_row = seq // tm
    return pl.BlockSpec((1, tm // dilation, dilation * ATTN_WIDTH),
                        lambda i: (i // blocks_per_row, i % blocks_per_row, 0))


def _view_shape(batch, seq, dilation, dtype):
    return jax.ShapeDtypeStruct((batch, seq // dilation, dilation * ATTN_WIDTH), dtype)


def _in_proj(x2d, batch, seq, gmix, win, gq, gk, g64, gvg, wsp, bsp, gout):
    n = x2d.shape[0]
    tm = TM_PROJ
    full = lambda shape: pl.BlockSpec(shape, lambda i: (0,) * len(shape))
    tok = lambda w: pl.BlockSpec((tm, w), lambda i: (i, 0))
    view_specs = [_view_spec(seq, tm, d) for _, d in PATTERNS] * 3
    view_shapes = [_view_shape(batch, seq, d, jnp.bfloat16) for _, d in PATTERNS] * 3
    return pl.pallas_call(
        _in_proj_kernel,
        grid=(n // tm,),
        in_specs=[tok(D_MODEL), full((1, D_MODEL)), full((D_MODEL, IN_WIDTH)), full((1, ATTN_WIDTH)),
                  full((1, ATTN_WIDTH)), full((ATTN_WIDTH, ATTN_WIDTH)), full((1, GMLP_WIDTH)),
                  full((GMLP_GROUPS, GMLP_CHUNK, GMLP_CHUNK)), full((GMLP_CHUNK, GMLP_WIDTH)), full((1, GMLP_WIDTH))],
        out_specs=view_specs + [tok(GMLP_WIDTH)],
        out_shape=view_shapes + [jax.ShapeDtypeStruct((n, GMLP_WIDTH), jnp.bfloat16)],
        scratch_shapes=[pltpu.VMEM((tm, GMLP_WIDTH), jnp.float32),
                        pltpu.VMEM((ATTN_WIDTH // LANES, tm, LANES), jnp.float32)],
        compiler_params=pltpu.CompilerParams(dimension_semantics=("arbitrary",), vmem_limit_bytes=VMEM_LIMIT),
        name="in_proj",
    )(x2d, gmix, win, gq, gk, g64, gvg, wsp, bsp, gout)


def _attn_kernel(q_ref, kp_ref, kc_ref, vp_ref, vc_ref, o_ref, l_ref):
    tq = q_ref.shape[1]
    have_prev = pl.program_id(2) > 0
    qi = lax.broadcasted_iota(jnp.int32, (ATTN_BLOCK, 2 * ATTN_BLOCK), 0)
    kj = lax.broadcasted_iota(jnp.int32, (ATTN_BLOCK, 2 * ATTN_BLOCK), 1)
    rel = qi + ATTN_BLOCK - kj
    band = (rel >= 0) & (rel <= N_BACK)
    lane = lax.broadcasted_iota(jnp.int32, (ATTN_BLOCK, LANES), 1)
    for qb in range(tq // ATTN_BLOCK):
        rs = slice(qb * ATTN_BLOCK, (qb + 1) * ATTN_BLOCK)
        if qb == 0:
            mask = band & ((kj >= ATTN_BLOCK) | have_prev)
        else:
            mask = band
        for p in range(HEAD_PAIRS):
            cs = slice(p * LANES, (p + 1) * LANES)
            qp = q_ref[0, rs, cs]
            if qb == 0:
                kprev, vprev = kp_ref[0, :, cs], vp_ref[0, :, cs]
            else:
                ps = slice((qb - 1) * ATTN_BLOCK, qb * ATTN_BLOCK)
                kprev, vprev = kc_ref[0, ps, cs], vc_ref[0, ps, cs]
            keys = jnp.concatenate([kprev, kc_ref[0, rs, cs]], axis=0)
            vals = jnp.concatenate([vprev, vc_ref[0, rs, cs]], axis=0)
            out_pair = jnp.zeros((ATTN_BLOCK, LANES), jnp.float32)
            lse_pair = jnp.zeros((ATTN_BLOCK, LANES), jnp.float32)
            for hh in range(2):
                in_head = (lane >= hh * HEAD_DIM) & (lane < (hh + 1) * HEAD_DIM)
                s = _dot_nt(jnp.where(in_head, qp, jnp.zeros_like(qp)), keys)
                s = jnp.where(mask, s, NEG)
                m = jnp.max(s, axis=-1, keepdims=True)
                e = jnp.exp(s - m)
                den = jnp.sum(e, axis=-1, keepdims=True)
                o = _dot(e.astype(jnp.bfloat16), vals) / den
                out_pair = jnp.where(in_head, o, out_pair)
                lse_pair = jnp.where(in_head, m + jnp.log(den), lse_pair)
            o_ref[0, rs, cs] = out_pair
            l_ref[0, rs, cs] = lse_pair


def _attention(q, k, v, dilation):
    batch, L, _ = q.shape
    tq = min(TQ_ATTN, L)
    cur = pl.BlockSpec((1, tq, ATTN_WIDTH), lambda b, r, i: (b, i, r))
    prev = pl.BlockSpec((1, ATTN_BLOCK, ATTN_WIDTH),
                        lambda b, r, i: (b, jnp.maximum(i * (tq // ATTN_BLOCK) - 1, 0), r))
    out = jax.ShapeDtypeStruct((batch, L, dilation * ATTN_WIDTH), jnp.float32)
    return pl.pallas_call(
        _attn_kernel,
        grid=(batch, dilation, L // tq),
        in_specs=[cur, prev, cur, prev, cur],
        out_specs=[cur, cur],
        out_shape=[out, out],
        compiler_params=pltpu.CompilerParams(dimension_semantics=("arbitrary",) * 3, vmem_limit_bytes=VMEM_LIMIT),
        name=f"attention_d{dilation}",
    )(q, k, k, v, v)


def _load_view(v_ref, stage_ref, dilation):
    if dilation == 1:
        return v_ref[0]
    rows = v_ref.shape[1]
    chunks = ATTN_WIDTH // LANES
    for r in range(dilation):
        for c in range(chunks):
            stage_ref[c, pl.ds(r, rows, stride=dilation), :] = v_ref[0, :, pl.ds(r * ATTN_WIDTH + c * LANES, LANES)]
    return jnp.concatenate([stage_ref[c] for c in range(chunks)], axis=1)


def _out_proj_kernel(o1_ref, o2_ref, o3_ref, l1_ref, l2_ref, l3_ref, gated_ref, x_ref, wout_ref, gattn_ref, x2_ref,
                     so2_ref, so3_ref, sl2_ref, sl3_ref):
    dil = [d for _, d in PATTERNS]
    l1 = _load_view(l1_ref, None, dil[0])
    l2 = _load_view(l2_ref, sl2_ref, dil[1])
    l3 = _load_view(l3_ref, sl3_ref, dil[2])
    o1 = _load_view(o1_ref, None, dil[0])
    o2 = _load_view(o2_ref, so2_ref, dil[1])
    o3 = _load_view(o3_ref, so3_ref, dil[2])
    m = jnp.maximum(jnp.maximum(l1, l2), l3)
    e1, e2, e3 = jnp.exp(l1 - m), jnp.exp(l2 - m), jnp.exp(l3 - m)
    attn = (e1 * o1 + e2 * o2 + e3 * o3) / (e1 + e2 + e3)
    ms = jnp.mean(attn * attn, axis=-1, keepdims=True)
    attn_n = (attn * lax.rsqrt(ms + RMS_EPS) * gattn_ref[...]).astype(jnp.bfloat16)
    y = _dot(attn_n, wout_ref[0:ATTN_WIDTH, :]) + _dot(gated_ref[...], wout_ref[ATTN_WIDTH:, :])
    x2_ref[...] = x_ref[...] + y


def _out_proj(outs, lses, gated, x2d, seq, wout, gattn):
    n = x2d.shape[0]
    tm = TM_PROJ
    tok = lambda w: pl.BlockSpec((tm, w), lambda i: (i, 0))
    full = lambda shape: pl.BlockSpec(shape, lambda i: (0,) * len(shape))
    view_specs = [_view_spec(seq, tm, d) for _, d in PATTERNS] * 2
    stage = pltpu.VMEM((ATTN_WIDTH // LANES, tm, LANES), jnp.float32)
    return pl.pallas_call(
        _out_proj_kernel,
        grid=(n // tm,),
        in_specs=view_specs + [tok(GMLP_WIDTH), tok(D_MODEL), full((D_MODEL, D_MODEL)), full((1, ATTN_WIDTH))],
        out_specs=tok(D_MODEL),
        out_shape=jax.ShapeDtypeStruct((n, D_MODEL), jnp.float32),
        scratch_shapes=[stage] * 4,
        compiler_params=pltpu.CompilerParams(dimension_semantics=("arbitrary",), vmem_limit_bytes=VMEM_LIMIT),
        name="out_proj",
    )(*outs, *lses, gated, x2d, wout, gattn)


CAND_J_COUNT = (16, 8, 5, 4, 3, 2, 2, 2)


def _top16_groups(vals, ids):
    vals = list(vals)
    out_v, out_i = [], []
    for _ in range(PEER_TOPK):
        bv, bi = vals[0], ids[0]
        for v, i in zip(vals[1:], ids[1:]):
            gt = v > bv
            bv = jnp.where(gt, v, bv)
            bi = jnp.where(gt, i, bi)
        for sh in (4, 2, 1):
            rv, ri = pltpu.roll(bv, sh, axis=0), pltpu.roll(bi, sh, axis=0)
            better = (rv > bv) | ((rv == bv) & (ri < bi))
            bv = jnp.where(better, rv, bv)
            bi = jnp.where(better, ri, bi)
        out_v.append(bv)
        out_i.append(bi)
        vals = [jnp.where(i == bi, -jnp.inf, v) for v, i in zip(vals, ids)]
    return out_v, out_i


def _stack(reps, start):
    row = lax.broadcasted_iota(jnp.int32, reps[0].shape, 0)
    out = reps[start + SUBLANES - 1]
    for r in range(SUBLANES - 2, -1, -1):
        out = jnp.where(row == r, reps[start + r], out)
    return out


def _select_experts(sa, sb):
    t = sa.shape[1]
    row = lax.broadcasted_iota(jnp.int32, (SUBLANES, t), 0)
    groups = lambda s: [s[g * SUBLANES:(g + 1) * SUBLANES, :] for g in range(N_KEYS // SUBLANES)]
    key_ids = [row + g * SUBLANES for g in range(N_KEYS // SUBLANES)]
    va, ia = _top16_groups(groups(sa), key_ids)
    vb, ib = _top16_groups(groups(sb), key_ids)
    vb_lo, vb_hi, va_hi = _stack(vb, 0), _stack(vb, SUBLANES), _stack(va, SUBLANES)
    cand = [va[0] + vb_lo, va[0] + vb_hi]
    cand_ids = [row, row + SUBLANES]
    for i in range(1, SUBLANES):
        cand.append(jnp.where(row < CAND_J_COUNT[i], va[i] + vb_lo, -jnp.inf))
        cand_ids.append(row + i * PEER_TOPK)
    cand.append(va_hi + vb[0])
    cand_ids.append((row + SUBLANES) * PEER_TOPK)
    top_s, pos = _top16_groups(cand, cand_ids)
    idx_halves, e_halves = [], []
    for h in range(2):
        p = _stack(pos, h * SUBLANES)
        pa, pb = p >> 4, p & (PEER_TOPK - 1)
        ea, eb = jnp.zeros_like(p), jnp.zeros_like(p)
        for i in range(PEER_TOPK):
            ea = jnp.where(pa == i, ia[i], ea)
            eb = jnp.where(pb == i, ib[i], eb)
        idx_halves.append((ea * N_KEYS + eb) * TABLE_ROWS_PER_EXPERT)
        e_halves.append(jnp.exp(_stack(top_s, h * SUBLANES) - top_s[0]))
    den = e_halves[0] + e_halves[1]
    for sh in (4, 2, 1):
        den = den + pltpu.roll(den, sh, axis=0)
    return jnp.concatenate(idx_halves, axis=0), jnp.concatenate([e / den for e in e_halves], axis=0)


def _route_kernel(x2_ref, gffn_ref, wq_ref, ka_ref, kb_ref, xn_ref, idx_ref, gate_ref):
    x = x2_ref[...]
    ms = jnp.mean(x * x, axis=-1, keepdims=True)
    xn = x * lax.rsqrt(ms + RMS_EPS) * gffn_ref[...]
    for c in range(D_MODEL // LANES):
        xn_ref[pl.ds(c, x.shape[0], stride=D_MODEL // LANES), :] = xn[:, c * LANES:(c + 1) * LANES]
    qh = _dot(xn.astype(jnp.bfloat16), wq_ref[...])
    half = D_KEY // 2
    for lt in range(x.shape[0] // LANES):
        ts = slice(lt * LANES, (lt + 1) * LANES)
        idx_rows, gate_rows = [], []
        for h in range(PEER_HEADS):
            qa = qh[ts, h * D_KEY:h * D_KEY + half].astype(jnp.bfloat16)
            qb = qh[ts, h * D_KEY + half:(h + 1) * D_KEY].astype(jnp.bfloat16)
            idx_h, gate_h = _select_experts(_dot_nt(ka_ref[...], qa), _dot_nt(kb_ref[...], qb))
            idx_rows.append(idx_h)
            gate_rows.append(gate_h)
        idx_ref[ts, :] = jnp.concatenate(idx_rows, axis=0).T
        gate_ref[ts, :] = jnp.concatenate(gate_rows, axis=0).T


def _peer_route(x2, gffn, wq, ka, kb):
    n = x2.shape[0]
    tm = TM_ROUTE
    tok = lambda w: pl.BlockSpec((tm, w), lambda i: (i, 0))
    full = lambda shape: pl.BlockSpec(shape, lambda i: (0,) * len(shape))
    return pl.pallas_call(
        _route_kernel,
        grid=(n // tm,),
        in_specs=[tok(D_MODEL), full((1, D_MODEL)), full((D_MODEL, PEER_HEADS * D_KEY)),
                  full((N_KEYS, D_KEY // 2)), full((N_KEYS, D_KEY // 2))],
        out_specs=[pl.BlockSpec((tm * SUBLANES, LANES), lambda i: (i, 0)), tok(NJ), tok(NJ)],
        out_shape=[jax.ShapeDtypeStruct((n * SUBLANES, LANES), jnp.float32),
                   jax.ShapeDtypeStruct((n, NJ), jnp.int32),
                   jax.ShapeDtypeStruct((n, NJ), jnp.float32)],
        compiler_params=pltpu.CompilerParams(dimension_semantics=("arbitrary",), vmem_limit_bytes=VMEM_LIMIT),
        name="peer_route",
    )(x2, gffn, wq, ka, kb)


def _pack_table(tab):
    bits = lax.bitcast_convert_type(tab.astype(jnp.bfloat16), jnp.uint16).astype(jnp.uint32)
    half = D_MODEL // 2
    word = bits[:, :half] | (bits[:, half:] << 16)
    return word.reshape(tab.shape[0] * TABLE_ROWS_PER_EXPERT, LANES)


def _unpack(w):
    lo = lax.bitcast_convert_type(w << 16, jnp.float32)
    hi = lax.bitcast_convert_type(w & jnp.uint32(0xFFFF0000), jnp.float32)
    return lo, hi


def _gather_rows(idx_ref, base, tab_ref, w_ref):
    for g in range(NJ // SUBLANES):
        window = idx_ref.at[pl.ds(base + g * SUBLANES, SUBLANES)]
        for r in range(SUBLANES):
            off = pl.multiple_of(window[r], TABLE_ROWS_PER_EXPERT)
            w_ref[pl.ds((g * SUBLANES + r) * TABLE_ROWS_PER_EXPERT, TABLE_ROWS_PER_EXPERT), :] = (
                tab_ref[pl.ds(off, TABLE_ROWS_PER_EXPERT), :])


def _group_chunk(w_ref, g, s):
    return _unpack(w_ref[pl.ds(g * SUBLANES * TABLE_ROWS_PER_EXPERT + s, SUBLANES, stride=TABLE_ROWS_PER_EXPERT), :])


def _peer_up_kernel(idx_ref, gate_ref, xn_ref, tab_ref, coef_ref, w_ref, pa_ref, pb_ref, act_ref, *, tb):
    def partial_dots(t, p_ref):
        _gather_rows(idx_ref, t * NJ, tab_ref, w_ref)
        xt = xn_ref[t]
        xb = [jnp.broadcast_to(xt[c:c + 1, :], (SUBLANES, LANES)) for c in range(SUBLANES)]
        for g in range(NJ // SUBLANES):
            acc = None
            for s in range(4):
                lo, hi = _group_chunk(w_ref, g, s)
                term = lo * xb[s] + hi * xb[4 + s]
                acc = term if acc is None else acc + term
            p_ref[pl.ds(g * SUBLANES, SUBLANES), :] = acc

    def lane_sums(p_ref, t):
        act_ref[pl.ds(t, 1), :] = jnp.sum(p_ref[...].T, axis=0, keepdims=True)

    pb_ref[...] = jnp.zeros_like(pb_ref)

    def body(i, carry):
        t0 = 2 * i
        lane_sums(pb_ref, jnp.maximum(t0 - 1, 0))
        partial_dots(t0, pa_ref)
        lane_sums(pa_ref, t0)
        partial_dots(t0 + 1, pb_ref)
        return carry

    lax.fori_loop(0, tb // 2, body, 0)
    lane_sums(pb_ref, tb - 1)
    coef_ref[...] = gate_ref[...] * _gelu(act_ref[...])


def _peer_down_kernel(idx_ref, coef_ref, x2_ref, tab_ref, o_ref, w_ref, ca_ref, cb_ref, *, tb):
    def spread(t, c_ref):
        c_ref[...] = jnp.broadcast_to(coef_ref[pl.ds(t, 1), :], (NJ, LANES)).T

    def weighted_sum(t, c_ref):
        _gather_rows(idx_ref, t * NJ, tab_ref, w_ref)
        acc_lo, acc_hi = [None] * 4, [None] * 4
        for g in range(NJ // SUBLANES):
            cg = c_ref[pl.ds(g * SUBLANES, SUBLANES), :]
            for s in range(4):
                lo, hi = _group_chunk(w_ref, g, s)
                acc_lo[s] = lo * cg if acc_lo[s] is None else acc_lo[s] + lo * cg
                acc_hi[s] = hi * cg if acc_hi[s] is None else acc_hi[s] + hi * cg
        rows = [jnp.sum(a, axis=0, keepdims=True) for a in acc_lo + acc_hi]
        o_ref[t] = x2_ref[t] + jnp.concatenate(rows, axis=0)

    spread(0, ca_ref)

    def body(i, carry):
        t0 = 2 * i
        spread(t0 + 1, cb_ref)
        weighted_sum(t0, ca_ref)
        spread(jnp.minimum(t0 + 2, tb - 1), ca_ref)
        weighted_sum(t0 + 1, cb_ref)
        return carry

    lax.fori_loop(0, tb // 2, body, 0)


def _peer_specs(n, table_rows, tb):
    idx_spec = pl.BlockSpec((tb * NJ,), lambda i: (i,), memory_space=pltpu.SMEM)
    row_spec = pl.BlockSpec((tb, NJ), lambda i: (i, 0))
    tok_spec = pl.BlockSpec((tb, SUBLANES, LANES), lambda i: (i, 0, 0))
    tab_spec = pl.BlockSpec((table_rows, LANES), lambda i: (0, 0), pipeline_mode=pl.Buffered(1))
    return idx_spec, row_spec, tok_spec, tab_spec


_W_SCRATCH = pltpu.VMEM((NJ * TABLE_ROWS_PER_EXPERT, LANES), jnp.uint32)
_SQUARE_SCRATCH = pltpu.VMEM((NJ, LANES), jnp.float32)


def _peer_up(idx, gate, xn, tab):
    n, tb = gate.shape[0], TB_PEER
    idx_spec, row_spec, tok_spec, tab_spec = _peer_specs(n, tab.shape[0], tb)
    return pl.pallas_call(
        functools.partial(_peer_up_kernel, tb=tb),
        grid=(n // tb,),
        in_specs=[idx_spec, row_spec, tok_spec, tab_spec],
        out_specs=row_spec,
        out_shape=jax.ShapeDtypeStruct((n, NJ), jnp.float32),
        scratch_shapes=[_W_SCRATCH, _SQUARE_SCRATCH, _SQUARE_SCRATCH, pltpu.VMEM((tb, NJ), jnp.float32)],
        compiler_params=pltpu.CompilerParams(dimension_semantics=("arbitrary",), vmem_limit_bytes=VMEM_LIMIT),
        name="peer_up",
    )(idx.reshape(-1), gate, xn.reshape(n, SUBLANES, LANES), tab)


def _peer_down(idx, coef, x2, tab):
    n, tb = x2.shape[0], TB_PEER
    idx_spec, row_spec, tok_spec, tab_spec = _peer_specs(n, tab.shape[0], tb)
    return pl.pallas_call(
        functools.partial(_peer_down_kernel, tb=tb),
        grid=(n // tb,),
        in_specs=[idx_spec, row_spec, tok_spec, tab_spec],
        out_specs=tok_spec,
        out_shape=jax.ShapeDtypeStruct((n, SUBLANES, LANES), jnp.float32),
        scratch_shapes=[_W_SCRATCH, _SQUARE_SCRATCH, _SQUARE_SCRATCH],
        compiler_params=pltpu.CompilerParams(dimension_semantics=("arbitrary",), vmem_limit_bytes=VMEM_LIMIT),
        name="peer_down",
    )(idx.reshape(-1), coef, x2.reshape(n, SUBLANES, LANES), tab).reshape(n, D_MODEL)


def _layer(x2d, batch, seq, mix_norm_g, w_in, q_norm_g, k_norm_g, v_gate_norm_g, w_spatial, b_spatial, attn_out_g,
           gate_out_g, w_out, ffn_norm_g, w_query, sub_keys_a, sub_keys_b, expert_u, expert_v):
    bf16 = jnp.bfloat16
    heads = ATTN_WIDTH // HEAD_DIM
    lane = jnp.arange(ATTN_WIDTH)
    g64 = (lane[:, None] // HEAD_DIM == lane[None, :] // HEAD_DIM).astype(bf16)
    gq = (jnp.tile(q_norm_g, heads) * HEAD_DIM ** -0.5)[None, :]
    gk = jnp.tile(k_norm_g, heads)[None, :]
    bsp = jnp.repeat(b_spatial.T, GMLP_WIDTH // GMLP_GROUPS, axis=1)
    *qkv, gated = _in_proj(x2d, batch, seq, mix_norm_g[None, :], w_in.astype(bf16), gq, gk, g64,
                           v_gate_norm_g.reshape(1, GMLP_WIDTH), w_spatial, bsp, gate_out_g[None, :])
    n_pat = len(PATTERNS)
    outs, lses = [], []
    for p, (window, dilation) in enumerate(PATTERNS):
        assert window // dilation == N_BACK
        o, l = _attention(qkv[p], qkv[n_pat + p], qkv[2 * n_pat + p], dilation)
        outs.append(o)
        lses.append(l)
    x2 = _out_proj(outs, lses, gated, x2d, seq, w_out.astype(bf16), attn_out_g[None, :])
    xn, idx, gate = _peer_route(x2, ffn_norm_g[None, :], w_query.astype(bf16), sub_keys_a.astype(bf16),
                                sub_keys_b.astype(bf16))
    coef = _peer_up(idx, gate, xn, _pack_table(expert_u))
    return _peer_down(idx, coef, x2, _pack_table(expert_v))


def kernel(x, mix_norm_g, w_in, q_norm_g, k_norm_g, v_gate_norm_g, w_spatial, b_spatial, attn_out_g, gate_out_g,
           w_out, ffn_norm_g, w_query, sub_keys_a, sub_keys_b, expert_u, expert_v):
    batch, seq, d = x.shape
    assert d == D_MODEL and seq % TM_PROJ == 0 and (batch * seq) % TB_PEER == 0 and TB_PEER % 2 == 0
    for _, dilation in PATTERNS:
        sub_len = seq // dilation
        assert seq % dilation == 0 and sub_len % ATTN_BLOCK == 0 and sub_len % min(TQ_ATTN, sub_len) == 0
    x2d = x.reshape(batch * seq, d)
    for l in range(mix_norm_g.shape[0]):
        x2d = _layer(x2d, batch, seq, mix_norm_g[l], w_in[l], q_norm_g[l], k_norm_g[l], v_gate_norm_g[l],
                     w_spatial[l], b_spatial[l], attn_out_g[l], gate_out_g[l], w_out[l], ffn_norm_g[l],
                     w_query[l], sub_keys_a[l], sub_keys_b[l], expert_u[l], expert_v[l])
    return x2d.reshape(batch, seq, d)
```

```python
import functools

import jax
import jax.numpy as jnp
from jax import lax
from jax.experimental import pallas as pl
from jax.experimental.pallas import tpu as pltpu

D_MODEL = 1024
ATTN_WIDTH = 512
HEAD_DIM = 64
HEAD_PAIRS = ATTN_WIDTH // 128
PATTERNS = ((128, 1), (512, 4), (2048, 16))
N_BACK = 128
ATTN_BLOCK = 128
GMLP_WIDTH = 512
GMLP_CHUNK = 128
GMLP_GROUPS = 4
IN_WIDTH = 3 * ATTN_WIDTH + 2 * GMLP_WIDTH
PEER_HEADS = 8
PEER_TOPK = 16
N_KEYS = 128
D_KEY = 256
NJ = PEER_HEADS * PEER_TOPK
TABLE_ROWS_PER_EXPERT = 4
RMS_EPS = 1e-6
NEG = -1e30

LANES = 128
SUBLANES = 8
VMEM_LIMIT = 56 * 2 ** 20

TM_PROJ = 512
TQ_ATTN = 512
TM_ROUTE = 256
TB_PEER = 256
TE_PACK = 512


def _gelu(x):
    return 0.5 * x * (1.0 + lax.erf(x * (2.0 ** -0.5)))


def _split_bf16(x):
    hi = x.astype(jnp.bfloat16)
    lo = (x - hi.astype(jnp.float32)).astype(jnp.bfloat16)
    return hi, lo


def _dot(a, b):
    return jnp.dot(a, b, preferred_element_type=jnp.float32)


def _dot_nt(a, b):
    return lax.dot_general(a, b, (((1,), (1,)), ((), ())), preferred_element_type=jnp.float32)


def _store_views(val, stage_ref, out_refs):
    chunks = ATTN_WIDTH // LANES
    for c in range(chunks):
        stage_ref[c] = val[:, c * LANES:(c + 1) * LANES]
    for (_, dilation), o_ref in zip(PATTERNS, out_refs):
        rows = val.shape[0] // dilation
        for r in range(dilation):
            for c in range(chunks):
                lanes = pl.ds(r * ATTN_WIDTH + c * LANES, LANES)
                o_ref[0, :, lanes] = stage_ref[c, pl.ds(r, rows, stride=dilation), :].astype(jnp.bfloat16)


def _in_proj_kernel(x_ref, gmix_ref, win_ref, gq_ref, gk_ref, g64_ref, gvg_ref, wsp_ref, bsp_ref, gout_ref,
                    q1_ref, q2_ref, q3_ref, k1_ref, k2_ref, k3_ref, v1_ref, v2_ref, v3_ref, gated_ref,
                    gs_ref, stage_ref):
    x = x_ref[...]
    ms = jnp.mean(x * x, axis=-1, keepdims=True)
    hn = (x * lax.rsqrt(ms + RMS_EPS) * gmix_ref[...]).astype(jnp.bfloat16)

    def head_norm(t, g):
        hi, lo = _split_bf16(t * t)
        msq = (_dot(hi, g64_ref[...]) + _dot(lo, g64_ref[...])) * (1.0 / HEAD_DIM)
        return t * lax.rsqrt(msq + RMS_EPS) * g

    q = _dot(hn, win_ref[:, 0:ATTN_WIDTH])
    _store_views(head_norm(q, gq_ref[...]), stage_ref, (q1_ref, q2_ref, q3_ref))
    k = _dot(hn, win_ref[:, ATTN_WIDTH:2 * ATTN_WIDTH])
    _store_views(head_norm(k, gk_ref[...]), stage_ref, (k1_ref, k2_ref, k3_ref))
    _store_views(_dot(hn, win_ref[:, 2 * ATTN_WIDTH:3 * ATTN_WIDTH]), stage_ref, (v1_ref, v2_ref, v3_ref))

    u = _gelu(_dot(hn, win_ref[:, 3 * ATTN_WIDTH:3 * ATTN_WIDTH + GMLP_WIDTH]))
    gv = _gelu(_dot(hn, win_ref[:, 3 * ATTN_WIDTH + GMLP_WIDTH:IN_WIDTH]))
    row = lax.broadcasted_iota(jnp.int32, (GMLP_CHUNK, GMLP_CHUNK), 0)
    col = lax.broadcasted_iota(jnp.int32, (GMLP_CHUNK, GMLP_CHUNK), 1)
    causal = col <= row
    n_chunks = x.shape[0] // GMLP_CHUNK
    for g in range(GMLP_GROUPS):
        cs = slice(g * LANES, (g + 1) * LANES)
        vg = gv[:, cs]
        msg = jnp.mean(vg * vg, axis=-1, keepdims=True)
        vn = (vg * lax.rsqrt(msg + RMS_EPS) * gvg_ref[:, cs]).astype(jnp.bfloat16)
        w = jnp.where(causal, wsp_ref[g], 0.0).astype(jnp.bfloat16)
        for c in range(n_chunks):
            rs = slice(c * GMLP_CHUNK, (c + 1) * GMLP_CHUNK)
            z = _dot(w, vn[rs, :]) + bsp_ref[:, cs]
            gs_ref[rs, cs] = u[rs, cs] * z
    gated = gs_ref[...]
    msg = jnp.mean(gated * gated, axis=-1, keepdims=True)
    gated_ref[...] = (gated * lax.rsqrt(msg + RMS_EPS) * gout_ref[...]).astype(jnp.bfloat16)


def _view_spec(seq, tm, dilation):
    blocks_per_row = seq // tm
    return pl.BlockSpec((1, tm // dilation, dilation * ATTN_WIDTH),
                        lambda i: (i // blocks_per_row, i % blocks_per_row, 0))


def _view_shape(batch, seq, dilation, dtype):
    return jax.ShapeDtypeStruct((batch, seq // dilation, dilation * ATTN_WIDTH), dtype)


def _in_proj(x2d, batch, seq, gmix, win, gq, gk, g64, gvg, wsp, bsp, gout):
    n = x2d.shape[0]
    tm = TM_PROJ
    full = lambda shape: pl.BlockSpec(shape, lambda i: (0,) * len(shape))
    tok = lambda w: pl.BlockSpec((tm, w), lambda i: (i, 0))
    view_specs = [_view_spec(seq, tm, d) for _, d in PATTERNS] * 3
    view_shapes = [_view_shape(batch, seq, d, jnp.bfloat16) for _, d in PATTERNS] * 3
    return pl.pallas_call(
        _in_proj_kernel,
        grid=(n // tm,),
        in_specs=[tok(D_MODEL), full((1, D_MODEL)), full((D_MODEL, IN_WIDTH)), full((1, ATTN_WIDTH)),
                  full((1, ATTN_WIDTH)), full((ATTN_WIDTH, ATTN_WIDTH)), full((1, GMLP_WIDTH)),
                  full((GMLP_GROUPS, GMLP_CHUNK, GMLP_CHUNK)), full((GMLP_CHUNK, GMLP_WIDTH)), full((1, GMLP_WIDTH))],
        out_specs=view_specs + [tok(GMLP_WIDTH)],
        out_shape=view_shapes + [jax.ShapeDtypeStruct((n, GMLP_WIDTH), jnp.bfloat16)],
        scratch_shapes=[pltpu.VMEM((tm, GMLP_WIDTH), jnp.float32),
                        pltpu.VMEM((ATTN_WIDTH // LANES, tm, LANES), jnp.float32)],
        compiler_params=pltpu.CompilerParams(dimension_semantics=("arbitrary",), vmem_limit_bytes=VMEM_LIMIT),
        name="in_proj",
    )(x2d, gmix, win, gq, gk, g64, gvg, wsp, bsp, gout)


def _attn_kernel(q_ref, kp_ref, kc_ref, vp_ref, vc_ref, o_ref, l_ref):
    tq = q_ref.shape[1]
    have_prev = pl.program_id(2) > 0
    qi = lax.broadcasted_iota(jnp.int32, (ATTN_BLOCK, 2 * ATTN_BLOCK), 0)
    kj = lax.broadcasted_iota(jnp.int32, (ATTN_BLOCK, 2 * ATTN_BLOCK), 1)
    rel = qi + ATTN_BLOCK - kj
    band = (rel >= 0) & (rel <= N_BACK)
    lane = lax.broadcasted_iota(jnp.int32, (ATTN_BLOCK, LANES), 1)
    for qb in range(tq // ATTN_BLOCK):
        rs = slice(qb * ATTN_BLOCK, (qb + 1) * ATTN_BLOCK)
        if qb == 0:
            mask = band & ((kj >= ATTN_BLOCK) | have_prev)
        else:
            mask = band
        for p in range(HEAD_PAIRS):
            cs = slice(p * LANES, (p + 1) * LANES)
            qp = q_ref[0, rs, cs]
            if qb == 0:
                kprev, vprev = kp_ref[0, :, cs], vp_ref[0, :, cs]
            else:
                ps = slice((qb - 1) * ATTN_BLOCK, qb * ATTN_BLOCK)
                kprev, vprev = kc_ref[0, ps, cs], vc_ref[0, ps, cs]
            keys = jnp.concatenate([kprev, kc_ref[0, rs, cs]], axis=0)
            vals = jnp.concatenate([vprev, vc_ref[0, rs, cs]], axis=0)
            out_pair = jnp.zeros((ATTN_BLOCK, LANES), jnp.float32)
            lse_pair = jnp.zeros((ATTN_BLOCK, LANES), jnp.float32)
            for hh in range(2):
                in_head = (lane >= hh * HEAD_DIM) & (lane < (hh + 1) * HEAD_DIM)
                s = _dot_nt(jnp.where(in_head, qp, jnp.zeros_like(qp)), keys)
                s = jnp.where(mask, s, NEG)
                m = jnp.max(s, axis=-1, keepdims=True)
                e = jnp.exp(s - m)
                den = jnp.sum(e, axis=-1, keepdims=True)
                o = _dot(e.astype(jnp.bfloat16), vals) / den
                out_pair = jnp.where(in_head, o, out_pair)
                lse_pair = jnp.where(in_head, m + jnp.log(den), lse_pair)
            o_ref[0, rs, cs] = out_pair
            l_ref[0, rs, cs] = lse_pair


def _attention(q, k, v, dilation):
    batch, L, _ = q.shape
    tq = min(TQ_ATTN, L)
    cur = pl.BlockSpec((1, tq, ATTN_WIDTH), lambda b, r, i: (b, i, r))
    prev = pl.BlockSpec((1, ATTN_BLOCK, ATTN_WIDTH),
                        lambda b, r, i: (b, jnp.maximum(i * (tq // ATTN_BLOCK) - 1, 0), r))
    out = jax.ShapeDtypeStruct((batch, L, dilation * ATTN_WIDTH), jnp.float32)
    return pl.pallas_call(
        _attn_kernel,
        grid=(batch, dilation, L // tq),
        in_specs=[cur, prev, cur, prev, cur],
        out_specs=[cur, cur],
        out_shape=[out, out],
        compiler_params=pltpu.CompilerParams(dimension_semantics=("arbitrary",) * 3, vmem_limit_bytes=VMEM_LIMIT),
        name=f"attention_d{dilation}",
    )(q, k, k, v, v)


def _load_view(v_ref, stage_ref, dilation):
    if dilation == 1:
        return v_ref[0]
    rows = v_ref.shape[1]
    chunks = ATTN_WIDTH // LANES
    for r in range(dilation):
        for c in range(chunks):
            stage_ref[c, pl.ds(r, rows, stride=dilation), :] = v_ref[0, :, pl.ds(r * ATTN_WIDTH + c * LANES, LANES)]
    return jnp.concatenate([stage_ref[c] for c in range(chunks)], axis=1)


def _store_token_tiles(val, o_ref):
    chunks = val.shape[1] // LANES
    for c in range(chunks):
        o_ref[pl.ds(c, val.shape[0], stride=chunks), :] = val[:, c * LANES:(c + 1) * LANES]


def _out_proj_kernel(o1_ref, o2_ref, o3_ref, l1_ref, l2_ref, l3_ref, gated_ref, x_ref, wout_ref, gattn_ref, x2_ref,
                     x2t_ref, so2_ref, so3_ref, sl2_ref, sl3_ref):
    dil = [d for _, d in PATTERNS]
    l1 = _load_view(l1_ref, None, dil[0])
    l2 = _load_view(l2_ref, sl2_ref, dil[1])
    l3 = _load_view(l3_ref, sl3_ref, dil[2])
    o1 = _load_view(o1_ref, None, dil[0])
    o2 = _load_view(o2_ref, so2_ref, dil[1])
    o3 = _load_view(o3_ref, so3_ref, dil[2])
    m = jnp.maximum(jnp.maximum(l1, l2), l3)
    e1, e2, e3 = jnp.exp(l1 - m), jnp.exp(l2 - m), jnp.exp(l3 - m)
    attn = (e1 * o1 + e2 * o2 + e3 * o3) / (e1 + e2 + e3)
    ms = jnp.mean(attn * attn, axis=-1, keepdims=True)
    attn_n = (attn * lax.rsqrt(ms + RMS_EPS) * gattn_ref[...]).astype(jnp.bfloat16)
    y = _dot(attn_n, wout_ref[0:ATTN_WIDTH, :]) + _dot(gated_ref[...], wout_ref[ATTN_WIDTH:, :])
    x2 = x_ref[...] + y
    x2_ref[...] = x2
    _store_token_tiles(x2, x2t_ref)


def _out_proj(outs, lses, gated, x2d, seq, wout, gattn):
    n = x2d.shape[0]
    tm = TM_PROJ
    tok = lambda w: pl.BlockSpec((tm, w), lambda i: (i, 0))
    full = lambda shape: pl.BlockSpec(shape, lambda i: (0,) * len(shape))
    view_specs = [_view_spec(seq, tm, d) for _, d in PATTERNS] * 2
    stage = pltpu.VMEM((ATTN_WIDTH // LANES, tm, LANES), jnp.float32)
    return pl.pallas_call(
        _out_proj_kernel,
        grid=(n // tm,),
        in_specs=view_specs + [tok(GMLP_WIDTH), tok(D_MODEL), full((D_MODEL, D_MODEL)), full((1, ATTN_WIDTH))],
        out_specs=[tok(D_MODEL), pl.BlockSpec((tm * SUBLANES, LANES), lambda i: (i, 0))],
        out_shape=[jax.ShapeDtypeStruct((n, D_MODEL), jnp.float32),
                   jax.ShapeDtypeStruct((n * SUBLANES, LANES), jnp.float32)],
        scratch_shapes=[stage] * 4,
        compiler_params=pltpu.CompilerParams(dimension_semantics=("arbitrary",), vmem_limit_bytes=VMEM_LIMIT),
        name="out_proj",
    )(*outs, *lses, gated, x2d, wout, gattn)


CAND_J_COUNT = (16, 8, 5, 4, 3, 2, 2, 2)


def _top16_groups(vals, ids):
    vals = list(vals)
    out_v, out_i = [], []
    for _ in range(PEER_TOPK):
        bv, bi = vals[0], ids[0]
        for v, i in zip(vals[1:], ids[1:]):
            gt = v > bv
            bv = jnp.where(gt, v, bv)
            bi = jnp.where(gt, i, bi)
        for sh in (4, 2, 1):
            rv, ri = pltpu.roll(bv, sh, axis=0), pltpu.roll(bi, sh, axis=0)
            better = (rv > bv) | ((rv == bv) & (ri < bi))
            bv = jnp.where(better, rv, bv)
            bi = jnp.where(better, ri, bi)
        out_v.append(bv)
        out_i.append(bi)
        vals = [jnp.where(i == bi, -jnp.inf, v) for v, i in zip(vals, ids)]
    return out_v, out_i


def _stack(reps, start):
    row = lax.broadcasted_iota(jnp.int32, reps[0].shape, 0)
    out = reps[start + SUBLANES - 1]
    for r in range(SUBLANES - 2, -1, -1):
        out = jnp.where(row == r, reps[start + r], out)
    return out


def _select_experts(sa, sb):
    t = sa.shape[1]
    row = lax.broadcasted_iota(jnp.int32, (SUBLANES, t), 0)
    groups = lambda s: [s[g * SUBLANES:(g + 1) * SUBLANES, :] for g in range(N_KEYS // SUBLANES)]
    key_ids = [row + g * SUBLANES for g in range(N_KEYS // SUBLANES)]
    va, ia = _top16_groups(groups(sa), key_ids)
    vb, ib = _top16_groups(groups(sb), key_ids)
    vb_lo, vb_hi, va_hi = _stack(vb, 0), _stack(vb, SUBLANES), _stack(va, SUBLANES)
    cand = [va[0] + vb_lo, va[0] + vb_hi]
    cand_ids = [row, row + SUBLANES]
    for i in range(1, SUBLANES):
        cand.append(jnp.where(row < CAND_J_COUNT[i], va[i] + vb_lo, -jnp.inf))
        cand_ids.append(row + i * PEER_TOPK)
    cand.append(va_hi + vb[0])
    cand_ids.append((row + SUBLANES) * PEER_TOPK)
    top_s, pos = _top16_groups(cand, cand_ids)
    idx_halves, e_halves = [], []
    for h in range(2):
        p = _stack(pos, h * SUBLANES)
        pa, pb = p >> 4, p & (PEER_TOPK - 1)
        ea, eb = jnp.zeros_like(p), jnp.zeros_like(p)
        for i in range(PEER_TOPK):
            ea = jnp.where(pa == i, ia[i], ea)
            eb = jnp.where(pb == i, ib[i], eb)
        idx_halves.append((ea * N_KEYS + eb) * TABLE_ROWS_PER_EXPERT)
        e_halves.append(jnp.exp(_stack(top_s, h * SUBLANES) - top_s[0]))
    den = e_halves[0] + e_halves[1]
    for sh in (4, 2, 1):
        den = den + pltpu.roll(den, sh, axis=0)
    return jnp.concatenate(idx_halves, axis=0), jnp.concatenate([e / den for e in e_halves], axis=0)


def _route_kernel(x2_ref, gffn_ref, wq_ref, ka_ref, kb_ref, xn_ref, idx_ref, gate_ref):
    x = x2_ref[...]
    ms = jnp.mean(x * x, axis=-1, keepdims=True)
    xn = x * lax.rsqrt(ms + RMS_EPS) * gffn_ref[...]
    _store_token_tiles(xn, xn_ref)
    qh = _dot(xn.astype(jnp.bfloat16), wq_ref[...])
    half = D_KEY // 2
    for lt in range(x.shape[0] // LANES):
        ts = slice(lt * LANES, (lt + 1) * LANES)
        idx_rows, gate_rows = [], []
        for h in range(PEER_HEADS):
            qa = qh[ts, h * D_KEY:h * D_KEY + half].astype(jnp.bfloat16)
            qb = qh[ts, h * D_KEY + half:(h + 1) * D_KEY].astype(jnp.bfloat16)
            idx_h, gate_h = _select_experts(_dot_nt(ka_ref[...], qa), _dot_nt(kb_ref[...], qb))
            idx_rows.append(idx_h)
            gate_rows.append(gate_h)
        idx_ref[ts, :] = jnp.concatenate(idx_rows, axis=0).T
        gate_ref[ts, :] = jnp.concatenate(gate_rows, axis=0).T


def _peer_route(x2, gffn, wq, ka, kb):
    n = x2.shape[0]
    tm = TM_ROUTE
    tok = lambda w: pl.BlockSpec((tm, w), lambda i: (i, 0))
    full = lambda shape: pl.BlockSpec(shape, lambda i: (0,) * len(shape))
    return pl.pallas_call(
        _route_kernel,
        grid=(n // tm,),
        in_specs=[tok(D_MODEL), full((1, D_MODEL)), full((D_MODEL, PEER_HEADS * D_KEY)),
                  full((N_KEYS, D_KEY // 2)), full((N_KEYS, D_KEY // 2))],
        out_specs=[pl.BlockSpec((tm * SUBLANES, LANES), lambda i: (i, 0)), tok(NJ), tok(NJ)],
        out_shape=[jax.ShapeDtypeStruct((n * SUBLANES, LANES), jnp.float32),
                   jax.ShapeDtypeStruct((n, NJ), jnp.int32),
                   jax.ShapeDtypeStruct((n, NJ), jnp.float32)],
        compiler_params=pltpu.CompilerParams(dimension_semantics=("arbitrary",), vmem_limit_bytes=VMEM_LIMIT),
        name="peer_route",
    )(x2, gffn, wq, ka, kb)


def _pack_kernel(t_ref, o_ref):
    x = t_ref[...]
    bits = lax.bitcast_convert_type(x.astype(jnp.bfloat16).astype(jnp.float32), jnp.uint32)
    half = D_MODEL // 2
    word = (bits[:, :half] >> 16) | (bits[:, half:] & jnp.uint32(0xFFFF0000))
    for s in range(TABLE_ROWS_PER_EXPERT):
        o_ref[pl.ds(s, x.shape[0], stride=TABLE_ROWS_PER_EXPERT), :] = word[:, s * LANES:(s + 1) * LANES]


def _pack_table(tab):
    e, te = tab.shape[0], TE_PACK
    return pl.pallas_call(
        _pack_kernel,
        grid=(e // te,),
        in_specs=[pl.BlockSpec((te, D_MODEL), lambda i: (i, 0))],
        out_specs=pl.BlockSpec((te * TABLE_ROWS_PER_EXPERT, LANES), lambda i: (i, 0)),
        out_shape=jax.ShapeDtypeStruct((e * TABLE_ROWS_PER_EXPERT, LANES), jnp.uint32),
        compiler_params=pltpu.CompilerParams(dimension_semantics=("arbitrary",), vmem_limit_bytes=VMEM_LIMIT),
        name="pack_table",
    )(tab)


def _unpack(w):
    lo = lax.bitcast_convert_type(w << 16, jnp.float32)
    hi = lax.bitcast_convert_type(w & jnp.uint32(0xFFFF0000), jnp.float32)
    return lo, hi


def _gather_rows(idx_ref, base, tab_ref, w_ref):
    for g in range(NJ // SUBLANES):
        window = idx_ref.at[pl.ds(base + g * SUBLANES, SUBLANES)]
        for r in range(SUBLANES):
            off = pl.multiple_of(window[r], TABLE_ROWS_PER_EXPERT)
            w_ref[pl.ds((g * SUBLANES + r) * TABLE_ROWS_PER_EXPERT, TABLE_ROWS_PER_EXPERT), :] = (
                tab_ref[pl.ds(off, TABLE_ROWS_PER_EXPERT), :])


def _group_chunk(w_ref, g, s):
    return _unpack(w_ref[pl.ds(g * SUBLANES * TABLE_ROWS_PER_EXPERT + s, SUBLANES, stride=TABLE_ROWS_PER_EXPERT), :])


def _peer_up_kernel(idx_ref, gate_ref, xn_ref, tab_ref, coef_ref, w_ref, act_ref, *, tb):
    def partial_dots(t):
        _gather_rows(idx_ref, t * NJ, tab_ref, w_ref)
        xt = xn_ref[t]
        xb = [jnp.broadcast_to(xt[c:c + 1, :], (SUBLANES, LANES)) for c in range(SUBLANES)]
        parts = []
        for g in range(NJ // SUBLANES):
            acc = None
            for s in range(4):
                lo, hi = _group_chunk(w_ref, g, s)
                term = lo * xb[s] + hi * xb[4 + s]
                acc = term if acc is None else acc + term
            parts.append(acc)
        return jnp.concatenate(parts, axis=0)

    def lane_sums(p, t):
        act_ref[pl.ds(t, 1), :] = jnp.sum(p.T, axis=0, keepdims=True)

    def body(i, p_prev):
        t0 = 2 * i
        lane_sums(p_prev, jnp.maximum(t0 - 1, 0))
        pa = partial_dots(t0)
        lane_sums(pa, t0)
        return partial_dots(t0 + 1)

    p_last = lax.fori_loop(0, tb // 2, body, jnp.zeros((NJ, LANES), jnp.float32))
    lane_sums(p_last, tb - 1)
    coef_ref[...] = gate_ref[...] * _gelu(act_ref[...])


def _peer_down_kernel(idx_ref, coef_ref, x2_ref, tab_ref, o_ref, w_ref, ca_ref, cb_ref, *, tb):
    def spread(t, c_ref):
        c_ref[...] = jnp.broadcast_to(coef_ref[pl.ds(t, 1), :], (NJ, LANES)).T

    def weighted_sum(t, c_ref):
        _gather_rows(idx_ref, t * NJ, tab_ref, w_ref)
        acc_lo, acc_hi = [None] * 4, [None] * 4
        for g in range(NJ // SUBLANES):
            cg = c_ref[pl.ds(g * SUBLANES, SUBLANES), :]
            for s in range(4):
                lo, hi = _group_chunk(w_ref, g, s)
                acc_lo[s] = lo * cg if acc_lo[s] is None else acc_lo[s] + lo * cg
                acc_hi[s] = hi * cg if acc_hi[s] is None else acc_hi[s] + hi * cg
        rows = [jnp.sum(a, axis=0, keepdims=True) for a in acc_lo + acc_hi]
        o_ref[t] = x2_ref[t] + jnp.concatenate(rows, axis=0)

    spread(0, ca_ref)

    def body(i, carry):
        t0 = 2 * i
        spread(t0 + 1, cb_ref)
        weighted_sum(t0, ca_ref)
        spread(jnp.minimum(t0 + 2, tb - 1), ca_ref)
        weighted_sum(t0 + 1, cb_ref)
        return carry

    lax.fori_loop(0, tb // 2, body, 0)


def _peer_specs(n, table_rows, tb):
    idx_spec = pl.BlockSpec((tb * NJ,), lambda i: (i,), memory_space=pltpu.SMEM)
    row_spec = pl.BlockSpec((tb, NJ), lambda i: (i, 0))
    tok_spec = pl.BlockSpec((tb, SUBLANES, LANES), lambda i: (i, 0, 0))
    tab_spec = pl.BlockSpec((table_rows, LANES), lambda i: (0, 0), pipeline_mode=pl.Buffered(1))
    return idx_spec, row_spec, tok_spec, tab_spec


_W_SCRATCH = pltpu.VMEM((NJ * TABLE_ROWS_PER_EXPERT, LANES), jnp.uint32)
_SQUARE_SCRATCH = pltpu.VMEM((NJ, LANES), jnp.float32)


def _peer_up(idx, gate, xn, tab):
    n, tb = gate.shape[0], TB_PEER
    idx_spec, row_spec, tok_spec, tab_spec = _peer_specs(n, tab.shape[0], tb)
    return pl.pallas_call(
        functools.partial(_peer_up_kernel, tb=tb),
        grid=(n // tb,),
        in_specs=[idx_spec, row_spec, tok_spec, tab_spec],
        out_specs=row_spec,
        out_shape=jax.ShapeDtypeStruct((n, NJ), jnp.float32),
        scratch_shapes=[_W_SCRATCH, pltpu.VMEM((tb, NJ), jnp.float32)],
        compiler_params=pltpu.CompilerParams(dimension_semantics=("arbitrary",), vmem_limit_bytes=VMEM_LIMIT),
        name="peer_up",
    )(idx.reshape(-1), gate, xn.reshape(n, SUBLANES, LANES), tab)


def _peer_down(idx, coef, x2t, tab):
    n, tb = coef.shape[0], TB_PEER
    idx_spec, row_spec, tok_spec, tab_spec = _peer_specs(n, tab.shape[0], tb)
    return pl.pallas_call(
        functools.partial(_peer_down_kernel, tb=tb),
        grid=(n // tb,),
        in_specs=[idx_spec, row_spec, tok_spec, tab_spec],
        out_specs=tok_spec,
        out_shape=jax.ShapeDtypeStruct((n, SUBLANES, LANES), jnp.float32),
        scratch_shapes=[_W_SCRATCH, _SQUARE_SCRATCH, _SQUARE_SCRATCH],
        compiler_params=pltpu.CompilerParams(dimension_semantics=("arbitrary",), vmem_limit_bytes=VMEM_LIMIT),
        name="peer_down",
    )(idx.reshape(-1), coef, x2t.reshape(n, SUBLANES, LANES), tab).reshape(n, D_MODEL)


def _layer(x2d, batch, seq, mix_norm_g, w_in, q_norm_g, k_norm_g, v_gate_norm_g, w_spatial, b_spatial, attn_out_g,
           gate_out_g, w_out, ffn_norm_g, w_query, sub_keys_a, sub_keys_b, expert_u, expert_v):
    bf16 = jnp.bfloat16
    heads = ATTN_WIDTH // HEAD_DIM
    lane = jnp.arange(ATTN_WIDTH)
    g64 = (lane[:, None] // HEAD_DIM == lane[None, :] // HEAD_DIM).astype(bf16)
    gq = (jnp.tile(q_norm_g, heads) * HEAD_DIM ** -0.5)[None, :]
    gk = jnp.tile(k_norm_g, heads)[None, :]
    bsp = jnp.repeat(b_spatial.T, GMLP_WIDTH // GMLP_GROUPS, axis=1)
    *qkv, gated = _in_proj(x2d, batch, seq, mix_norm_g[None, :], w_in.astype(bf16), gq, gk, g64,
                           v_gate_norm_g.reshape(1, GMLP_WIDTH), w_spatial, bsp, gate_out_g[None, :])
    n_pat = len(PATTERNS)
    outs, lses = [], []
    for p, (window, dilation) in enumerate(PATTERNS):
        assert window // dilation == N_BACK
        o, l = _attention(qkv[p], qkv[n_pat + p], qkv[2 * n_pat + p], dilation)
        outs.append(o)
        lses.append(l)
    x2, x2t = _out_proj(outs, lses, gated, x2d, seq, w_out.astype(bf16), attn_out_g[None, :])
    xn, idx, gate = _peer_route(x2, ffn_norm_g[None, :], w_query.astype(bf16), sub_keys_a.astype(bf16),
                                sub_keys_b.astype(bf16))
    coef = _peer_up(idx, gate, xn, _pack_table(expert_u))
    return _peer_down(idx, coef, x2t, _pack_table(expert_v))


def kernel(x, mix_norm_g, w_in, q_norm_g, k_norm_g, v_gate_norm_g, w_spatial, b_spatial, attn_out_g, gate_out_g,
           w_out, ffn_norm_g, w_query, sub_keys_a, sub_keys_b, expert_u, expert_v):
    batch, seq, d = x.shape
    assert d == D_MODEL and seq % TM_PROJ == 0 and (batch * seq) % TB_PEER == 0 and TB_PEER % 2 == 0
    for _, dilation in PATTERNS:
        sub_len = seq // dilation
        assert seq % dilation == 0 and sub_len % ATTN_BLOCK == 0 and sub_len % min(TQ_ATTN, sub_len) == 0
    x2d = x.reshape(batch * seq, d)
    for l in range(mix_norm_g.shape[0]):
        x2d = _layer(x2d, batch, seq, mix_norm_g[l], w_in[l], q_norm_g[l], k_norm_g[l], v_gate_norm_g[l],
                     w_spatial[l], b_spatial[l], attn_out_g[l], gate_out_g[l], w_out[l], ffn_norm_g[l],
                     w_query[l], sub_keys_a[l], sub_keys_b[l], expert_u[l], expert_v[l])
    return x2d.reshape(batch, seq, d)
```

```python
import functools

import jax
import jax.numpy as jnp
from jax import lax
from jax.experimental import pallas as pl
from jax.experimental.pallas import tpu as pltpu

D_MODEL = 1024
ATTN_WIDTH = 512
HEAD_DIM = 64
HEAD_PAIRS = ATTN_WIDTH // 128
PATTERNS = ((128, 1), (512, 4), (2048, 16))
N_BACK = 128
ATTN_BLOCK = 128
GMLP_WIDTH = 512
GMLP_CHUNK = 128
GMLP_GROUPS = 4
IN_WIDTH = 3 * ATTN_WIDTH + 2 * GMLP_WIDTH
PEER_HEADS = 8
PEER_TOPK = 16
N_KEYS = 128
D_KEY = 256
NJ = PEER_HEADS * PEER_TOPK
TABLE_ROWS_PER_EXPERT = 4
RMS_EPS = 1e-6
NEG = -1e30

LANES = 128
SUBLANES = 8
VMEM_LIMIT = 56 * 2 ** 20

TM_PROJ = 512
TQ_ATTN = 512
TM_ROUTE = 256
TB_PEER = 256
TE_PACK = 512


def _gelu(x):
    return 0.5 * x * (1.0 + lax.erf(x * (2.0 ** -0.5)))


def _split_bf16(x):
    hi = x.astype(jnp.bfloat16)
    lo = (x - hi.astype(jnp.float32)).astype(jnp.bfloat16)
    return hi, lo


def _dot(a, b):
    return jnp.dot(a, b, preferred_element_type=jnp.float32)


def _dot_nt(a, b):
    return lax.dot_general(a, b, (((1,), (1,)), ((), ())), preferred_element_type=jnp.float32)


def _store_views(val, stage_ref, out_refs):
    chunks = ATTN_WIDTH // LANES
    for c in range(chunks):
        stage_ref[c] = val[:, c * LANES:(c + 1) * LANES]
    for (_, dilation), o_ref in zip(PATTERNS, out_refs):
        rows = val.shape[0] // dilation
        for r in range(dilation):
            for c in range(chunks):
                lanes = pl.ds(r * ATTN_WIDTH + c * LANES, LANES)
                o_ref[0, :, lanes] = stage_ref[c, pl.ds(r, rows, stride=dilation), :].astype(jnp.bfloat16)


def _in_proj_kernel(x_ref, gmix_ref, win_ref, gq_ref, gk_ref, g64_ref, gvg_ref, wsp_ref, bsp_ref, gout_ref,
                    q1_ref, q2_ref, q3_ref, k1_ref, k2_ref, k3_ref, v1_ref, v2_ref, v3_ref, gated_ref,
                    gs_ref, stage_ref):
    x = x_ref[...]
    ms = jnp.mean(x * x, axis=-1, keepdims=True)
    hn = (x * lax.rsqrt(ms + RMS_EPS) * gmix_ref[...]).astype(jnp.bfloat16)

    def head_norm(t, g):
        hi, lo = _split_bf16(t * t)
        msq = (_dot(hi, g64_ref[...]) + _dot(lo, g64_ref[...])) * (1.0 / HEAD_DIM)
        return t * lax.rsqrt(msq + RMS_EPS) * g

    q = _dot(hn, win_ref[:, 0:ATTN_WIDTH])
    _store_views(head_norm(q, gq_ref[...]), stage_ref, (q1_ref, q2_ref, q3_ref))
    k = _dot(hn, win_ref[:, ATTN_WIDTH:2 * ATTN_WIDTH])
    _store_views(head_norm(k, gk_ref[...]), stage_ref, (k1_ref, k2_ref, k3_ref))
    _store_views(_dot(hn, win_ref[:, 2 * ATTN_WIDTH:3 * ATTN_WIDTH]), stage_ref, (v1_ref, v2_ref, v3_ref))

    u = _gelu(_dot(hn, win_ref[:, 3 * ATTN_WIDTH:3 * ATTN_WIDTH + GMLP_WIDTH]))
    gv = _gelu(_dot(hn, win_ref[:, 3 * ATTN_WIDTH + GMLP_WIDTH:IN_WIDTH]))
    row = lax.broadcasted_iota(jnp.int32, (GMLP_CHUNK, GMLP_CHUNK), 0)
    col = lax.broadcasted_iota(jnp.int32, (GMLP_CHUNK, GMLP_CHUNK), 1)
    causal = col <= row
    n_chunks = x.shape[0] // GMLP_CHUNK
    for g in range(GMLP_GROUPS):
        cs = slice(g * LANES, (g + 1) * LANES)
        vg = gv[:, cs]
        msg = jnp.mean(vg * vg, axis=-1, keepdims=True)
        vn = (vg * lax.rsqrt(msg + RMS_EPS) * gvg_ref[:, cs]).astype(jnp.bfloat16)
        w = jnp.where(causal, wsp_ref[g], 0.0).astype(jnp.bfloat16)
        for c in range(n_chunks):
            rs = slice(c * GMLP_CHUNK, (c + 1) * GMLP_CHUNK)
            z = _dot(w, vn[rs, :]) + bsp_ref[:, cs]
            gs_ref[rs, cs] = u[rs, cs] * z
    gated = gs_ref[...]
    msg = jnp.mean(gated * gated, axis=-1, keepdims=True)
    gated_ref[...] = (gated * lax.rsqrt(msg + RMS_EPS) * gout_ref[...]).astype(jnp.bfloat16)


def _view_spec(seq, tm, dilation):
    blocks_per_row = seq // tm
    return pl.BlockSpec((1, tm // dilation, dilation * ATTN_WIDTH),
                        lambda i: (i // blocks_per_row, i % blocks_per_row, 0))


def _view_shape(batch, seq, dilation, dtype):
    return jax.ShapeDtypeStruct((batch, seq // dilation, dilation * ATTN_WIDTH), dtype)


def _in_proj(x2d, batch, seq, gmix, win, gq, gk, g64, gvg, wsp, bsp, gout):
    n = x2d.shape[0]
    tm = TM_PROJ
    full = lambda shape: pl.BlockSpec(shape, lambda i: (0,) * len(shape))
    tok = lambda w: pl.BlockSpec((tm, w), lambda i: (i, 0))
    view_specs = [_view_spec(seq, tm, d) for _, d in PATTERNS] * 3
    view_shapes = [_view_shape(batch, seq, d, jnp.bfloat16) for _, d in PATTERNS] * 3
    return pl.pallas_call(
        _in_proj_kernel,
        grid=(n // tm,),
        in_specs=[tok(D_MODEL), full((1, D_MODEL)), full((D_MODEL, IN_WIDTH)), full((1, ATTN_WIDTH)),
                  full((1, ATTN_WIDTH)), full((ATTN_WIDTH, ATTN_WIDTH)), full((1, GMLP_WIDTH)),
                  full((GMLP_GROUPS, GMLP_CHUNK, GMLP_CHUNK)), full((GMLP_CHUNK, GMLP_WIDTH)), full((1, GMLP_WIDTH))],
        out_specs=view_specs + [tok(GMLP_WIDTH)],
        out_shape=view_shapes + [jax.ShapeDtypeStruct((n, GMLP_WIDTH), jnp.bfloat16)],
        scratch_shapes=[pltpu.VMEM((tm, GMLP_WIDTH), jnp.float32),
                        pltpu.VMEM((ATTN_WIDTH // LANES, tm, LANES), jnp.float32)],
        compiler_params=pltpu.CompilerParams(dimension_semantics=("arbitrary",), vmem_limit_bytes=VMEM_LIMIT),
        name="in_proj",
    )(x2d, gmix, win, gq, gk, g64, gvg, wsp, bsp, gout)


def _attn_kernel(q_ref, kp_ref, kc_ref, vp_ref, vc_ref, o_ref, l_ref):
    tq = q_ref.shape[1]
    have_prev = pl.program_id(2) > 0
    qi = lax.broadcasted_iota(jnp.int32, (ATTN_BLOCK, 2 * ATTN_BLOCK), 0)
    kj = lax.broadcasted_iota(jnp.int32, (ATTN_BLOCK, 2 * ATTN_BLOCK), 1)
    rel = qi + ATTN_BLOCK - kj
    band = (rel >= 0) & (rel <= N_BACK)
    lane = lax.broadcasted_iota(jnp.int32, (ATTN_BLOCK, LANES), 1)
    n_lane_tiles = q_ref.shape[2] // LANES
    for qb in range(tq // ATTN_BLOCK):
        rs = slice(qb * ATTN_BLOCK, (qb + 1) * ATTN_BLOCK)
        if qb == 0:
            mask = band & ((kj >= ATTN_BLOCK) | have_prev)
        else:
            mask = band
        for p in range(n_lane_tiles):
            cs = slice(p * LANES, (p + 1) * LANES)
            qp = q_ref[0, rs, cs]
            if qb == 0:
                kprev, vprev = kp_ref[0, :, cs], vp_ref[0, :, cs]
            else:
                ps = slice((qb - 1) * ATTN_BLOCK, qb * ATTN_BLOCK)
                kprev, vprev = kc_ref[0, ps, cs], vc_ref[0, ps, cs]
            keys = jnp.concatenate([kprev, kc_ref[0, rs, cs]], axis=0)
            vals = jnp.concatenate([vprev, vc_ref[0, rs, cs]], axis=0)
            out_pair = jnp.zeros((ATTN_BLOCK, LANES), jnp.float32)
            lse_pair = jnp.zeros((ATTN_BLOCK, LANES), jnp.float32)
            for hh in range(2):
                in_head = (lane >= hh * HEAD_DIM) & (lane < (hh + 1) * HEAD_DIM)
                s = _dot_nt(jnp.where(in_head, qp, jnp.zeros_like(qp)), keys)
                s = jnp.where(mask, s, NEG)
                m = jnp.max(s, axis=-1, keepdims=True)
                e = jnp.exp(s - m)
                den = jnp.sum(e, axis=-1, keepdims=True)
                o = _dot(e.astype(jnp.bfloat16), vals) / den
                out_pair = jnp.where(in_head, o, out_pair)
                lse_pair = jnp.where(in_head, m + jnp.log(den), lse_pair)
            o_ref[0, rs, cs] = out_pair
            l_ref[0, rs, cs] = lse_pair


def _attention(q, k, v, dilation):
    batch, L, _ = q.shape
    tq = min(TQ_ATTN, L)
    res = min(dilation, TQ_ATTN // tq)
    cur = pl.BlockSpec((1, tq, res * ATTN_WIDTH), lambda b, r, i: (b, i, r))
    prev = pl.BlockSpec((1, ATTN_BLOCK, res * ATTN_WIDTH),
                        lambda b, r, i: (b, jnp.maximum(i * (tq // ATTN_BLOCK) - 1, 0), r))
    out = jax.ShapeDtypeStruct((batch, L, dilation * ATTN_WIDTH), jnp.float32)
    return pl.pallas_call(
        _attn_kernel,
        grid=(batch, dilation // res, L // tq),
        in_specs=[cur, prev, cur, prev, cur],
        out_specs=[cur, cur],
        out_shape=[out, out],
        compiler_params=pltpu.CompilerParams(dimension_semantics=("arbitrary",) * 3, vmem_limit_bytes=VMEM_LIMIT),
        name=f"attention_d{dilation}",
    )(q, k, k, v, v)


def _load_view(v_ref, stage_ref, dilation):
    if dilation == 1:
        return v_ref[0]
    rows = v_ref.shape[1]
    chunks = ATTN_WIDTH // LANES
    for r in range(dilation):
        for c in range(chunks):
            stage_ref[c, pl.ds(r, rows, stride=dilation), :] = v_ref[0, :, pl.ds(r * ATTN_WIDTH + c * LANES, LANES)]
    return jnp.concatenate([stage_ref[c] for c in range(chunks)], axis=1)


def _store_token_tiles(val, o_ref):
    chunks = val.shape[1] // LANES
    for c in range(chunks):
        o_ref[pl.ds(c, val.shape[0], stride=chunks), :] = val[:, c * LANES:(c + 1) * LANES]


def _out_proj_kernel(o1_ref, o2_ref, o3_ref, l1_ref, l2_ref, l3_ref, gated_ref, x_ref, wout_ref, gattn_ref, x2_ref,
                     so2_ref, so3_ref, sl2_ref, sl3_ref):
    dil = [d for _, d in PATTERNS]
    l1 = _load_view(l1_ref, None, dil[0])
    l2 = _load_view(l2_ref, sl2_ref, dil[1])
    l3 = _load_view(l3_ref, sl3_ref, dil[2])
    o1 = _load_view(o1_ref, None, dil[0])
    o2 = _load_view(o2_ref, so2_ref, dil[1])
    o3 = _load_view(o3_ref, so3_ref, dil[2])
    m = jnp.maximum(jnp.maximum(l1, l2), l3)
    e1, e2, e3 = jnp.exp(l1 - m), jnp.exp(l2 - m), jnp.exp(l3 - m)
    attn = (e1 * o1 + e2 * o2 + e3 * o3) / (e1 + e2 + e3)
    ms = jnp.mean(attn * attn, axis=-1, keepdims=True)
    attn_n = (attn * lax.rsqrt(ms + RMS_EPS) * gattn_ref[...]).astype(jnp.bfloat16)
    y = _dot(attn_n, wout_ref[0:ATTN_WIDTH, :]) + _dot(gated_ref[...], wout_ref[ATTN_WIDTH:, :])
    x2_ref[...] = x_ref[...] + y


def _out_proj(outs, lses, gated, x2d, seq, wout, gattn):
    n = x2d.shape[0]
    tm = TM_PROJ
    tok = lambda w: pl.BlockSpec((tm, w), lambda i: (i, 0))
    full = lambda shape: pl.BlockSpec(shape, lambda i: (0,) * len(shape))
    view_specs = [_view_spec(seq, tm, d) for _, d in PATTERNS] * 2
    stage = pltpu.VMEM((ATTN_WIDTH // LANES, tm, LANES), jnp.float32)
    return pl.pallas_call(
        _out_proj_kernel,
        grid=(n // tm,),
        in_specs=view_specs + [tok(GMLP_WIDTH), tok(D_MODEL), full((D_MODEL, D_MODEL)), full((1, ATTN_WIDTH))],
        out_specs=tok(D_MODEL),
        out_shape=jax.ShapeDtypeStruct((n, D_MODEL), jnp.float32),
        scratch_shapes=[stage] * 4,
        compiler_params=pltpu.CompilerParams(dimension_semantics=("arbitrary",), vmem_limit_bytes=VMEM_LIMIT),
        name="out_proj",
    )(*outs, *lses, gated, x2d, wout, gattn)


CAND_J_COUNT = (16, 8, 5, 4, 3, 2, 2, 2)


def _top16_groups(vals, ids):
    vals = list(vals)
    out_v, out_i = [], []
    for _ in range(PEER_TOPK):
        bv, bi = vals[0], ids[0]
        for v, i in zip(vals[1:], ids[1:]):
            gt = v > bv
            bv = jnp.where(gt, v, bv)
            bi = jnp.where(gt, i, bi)
        for sh in (4, 2, 1):
            rv, ri = pltpu.roll(bv, sh, axis=0), pltpu.roll(bi, sh, axis=0)
            better = (rv > bv) | ((rv == bv) & (ri < bi))
            bv = jnp.where(better, rv, bv)
            bi = jnp.where(better, ri, bi)
        out_v.append(bv)
        out_i.append(bi)
        vals = [jnp.where(i == bi, -jnp.inf, v) for v, i in zip(vals, ids)]
    return out_v, out_i


def _stack(reps, start):
    row = lax.broadcasted_iota(jnp.int32, reps[0].shape, 0)
    out = reps[start + SUBLANES - 1]
    for r in range(SUBLANES - 2, -1, -1):
        out = jnp.where(row == r, reps[start + r], out)
    return out


def _select_experts(sa, sb):
    t = sa.shape[1]
    row = lax.broadcasted_iota(jnp.int32, (SUBLANES, t), 0)
    groups = lambda s: [s[g * SUBLANES:(g + 1) * SUBLANES, :] for g in range(N_KEYS // SUBLANES)]
    key_ids = [row + g * SUBLANES for g in range(N_KEYS // SUBLANES)]
    va, ia = _top16_groups(groups(sa), key_ids)
    vb, ib = _top16_groups(groups(sb), key_ids)
    vb_lo, vb_hi, va_hi = _stack(vb, 0), _stack(vb, SUBLANES), _stack(va, SUBLANES)
    cand = [va[0] + vb_lo, va[0] + vb_hi]
    cand_ids = [row, row + SUBLANES]
    for i in range(1, SUBLANES):
        cand.append(jnp.where(row < CAND_J_COUNT[i], va[i] + vb_lo, -jnp.inf))
        cand_ids.append(row + i * PEER_TOPK)
    cand.append(va_hi + vb[0])
    cand_ids.append((row + SUBLANES) * PEER_TOPK)
    top_s, pos = _top16_groups(cand, cand_ids)
    idx_halves, e_halves = [], []
    for h in range(2):
        p = _stack(pos, h * SUBLANES)
        pa, pb = p >> 4, p & (PEER_TOPK - 1)
        ea, eb = jnp.zeros_like(p), jnp.zeros_like(p)
        for i in range(PEER_TOPK):
            ea = jnp.where(pa == i, ia[i], ea)
            eb = jnp.where(pb == i, ib[i], eb)
        idx_halves.append((ea * N_KEYS + eb) * TABLE_ROWS_PER_EXPERT)
        e_halves.append(jnp.exp(_stack(top_s, h * SUBLANES) - top_s[0]))
    den = e_halves[0] + e_halves[1]
    for sh in (4, 2, 1):
        den = den + pltpu.roll(den, sh, axis=0)
    return jnp.concatenate(idx_halves, axis=0), jnp.concatenate([e / den for e in e_halves], axis=0)


def _route_kernel(x2_ref, gffn_ref, wq_ref, ka_ref, kb_ref, xn_ref, idx_ref, gate_ref):
    x = x2_ref[...]
    ms = jnp.mean(x * x, axis=-1, keepdims=True)
    xn = x * lax.rsqrt(ms + RMS_EPS) * gffn_ref[...]
    _store_token_tiles(xn, xn_ref)
    qh = _dot(xn.astype(jnp.bfloat16), wq_ref[...])
    half = D_KEY // 2
    for lt in range(x.shape[0] // LANES):
        ts = slice(lt * LANES, (lt + 1) * LANES)
        idx_rows, gate_rows = [], []
        for h in range(PEER_HEADS):
            qa = qh[ts, h * D_KEY:h * D_KEY + half].astype(jnp.bfloat16)
            qb = qh[ts, h * D_KEY + half:(h + 1) * D_KEY].astype(jnp.bfloat16)
            idx_h, gate_h = _select_experts(_dot_nt(ka_ref[...], qa), _dot_nt(kb_ref[...], qb))
            idx_rows.append(idx_h)
            gate_rows.append(gate_h)
        idx_ref[ts, :] = jnp.concatenate(idx_rows, axis=0).T
        gate_ref[ts, :] = jnp.concatenate(gate_rows, axis=0).T


def _peer_route(x2, gffn, wq, ka, kb):
    n = x2.shape[0]
    tm = TM_ROUTE
    tok = lambda w: pl.BlockSpec((tm, w), lambda i: (i, 0))
    full = lambda shape: pl.BlockSpec(shape, lambda i: (0,) * len(shape))
    return pl.pallas_call(
        _route_kernel,
        grid=(n // tm,),
        in_specs=[tok(D_MODEL), full((1, D_MODEL)), full((D_MODEL, PEER_HEADS * D_KEY)),
                  full((N_KEYS, D_KEY // 2)), full((N_KEYS, D_KEY // 2))],
        out_specs=[pl.BlockSpec((tm * SUBLANES, LANES), lambda i: (i, 0)), tok(NJ), tok(NJ)],
        out_shape=[jax.ShapeDtypeStruct((n * SUBLANES, LANES), jnp.float32),
                   jax.ShapeDtypeStruct((n, NJ), jnp.int32),
                   jax.ShapeDtypeStruct((n, NJ), jnp.float32)],
        compiler_params=pltpu.CompilerParams(dimension_semantics=("arbitrary",), vmem_limit_bytes=VMEM_LIMIT),
        name="peer_route",
    )(x2, gffn, wq, ka, kb)


def _pack_kernel(t_ref, o_ref):
    x = t_ref[...]
    bits = lax.bitcast_convert_type(x.astype(jnp.bfloat16).astype(jnp.float32), jnp.uint32)
    half = D_MODEL // 2
    word = (bits[:, :half] >> 16) | (bits[:, half:] & jnp.uint32(0xFFFF0000))
    for s in range(TABLE_ROWS_PER_EXPERT):
        o_ref[pl.ds(s, x.shape[0], stride=TABLE_ROWS_PER_EXPERT), :] = word[:, s * LANES:(s + 1) * LANES]


def _pack_table(tab):
    e, te = tab.shape[0], TE_PACK
    return pl.pallas_call(
        _pack_kernel,
        grid=(e // te,),
        in_specs=[pl.BlockSpec((te, D_MODEL), lambda i: (i, 0))],
        out_specs=pl.BlockSpec((te * TABLE_ROWS_PER_EXPERT, LANES), lambda i: (i, 0)),
        out_shape=jax.ShapeDtypeStruct((e * TABLE_ROWS_PER_EXPERT, LANES), jnp.uint32),
        compiler_params=pltpu.CompilerParams(dimension_semantics=("arbitrary",), vmem_limit_bytes=VMEM_LIMIT),
        name="pack_table",
    )(tab)


def _unpack(w):
    lo = lax.bitcast_convert_type(w << 16, jnp.float32)
    hi = lax.bitcast_convert_type(w & jnp.uint32(0xFFFF0000), jnp.float32)
    return lo, hi


def _gather_rows(idx_ref, base, tab_ref, w_ref):
    for g in range(NJ // SUBLANES):
        window = idx_ref.at[pl.ds(base + g * SUBLANES, SUBLANES)]
        for r in range(SUBLANES):
            off = pl.multiple_of(window[r], TABLE_ROWS_PER_EXPERT)
            w_ref[pl.ds((g * SUBLANES + r) * TABLE_ROWS_PER_EXPERT, TABLE_ROWS_PER_EXPERT), :] = (
                tab_ref[pl.ds(off, TABLE_ROWS_PER_EXPERT), :])


def _group_chunk(w_ref, g, s):
    return _unpack(w_ref[pl.ds(g * SUBLANES * TABLE_ROWS_PER_EXPERT + s, SUBLANES, stride=TABLE_ROWS_PER_EXPERT), :])


def _peer_up_kernel(idx_ref, gate_ref, xn_ref, tab_ref, coef_ref, w_ref, act_ref, *, tb):
    def partial_dots(t):
        _gather_rows(idx_ref, t * NJ, tab_ref, w_ref)
        xt = xn_ref[t]
        xb = [jnp.broadcast_to(xt[c:c + 1, :], (SUBLANES, LANES)) for c in range(SUBLANES)]
        parts = []
        for g in range(NJ // SUBLANES):
            acc = None
            for s in range(4):
                lo, hi = _group_chunk(w_ref, g, s)
                term = lo * xb[s] + hi * xb[4 + s]
                acc = term if acc is None else acc + term
            parts.append(acc)
        return jnp.concatenate(parts, axis=0)

    def lane_sums(p, t):
        act_ref[pl.ds(t, 1), :] = jnp.sum(p.T, axis=0, keepdims=True)

    def body(i, p_prev):
        t0 = 2 * i
        lane_sums(p_prev, jnp.maximum(t0 - 1, 0))
        pa = partial_dots(t0)
        lane_sums(pa, t0)
        return partial_dots(t0 + 1)

    p_last = lax.fori_loop(0, tb // 2, body, jnp.zeros((NJ, LANES), jnp.float32))
    lane_sums(p_last, tb - 1)
    coef_ref[...] = gate_ref[...] * _gelu(act_ref[...])


def _peer_down_kernel(idx_ref, coef_ref, x2_ref, tab_ref, o_ref, w_ref, ca_ref, cb_ref, *, tb):
    def spread(t, c_ref):
        c_ref[...] = jnp.broadcast_to(coef_ref[pl.ds(t, 1), :], (NJ, LANES)).T

    def weighted_sum(t, c_ref):
        _gather_rows(idx_ref, t * NJ, tab_ref, w_ref)
        acc_lo, acc_hi = [None] * 4, [None] * 4
        for g in range(NJ // SUBLANES):
            cg = c_ref[pl.ds(g * SUBLANES, SUBLANES), :]
            for s in range(4):
                lo, hi = _group_chunk(w_ref, g, s)
                acc_lo[s] = lo * cg if acc_lo[s] is None else acc_lo[s] + lo * cg
                acc_hi[s] = hi * cg if acc_hi[s] is None else acc_hi[s] + hi * cg
        xt = x2_ref[t]
        rows = [xt[c:c + 1, :] + jnp.sum(a, axis=0, keepdims=True) for c, a in enumerate(acc_lo + acc_hi)]
        o_ref[pl.ds(t, 1), :] = jnp.concatenate(rows, axis=1)

    spread(0, ca_ref)

    def body(i, carry):
        t0 = 2 * i
        spread(t0 + 1, cb_ref)
        weighted_sum(t0, ca_ref)
        spread(jnp.minimum(t0 + 2, tb - 1), ca_ref)
        weighted_sum(t0 + 1, cb_ref)
        return carry

    lax.fori_loop(0, tb // 2, body, 0)


def _peer_specs(n, table_rows, tb):
    idx_spec = pl.BlockSpec((tb * NJ,), lambda i: (i,), memory_space=pltpu.SMEM)
    row_spec = pl.BlockSpec((tb, NJ), lambda i: (i, 0))
    tok_spec = pl.BlockSpec((tb, SUBLANES, LANES), lambda i: (i, 0, 0))
    tab_spec = pl.BlockSpec((table_rows, LANES), lambda i: (0, 0), pipeline_mode=pl.Buffered(1))
    return idx_spec, row_spec, tok_spec, tab_spec


_W_SCRATCH = pltpu.VMEM((NJ * TABLE_ROWS_PER_EXPERT, LANES), jnp.uint32)
_SQUARE_SCRATCH = pltpu.VMEM((NJ, LANES), jnp.float32)


def _peer_up(idx, gate, xn, tab):
    n, tb = gate.shape[0], TB_PEER
    idx_spec, row_spec, tok_spec, tab_spec = _peer_specs(n, tab.shape[0], tb)
    return pl.pallas_call(
        functools.partial(_peer_up_kernel, tb=tb),
        grid=(n // tb,),
        in_specs=[idx_spec, row_spec, tok_spec, tab_spec],
        out_specs=row_spec,
        out_shape=jax.ShapeDtypeStruct((n, NJ), jnp.float32),
        scratch_shapes=[_W_SCRATCH, pltpu.VMEM((tb, NJ), jnp.float32)],
        compiler_params=pltpu.CompilerParams(dimension_semantics=("arbitrary",), vmem_limit_bytes=VMEM_LIMIT),
        name="peer_up",
    )(idx.reshape(-1), gate, xn.reshape(n, SUBLANES, LANES), tab)


def _peer_down(idx, coef, x2, tab):
    n, tb = coef.shape[0], TB_PEER
    idx_spec, row_spec, tok_spec, tab_spec = _peer_specs(n, tab.shape[0], tb)
    return pl.pallas_call(
        functools.partial(_peer_down_kernel, tb=tb),
        grid=(n // tb,),
        in_specs=[idx_spec, row_spec, tok_spec, tab_spec],
        out_specs=pl.BlockSpec((tb, D_MODEL), lambda i: (i, 0)),
        out_shape=jax.ShapeDtypeStruct((n, D_MODEL), jnp.float32),
        scratch_shapes=[_W_SCRATCH, _SQUARE_SCRATCH, _SQUARE_SCRATCH],
        compiler_params=pltpu.CompilerParams(dimension_semantics=("arbitrary",), vmem_limit_bytes=VMEM_LIMIT),
        name="peer_down",
    )(idx.reshape(-1), coef, x2.reshape(n, SUBLANES, LANES), tab)


def _layer(x2d, batch, seq, mix_norm_g, w_in, q_norm_g, k_norm_g, v_gate_norm_g, w_spatial, b_spatial, attn_out_g,
           gate_out_g, w_out, ffn_norm_g, w_query, sub_keys_a, sub_keys_b, expert_u, expert_v):
    bf16 = jnp.bfloat16
    heads = ATTN_WIDTH // HEAD_DIM
    lane = jnp.arange(ATTN_WIDTH)
    g64 = (lane[:, None] // HEAD_DIM == lane[None, :] // HEAD_DIM).astype(bf16)
    gq = (jnp.tile(q_norm_g, heads) * HEAD_DIM ** -0.5)[None, :]
    gk = jnp.tile(k_norm_g, heads)[None, :]
    bsp = jnp.repeat(b_spatial.T, GMLP_WIDTH // GMLP_GROUPS, axis=1)
    *qkv, gated = _in_proj(x2d, batch, seq, mix_norm_g[None, :], w_in.astype(bf16), gq, gk, g64,
                           v_gate_norm_g.reshape(1, GMLP_WIDTH), w_spatial, bsp, gate_out_g[None, :])
    n_pat = len(PATTERNS)
    outs, lses = [], []
    for p, (window, dilation) in enumerate(PATTERNS):
        assert window // dilation == N_BACK
        o, l = _attention(qkv[p], qkv[n_pat + p], qkv[2 * n_pat + p], dilation)
        outs.append(o)
        lses.append(l)
    x2 = _out_proj(outs, lses, gated, x2d, seq, w_out.astype(bf16), attn_out_g[None, :])
    xn, idx, gate = _peer_route(x2, ffn_norm_g[None, :], w_query.astype(bf16), sub_keys_a.astype(bf16),
                                sub_keys_b.astype(bf16))
    coef = _peer_up(idx, gate, xn, _pack_table(expert_u))
    return _peer_down(idx, coef, x2, _pack_table(expert_v))


def kernel(x, mix_norm_g, w_in, q_norm_g, k_norm_g, v_gate_norm_g, w_spatial, b_spatial, attn_out_g, gate_out_g,
           w_out, ffn_norm_g, w_query, sub_keys_a, sub_keys_b, expert_u, expert_v):
    batch, seq, d = x.shape
    assert d == D_MODEL and seq % TM_PROJ == 0 and (batch * seq) % TB_PEER == 0 and TB_PEER % 2 == 0
    for _, dilation in PATTERNS:
        sub_len = seq // dilation
        assert seq % dilation == 0 and sub_len % ATTN_BLOCK == 0 and sub_len % min(TQ_ATTN, sub_len) == 0
    x2d = x.reshape(batch * seq, d)
    for l in range(mix_norm_g.shape[0]):
        x2d = _layer(x2d, batch, seq, mix_norm_g[l], w_in[l], q_norm_g[l], k_norm_g[l], v_gate_norm_g[l],
                     w_spatial[l], b_spatial[l], attn_out_g[l], gate_out_g[l], w_out[l], ffn_norm_g[l],
                     w_query[l], sub_keys_a[l], sub_keys_b[l], expert_u[l], expert_v[l])
    return x2d.reshape(batch, seq, d)
```

```python
import functools

import jax
import jax.numpy as jnp
from jax import lax
from jax.experimental import pallas as pl
from jax.experimental.pallas import tpu as pltpu

D_MODEL = 1024
ATTN_WIDTH = 512
HEAD_DIM = 64
HEAD_PAIRS = ATTN_WIDTH // 128
PATTERNS = ((128, 1), (512, 4), (2048, 16))
N_BACK = 128
ATTN_BLOCK = 128
GMLP_WIDTH = 512
GMLP_CHUNK = 128
GMLP_GROUPS = 4
IN_WIDTH = 3 * ATTN_WIDTH + 2 * GMLP_WIDTH
PEER_HEADS = 8
PEER_TOPK = 16
N_KEYS = 128
D_KEY = 256
NJ = PEER_HEADS * PEER_TOPK
TABLE_ROWS_PER_EXPERT = 4
RMS_EPS = 1e-6
NEG = -1e30

LANES = 128
SUBLANES = 8
VMEM_LIMIT = 56 * 2 ** 20

TM_PROJ = 512
TQ_ATTN = 512
TM_ROUTE = 256
TB_PEER = 256
TE_PACK = 512


def _gelu(x):
    return 0.5 * x * (1.0 + lax.erf(x * (2.0 ** -0.5)))


def _split_bf16(x):
    hi = x.astype(jnp.bfloat16)
    lo = (x - hi.astype(jnp.float32)).astype(jnp.bfloat16)
    return hi, lo


def _dot(a, b):
    return jnp.dot(a, b, preferred_element_type=jnp.float32)


def _dot_nt(a, b):
    return lax.dot_general(a, b, (((1,), (1,)), ((), ())), preferred_element_type=jnp.float32)


def _store_views(val, stage_ref, out_refs):
    chunks = ATTN_WIDTH // LANES
    for c in range(chunks):
        stage_ref[c] = val[:, c * LANES:(c + 1) * LANES]
    for (_, dilation), o_ref in zip(PATTERNS, out_refs):
        rows = val.shape[0] // dilation
        for r in range(dilation):
            for c in range(chunks):
                lanes = pl.ds(r * ATTN_WIDTH + c * LANES, LANES)
                o_ref[0, :, lanes] = stage_ref[c, pl.ds(r, rows, stride=dilation), :].astype(jnp.bfloat16)


def _in_proj_kernel(x_ref, gmix_ref, win_ref, gq_ref, gk_ref, g64_ref, gvg_ref, wsp_ref, bsp_ref, gout_ref,
                    q1_ref, q2_ref, q3_ref, k1_ref, k2_ref, k3_ref, v1_ref, v2_ref, v3_ref, gated_ref,
                    gs_ref, stage_ref):
    x = x_ref[...]
    ms = jnp.mean(x * x, axis=-1, keepdims=True)
    hn = (x * lax.rsqrt(ms + RMS_EPS) * gmix_ref[...]).astype(jnp.bfloat16)

    def head_norm(t, g):
        hi, lo = _split_bf16(t * t)
        msq = (_dot(hi, g64_ref[...]) + _dot(lo, g64_ref[...])) * (1.0 / HEAD_DIM)
        return t * lax.rsqrt(msq + RMS_EPS) * g

    q = _dot(hn, win_ref[:, 0:ATTN_WIDTH])
    _store_views(head_norm(q, gq_ref[...]), stage_ref, (q1_ref, q2_ref, q3_ref))
    k = _dot(hn, win_ref[:, ATTN_WIDTH:2 * ATTN_WIDTH])
    _store_views(head_norm(k, gk_ref[...]), stage_ref, (k1_ref, k2_ref, k3_ref))
    _store_views(_dot(hn, win_ref[:, 2 * ATTN_WIDTH:3 * ATTN_WIDTH]), stage_ref, (v1_ref, v2_ref, v3_ref))

    u = _gelu(_dot(hn, win_ref[:, 3 * ATTN_WIDTH:3 * ATTN_WIDTH + GMLP_WIDTH]))
    gv = _gelu(_dot(hn, win_ref[:, 3 * ATTN_WIDTH + GMLP_WIDTH:IN_WIDTH]))
    row = lax.broadcasted_iota(jnp.int32, (GMLP_CHUNK, GMLP_CHUNK), 0)
    col = lax.broadcasted_iota(jnp.int32, (GMLP_CHUNK, GMLP_CHUNK), 1)
    causal = col <= row
    n_chunks = x.shape[0] // GMLP_CHUNK
    for g in range(GMLP_GROUPS):
        cs = slice(g * LANES, (g + 1) * LANES)
        vg = gv[:, cs]
        msg = jnp.mean(vg * vg, axis=-1, keepdims=True)
        vn = (vg * lax.rsqrt(msg + RMS_EPS) * gvg_ref[:, cs]).astype(jnp.bfloat16)
        w = jnp.where(causal, wsp_ref[g], 0.0).astype(jnp.bfloat16)
        for c in range(n_chunks):
            rs = slice(c * GMLP_CHUNK, (c + 1) * GMLP_CHUNK)
            z = _dot(w, vn[rs, :]) + bsp_ref[:, cs]
            gs_ref[rs, cs] = u[rs, cs] * z
    gated = gs_ref[...]
    msg = jnp.mean(gated * gated, axis=-1, keepdims=True)
    gated_ref[...] = (gated * lax.rsqrt(msg + RMS_EPS) * gout_ref[...]).astype(jnp.bfloat16)


def _view_spec(seq, tm, dilation):
    blocks_per_row = seq // tm
    return pl.BlockSpec((1, tm // dilation, dilation * ATTN_WIDTH),
                        lambda i: (i // blocks_per_row, i % blocks_per_row, 0))


def _view_shape(batch, seq, dilation, dtype):
    return jax.ShapeDtypeStruct((batch, seq // dilation, dilation * ATTN_WIDTH), dtype)


def _in_proj(x2d, batch, seq, gmix, win, gq, gk, g64, gvg, wsp, bsp, gout):
    n = x2d.shape[0]
    tm = TM_PROJ
    full = lambda shape: pl.BlockSpec(shape, lambda i: (0,) * len(shape))
    tok = lambda w: pl.BlockSpec((tm, w), lambda i: (i, 0))
    view_specs = [_view_spec(seq, tm, d) for _, d in PATTERNS] * 3
    view_shapes = [_view_shape(batch, seq, d, jnp.bfloat16) for _, d in PATTERNS] * 3
    return pl.pallas_call(
        _in_proj_kernel,
        grid=(n // tm,),
        in_specs=[tok(D_MODEL), full((1, D_MODEL)), full((D_MODEL, IN_WIDTH)), full((1, ATTN_WIDTH)),
                  full((1, ATTN_WIDTH)), full((ATTN_WIDTH, ATTN_WIDTH)), full((1, GMLP_WIDTH)),
                  full((GMLP_GROUPS, GMLP_CHUNK, GMLP_CHUNK)), full((GMLP_CHUNK, GMLP_WIDTH)), full((1, GMLP_WIDTH))],
        out_specs=view_specs + [tok(GMLP_WIDTH)],
        out_shape=view_shapes + [jax.ShapeDtypeStruct((n, GMLP_WIDTH), jnp.bfloat16)],
        scratch_shapes=[pltpu.VMEM((tm, GMLP_WIDTH), jnp.float32),
                        pltpu.VMEM((ATTN_WIDTH // LANES, tm, LANES), jnp.float32)],
        compiler_params=pltpu.CompilerParams(dimension_semantics=("arbitrary",), vmem_limit_bytes=VMEM_LIMIT),
        name="in_proj",
    )(x2d, gmix, win, gq, gk, g64, gvg, wsp, bsp, gout)


def _attn_kernel(q_ref, kp_ref, kc_ref, vp_ref, vc_ref, o_ref, l_ref):
    tq = q_ref.shape[1]
    have_prev = pl.program_id(2) > 0
    qi = lax.broadcasted_iota(jnp.int32, (ATTN_BLOCK, 2 * ATTN_BLOCK), 0)
    kj = lax.broadcasted_iota(jnp.int32, (ATTN_BLOCK, 2 * ATTN_BLOCK), 1)
    rel = qi + ATTN_BLOCK - kj
    band = (rel >= 0) & (rel <= N_BACK)
    lane = lax.broadcasted_iota(jnp.int32, (ATTN_BLOCK, LANES), 1)
    n_lane_tiles = q_ref.shape[2] // LANES
    for qb in range(tq // ATTN_BLOCK):
        rs = slice(qb * ATTN_BLOCK, (qb + 1) * ATTN_BLOCK)
        if qb == 0:
            mask = band & ((kj >= ATTN_BLOCK) | have_prev)
        else:
            mask = band
        for p in range(n_lane_tiles):
            cs = slice(p * LANES, (p + 1) * LANES)
            qp = q_ref[0, rs, cs]
            if qb == 0:
                kprev, vprev = kp_ref[0, :, cs], vp_ref[0, :, cs]
            else:
                ps = slice((qb - 1) * ATTN_BLOCK, qb * ATTN_BLOCK)
                kprev, vprev = kc_ref[0, ps, cs], vc_ref[0, ps, cs]
            keys = jnp.concatenate([kprev, kc_ref[0, rs, cs]], axis=0)
            vals = jnp.concatenate([vprev, vc_ref[0, rs, cs]], axis=0)
            out_pair = jnp.zeros((ATTN_BLOCK, LANES), jnp.float32)
            lse_pair = jnp.zeros((ATTN_BLOCK, LANES), jnp.float32)
            for hh in range(2):
                in_head = (lane >= hh * HEAD_DIM) & (lane < (hh + 1) * HEAD_DIM)
                s = _dot_nt(jnp.where(in_head, qp, jnp.zeros_like(qp)), keys)
                s = jnp.where(mask, s, NEG)
                m = jnp.max(s, axis=-1, keepdims=True)
                e = jnp.exp(s - m)
                den = jnp.sum(e, axis=-1, keepdims=True)
                o = _dot(e.astype(jnp.bfloat16), vals) / den
                out_pair = jnp.where(in_head, o, out_pair)
                lse_pair = jnp.where(in_head, m + jnp.log(den), lse_pair)
            o_ref[0, rs, cs] = out_pair
            l_ref[0, rs, cs] = lse_pair


def _attention(q, k, v, dilation):
    batch, L, _ = q.shape
    tq = min(TQ_ATTN, L)
    res = min(dilation, TQ_ATTN // tq)
    cur = pl.BlockSpec((1, tq, res * ATTN_WIDTH), lambda b, r, i: (b, i, r))
    prev = pl.BlockSpec((1, ATTN_BLOCK, res * ATTN_WIDTH),
                        lambda b, r, i: (b, jnp.maximum(i * (tq // ATTN_BLOCK) - 1, 0), r))
    out = jax.ShapeDtypeStruct((batch, L, dilation * ATTN_WIDTH), jnp.float32)
    return pl.pallas_call(
        _attn_kernel,
        grid=(batch, dilation // res, L // tq),
        in_specs=[cur, prev, cur, prev, cur],
        out_specs=[cur, cur],
        out_shape=[out, out],
        compiler_params=pltpu.CompilerParams(dimension_semantics=("arbitrary",) * 3, vmem_limit_bytes=VMEM_LIMIT),
        name=f"attention_d{dilation}",
    )(q, k, k, v, v)


def _load_view(v_ref, stage_ref, dilation):
    if dilation == 1:
        return v_ref[0]
    rows = v_ref.shape[1]
    chunks = ATTN_WIDTH // LANES
    for r in range(dilation):
        for c in range(chunks):
            stage_ref[c, pl.ds(r, rows, stride=dilation), :] = v_ref[0, :, pl.ds(r * ATTN_WIDTH + c * LANES, LANES)]
    return jnp.concatenate([stage_ref[c] for c in range(chunks)], axis=1)


def _store_token_tiles(val, o_ref):
    chunks = val.shape[1] // LANES
    for c in range(chunks):
        o_ref[pl.ds(c, val.shape[0], stride=chunks), :] = val[:, c * LANES:(c + 1) * LANES]


def _out_proj_kernel(o1_ref, o2_ref, o3_ref, l1_ref, l2_ref, l3_ref, gated_ref, x_ref, wout_ref, gattn_ref, x2_ref,
                     so2_ref, so3_ref, sl2_ref, sl3_ref):
    dil = [d for _, d in PATTERNS]
    l1 = _load_view(l1_ref, None, dil[0])
    l2 = _load_view(l2_ref, sl2_ref, dil[1])
    l3 = _load_view(l3_ref, sl3_ref, dil[2])
    o1 = _load_view(o1_ref, None, dil[0])
    o2 = _load_view(o2_ref, so2_ref, dil[1])
    o3 = _load_view(o3_ref, so3_ref, dil[2])
    m = jnp.maximum(jnp.maximum(l1, l2), l3)
    e1, e2, e3 = jnp.exp(l1 - m), jnp.exp(l2 - m), jnp.exp(l3 - m)
    attn = (e1 * o1 + e2 * o2 + e3 * o3) / (e1 + e2 + e3)
    ms = jnp.mean(attn * attn, axis=-1, keepdims=True)
    attn_n = (attn * lax.rsqrt(ms + RMS_EPS) * gattn_ref[...]).astype(jnp.bfloat16)
    y = _dot(attn_n, wout_ref[0:ATTN_WIDTH, :]) + _dot(gated_ref[...], wout_ref[ATTN_WIDTH:, :])
    x2_ref[...] = x_ref[...] + y


def _out_proj(outs, lses, gated, x2d, seq, wout, gattn):
    n = x2d.shape[0]
    tm = TM_PROJ
    tok = lambda w: pl.BlockSpec((tm, w), lambda i: (i, 0))
    full = lambda shape: pl.BlockSpec(shape, lambda i: (0,) * len(shape))
    view_specs = [_view_spec(seq, tm, d) for _, d in PATTERNS] * 2
    stage = pltpu.VMEM((ATTN_WIDTH // LANES, tm, LANES), jnp.float32)
    return pl.pallas_call(
        _out_proj_kernel,
        grid=(n // tm,),
        in_specs=view_specs + [tok(GMLP_WIDTH), tok(D_MODEL), full((D_MODEL, D_MODEL)), full((1, ATTN_WIDTH))],
        out_specs=tok(D_MODEL),
        out_shape=jax.ShapeDtypeStruct((n, D_MODEL), jnp.float32),
        scratch_shapes=[stage] * 4,
        compiler_params=pltpu.CompilerParams(dimension_semantics=("arbitrary",), vmem_limit_bytes=VMEM_LIMIT),
        name="out_proj",
    )(*outs, *lses, gated, x2d, wout, gattn)


CAND_J_COUNT = (16, 8, 5, 4, 3, 2, 2, 2)


def _top16_groups(vals, ids):
    vals = list(vals)
    out_v, out_i = [], []
    for _ in range(PEER_TOPK):
        bv, bi = vals[0], ids[0]
        for v, i in zip(vals[1:], ids[1:]):
            gt = v > bv
            bv = jnp.where(gt, v, bv)
            bi = jnp.where(gt, i, bi)
        for sh in (4, 2, 1):
            rv, ri = pltpu.roll(bv, sh, axis=0), pltpu.roll(bi, sh, axis=0)
            better = (rv > bv) | ((rv == bv) & (ri < bi))
            bv = jnp.where(better, rv, bv)
            bi = jnp.where(better, ri, bi)
        out_v.append(bv)
        out_i.append(bi)
        vals = [jnp.where(i == bi, -jnp.inf, v) for v, i in zip(vals, ids)]
    return out_v, out_i


def _stack(reps, start):
    row = lax.broadcasted_iota(jnp.int32, reps[0].shape, 0)
    out = reps[start + SUBLANES - 1]
    for r in range(SUBLANES - 2, -1, -1):
        out = jnp.where(row == r, reps[start + r], out)
    return out


def _select_experts(sa, sb):
    t = sa.shape[1]
    row = lax.broadcasted_iota(jnp.int32, (SUBLANES, t), 0)
    groups = lambda s: [s[g * SUBLANES:(g + 1) * SUBLANES, :] for g in range(N_KEYS // SUBLANES)]
    key_ids = [row + g * SUBLANES for g in range(N_KEYS // SUBLANES)]
    va, ia = _top16_groups(groups(sa), key_ids)
    vb, ib = _top16_groups(groups(sb), key_ids)
    vb_lo, vb_hi, va_hi = _stack(vb, 0), _stack(vb, SUBLANES), _stack(va, SUBLANES)
    cand = [va[0] + vb_lo, va[0] + vb_hi]
    cand_ids = [row, row + SUBLANES]
    for i in range(1, SUBLANES):
        cand.append(jnp.where(row < CAND_J_COUNT[i], va[i] + vb_lo, -jnp.inf))
        cand_ids.append(row + i * PEER_TOPK)
    cand.append(va_hi + vb[0])
    cand_ids.append((row + SUBLANES) * PEER_TOPK)
    top_s, pos = _top16_groups(cand, cand_ids)
    idx_halves, e_halves = [], []
    for h in range(2):
        p = _stack(pos, h * SUBLANES)
        pa, pb = p >> 4, p & (PEER_TOPK - 1)
        ea, eb = jnp.zeros_like(p), jnp.zeros_like(p)
        for i in range(PEER_TOPK):
            ea = jnp.where(pa == i, ia[i], ea)
            eb = jnp.where(pb == i, ib[i], eb)
        idx_halves.append((ea * N_KEYS + eb) * TABLE_ROWS_PER_EXPERT)
        e_halves.append(jnp.exp(_stack(top_s, h * SUBLANES) - top_s[0]))
    den = e_halves[0] + e_halves[1]
    for sh in (4, 2, 1):
        den = den + pltpu.roll(den, sh, axis=0)
    return jnp.concatenate(idx_halves, axis=0), jnp.concatenate([e / den for e in e_halves], axis=0)


def _route_kernel(x2_ref, gffn_ref, wq_ref, ka_ref, kb_ref, xn_ref, idx_ref, gate_ref):
    x = x2_ref[...]
    ms = jnp.mean(x * x, axis=-1, keepdims=True)
    xn = x * lax.rsqrt(ms + RMS_EPS) * gffn_ref[...]
    _store_token_tiles(xn, xn_ref)
    qh = _dot(xn.astype(jnp.bfloat16), wq_ref[...])
    half = D_KEY // 2
    for lt in range(x.shape[0] // LANES):
        ts = slice(lt * LANES, (lt + 1) * LANES)
        idx_rows, gate_rows = [], []
        for h in range(PEER_HEADS):
            qa = qh[ts, h * D_KEY:h * D_KEY + half].astype(jnp.bfloat16)
            qb = qh[ts, h * D_KEY + half:(h + 1) * D_KEY].astype(jnp.bfloat16)
            idx_h, gate_h = _select_experts(_dot_nt(ka_ref[...], qa), _dot_nt(kb_ref[...], qb))
            idx_rows.append(idx_h)
            gate_rows.append(gate_h)
        idx_ref[ts, :] = jnp.concatenate(idx_rows, axis=0).T
        gate_ref[ts, :] = jnp.concatenate(gate_rows, axis=0).T


def _peer_route(x2, gffn, wq, ka, kb):
    n = x2.shape[0]
    tm = TM_ROUTE
    tok = lambda w: pl.BlockSpec((tm, w), lambda i: (i, 0))
    full = lambda shape: pl.BlockSpec(shape, lambda i: (0,) * len(shape))
    return pl.pallas_call(
        _route_kernel,
        grid=(n // tm,),
        in_specs=[tok(D_MODEL), full((1, D_MODEL)), full((D_MODEL, PEER_HEADS * D_KEY)),
                  full((N_KEYS, D_KEY // 2)), full((N_KEYS, D_KEY // 2))],
        out_specs=[pl.BlockSpec((tm * SUBLANES, LANES), lambda i: (i, 0)), tok(NJ), tok(NJ)],
        out_shape=[jax.ShapeDtypeStruct((n * SUBLANES, LANES), jnp.float32),
                   jax.ShapeDtypeStruct((n, NJ), jnp.int32),
                   jax.ShapeDtypeStruct((n, NJ), jnp.float32)],
        compiler_params=pltpu.CompilerParams(dimension_semantics=("arbitrary",), vmem_limit_bytes=VMEM_LIMIT),
        name="peer_route",
    )(x2, gffn, wq, ka, kb)


def _pack_kernel(t_ref, o_ref):
    x = t_ref[...]
    bits = lax.bitcast_convert_type(x.astype(jnp.bfloat16).astype(jnp.float32), jnp.uint32)
    half = D_MODEL // 2
    word = (bits[:, :half] >> 16) | (bits[:, half:] & jnp.uint32(0xFFFF0000))
    for s in range(TABLE_ROWS_PER_EXPERT):
        o_ref[pl.ds(s, x.shape[0], stride=TABLE_ROWS_PER_EXPERT), :] = word[:, s * LANES:(s + 1) * LANES]


def _pack_table(tab):
    e, te = tab.shape[0], TE_PACK
    return pl.pallas_call(
        _pack_kernel,
        grid=(e // te,),
        in_specs=[pl.BlockSpec((te, D_MODEL), lambda i: (i, 0))],
        out_specs=pl.BlockSpec((te * TABLE_ROWS_PER_EXPERT, LANES), lambda i: (i, 0)),
        out_shape=jax.ShapeDtypeStruct((e * TABLE_ROWS_PER_EXPERT, LANES), jnp.uint32),
        compiler_params=pltpu.CompilerParams(dimension_semantics=("arbitrary",), vmem_limit_bytes=VMEM_LIMIT),
        name="pack_table",
    )(tab)


def _unpack(w):
    lo = lax.bitcast_convert_type(w << 16, jnp.float32)
    hi = lax.bitcast_convert_type(w & jnp.uint32(0xFFFF0000), jnp.float32)
    return lo, hi


def _gather_rows(idx_ref, base, tab_ref, w_ref):
    for g in range(NJ // SUBLANES):
        window = idx_ref.at[pl.ds(base + g * SUBLANES, SUBLANES)]
        for r in range(SUBLANES):
            off = pl.multiple_of(window[r], TABLE_ROWS_PER_EXPERT)
            w_ref[pl.ds((g * SUBLANES + r) * TABLE_ROWS_PER_EXPERT, TABLE_ROWS_PER_EXPERT), :] = (
                tab_ref[pl.ds(off, TABLE_ROWS_PER_EXPERT), :])


def _group_chunk(w_ref, g, s):
    return _unpack(w_ref[pl.ds(g * SUBLANES * TABLE_ROWS_PER_EXPERT + s, SUBLANES, stride=TABLE_ROWS_PER_EXPERT), :])


def _peer_up_kernel(idx_ref, gate_ref, xn_ref, tab_ref, coef_ref, w_ref, act_ref, *, tb):
    def partial_dots(t):
        _gather_rows(idx_ref, t * NJ, tab_ref, w_ref)
        xt = xn_ref[t]
        xb = [jnp.broadcast_to(xt[c:c + 1, :], (SUBLANES, LANES)) for c in range(SUBLANES)]
        parts = []
        for g in range(NJ // SUBLANES):
            acc = None
            for s in range(4):
                lo, hi = _group_chunk(w_ref, g, s)
                term = lo * xb[s] + hi * xb[4 + s]
                acc = term if acc is None else acc + term
            parts.append(acc)
        return jnp.concatenate(parts, axis=0)

    eye = lax.broadcasted_iota(jnp.int32, (NJ, LANES), 0) == lax.broadcasted_iota(jnp.int32, (NJ, LANES), 1)
    ones = jnp.ones((LANES, LANES), jnp.bfloat16)

    def lane_sums(p, t):
        hi = p.astype(jnp.bfloat16)
        mid = (p - hi.astype(jnp.float32)).astype(jnp.bfloat16)
        s = _dot(hi, ones) + _dot(mid, ones)
        act_ref[pl.ds(t, 1), :] = jnp.sum(jnp.where(eye, s, 0.0), axis=0, keepdims=True)

    def body(i, p_prev):
        t0 = 2 * i
        lane_sums(p_prev, jnp.maximum(t0 - 1, 0))
        pa = partial_dots(t0)
        lane_sums(pa, t0)
        return partial_dots(t0 + 1)

    p_last = lax.fori_loop(0, tb // 2, body, jnp.zeros((NJ, LANES), jnp.float32))
    lane_sums(p_last, tb - 1)
    coef_ref[...] = gate_ref[...] * _gelu(act_ref[...])


def _peer_down_kernel(idx_ref, coef_ref, x2_ref, tab_ref, o_ref, w_ref, ca_ref, cb_ref, *, tb):
    def spread(t, c_ref):
        c_ref[...] = jnp.broadcast_to(coef_ref[pl.ds(t, 1), :], (NJ, LANES)).T

    def weighted_sum(t, c_ref):
        _gather_rows(idx_ref, t * NJ, tab_ref, w_ref)
        acc_lo, acc_hi = [None] * 4, [None] * 4
        for g in range(NJ // SUBLANES):
            cg = c_ref[pl.ds(g * SUBLANES, SUBLANES), :]
            for s in range(4):
                lo, hi = _group_chunk(w_ref, g, s)
                acc_lo[s] = lo * cg if acc_lo[s] is None else acc_lo[s] + lo * cg
                acc_hi[s] = hi * cg if acc_hi[s] is None else acc_hi[s] + hi * cg
        xt = x2_ref[t]
        rows = [xt[c:c + 1, :] + jnp.sum(a, axis=0, keepdims=True) for c, a in enumerate(acc_lo + acc_hi)]
        o_ref[pl.ds(t, 1), :] = jnp.concatenate(rows, axis=1)

    spread(0, ca_ref)

    def body(i, carry):
        t0 = 2 * i
        spread(t0 + 1, cb_ref)
        weighted_sum(t0, ca_ref)
        spread(jnp.minimum(t0 + 2, tb - 1), ca_ref)
        weighted_sum(t0 + 1, cb_ref)
        return carry

    lax.fori_loop(0, tb // 2, body, 0)


def _peer_specs(n, table_rows, tb):
    idx_spec = pl.BlockSpec((tb * NJ,), lambda i: (i,), memory_space=pltpu.SMEM)
    row_spec = pl.BlockSpec((tb, NJ), lambda i: (i, 0))
    tok_spec = pl.BlockSpec((tb, SUBLANES, LANES), lambda i: (i, 0, 0))
    tab_spec = pl.BlockSpec((table_rows, LANES), lambda i: (0, 0), pipeline_mode=pl.Buffered(1))
    return idx_spec, row_spec, tok_spec, tab_spec


_W_SCRATCH = pltpu.VMEM((NJ * TABLE_ROWS_PER_EXPERT, LANES), jnp.uint32)
_SQUARE_SCRATCH = pltpu.VMEM((NJ, LANES), jnp.float32)


def _peer_up(idx, gate, xn, tab):
    n, tb = gate.shape[0], TB_PEER
    idx_spec, row_spec, tok_spec, tab_spec = _peer_specs(n, tab.shape[0], tb)
    return pl.pallas_call(
        functools.partial(_peer_up_kernel, tb=tb),
        grid=(n // tb,),
        in_specs=[idx_spec, row_spec, tok_spec, tab_spec],
        out_specs=row_spec,
        out_shape=jax.ShapeDtypeStruct((n, NJ), jnp.float32),
        scratch_shapes=[_W_SCRATCH, pltpu.VMEM((tb, NJ), jnp.float32)],
        compiler_params=pltpu.CompilerParams(dimension_semantics=("arbitrary",), vmem_limit_bytes=VMEM_LIMIT),
        name="peer_up",
    )(idx.reshape(-1), gate, xn.reshape(n, SUBLANES, LANES), tab)


def _peer_down(idx, coef, x2, tab):
    n, tb = coef.shape[0], TB_PEER
    idx_spec, row_spec, tok_spec, tab_spec = _peer_specs(n, tab.shape[0], tb)
    return pl.pallas_call(
        functools.partial(_peer_down_kernel, tb=tb),
        grid=(n // tb,),
        in_specs=[idx_spec, row_spec, tok_spec, tab_spec],
        out_specs=pl.BlockSpec((tb, D_MODEL), lambda i: (i, 0)),
        out_shape=jax.ShapeDtypeStruct((n, D_MODEL), jnp.float32),
        scratch_shapes=[_W_SCRATCH, _SQUARE_SCRATCH, _SQUARE_SCRATCH],
        compiler_params=pltpu.CompilerParams(dimension_semantics=("arbitrary",), vmem_limit_bytes=VMEM_LIMIT),
        name="peer_down",
    )(idx.reshape(-1), coef, x2.reshape(n, SUBLANES, LANES), tab)


def _layer(x2d, batch, seq, mix_norm_g, w_in, q_norm_g, k_norm_g, v_gate_norm_g, w_spatial, b_spatial, attn_out_g,
           gate_out_g, w_out, ffn_norm_g, w_query, sub_keys_a, sub_keys_b, expert_u, expert_v):
    bf16 = jnp.bfloat16
    heads = ATTN_WIDTH // HEAD_DIM
    lane = jnp.arange(ATTN_WIDTH)
    g64 = (lane[:, None] // HEAD_DIM == lane[None, :] // HEAD_DIM).astype(bf16)
    gq = (jnp.tile(q_norm_g, heads) * HEAD_DIM ** -0.5)[None, :]
    gk = jnp.tile(k_norm_g, heads)[None, :]
    bsp = jnp.repeat(b_spatial.T, GMLP_WIDTH // GMLP_GROUPS, axis=1)
    *qkv, gated = _in_proj(x2d, batch, seq, mix_norm_g[None, :], w_in.astype(bf16), gq, gk, g64,
                           v_gate_norm_g.reshape(1, GMLP_WIDTH), w_spatial, bsp, gate_out_g[None, :])
    n_pat = len(PATTERNS)
    outs, lses = [], []
    for p, (window, dilation) in enumerate(PATTERNS):
        assert window // dilation == N_BACK
        o, l = _attention(qkv[p], qkv[n_pat + p], qkv[2 * n_pat + p], dilation)
        outs.append(o)
        lses.append(l)
    x2 = _out_proj(outs, lses, gated, x2d, seq, w_out.astype(bf16), attn_out_g[None, :])
    xn, idx, gate = _peer_route(x2, ffn_norm_g[None, :], w_query.astype(bf16), sub_keys_a.astype(bf16),
                                sub_keys_b.astype(bf16))
    coef = _peer_up(idx, gate, xn, _pack_table(expert_u))
    return _peer_down(idx, coef, x2, _pack_table(expert_v))


def kernel(x, mix_norm_g, w_in, q_norm_g, k_norm_g, v_gate_norm_g, w_spatial, b_spatial, attn_out_g, gate_out_g,
           w_out, ffn_norm_g, w_query, sub_keys_a, sub_keys_b, expert_u, expert_v):
    batch, seq, d = x.shape
    assert d == D_MODEL and seq % TM_PROJ == 0 and (batch * seq) % TB_PEER == 0 and TB_PEER % 2 == 0
    for _, dilation in PATTERNS:
        sub_len = seq // dilation
        assert seq % dilation == 0 and sub_len % ATTN_BLOCK == 0 and sub_len % min(TQ_ATTN, sub_len) == 0
    x2d = x.reshape(batch * seq, d)
    for l in range(mix_norm_g.shape[0]):
        x2d = _layer(x2d, batch, seq, mix_norm_g[l], w_in[l], q_norm_g[l], k_norm_g[l], v_gate_norm_g[l],
                     w_spatial[l], b_spatial[l], attn_out_g[l], gate_out_g[l], w_out[l], ffn_norm_g[l],
                     w_query[l], sub_keys_a[l], sub_keys_b[l], expert_u[l], expert_v[l])
    return x2d.reshape(batch, seq, d)
```

```python
import functools

import jax
import jax.numpy as jnp
from jax import lax
from jax.experimental import pallas as pl
from jax.experimental.pallas import tpu as pltpu

D_MODEL = 1024
ATTN_WIDTH = 512
HEAD_DIM = 64
HEAD_PAIRS = ATTN_WIDTH // 128
PATTERNS = ((128, 1), (512, 4), (2048, 16))
N_BACK = 128
ATTN_BLOCK = 128
GMLP_WIDTH = 512
GMLP_CHUNK = 128
GMLP_GROUPS = 4
IN_WIDTH = 3 * ATTN_WIDTH + 2 * GMLP_WIDTH
PEER_HEADS = 8
PEER_TOPK = 16
N_KEYS = 128
D_KEY = 256
NJ = PEER_HEADS * PEER_TOPK
TABLE_ROWS_PER_EXPERT = 4
RMS_EPS = 1e-6
NEG = -1e30

LANES = 128
SUBLANES = 8
VMEM_LIMIT = 56 * 2 ** 20

TM_PROJ = 512
TQ_ATTN = 512
TM_ROUTE = 256
TB_PEER = 256
TE_PACK = 512


def _gelu(x):
    return 0.5 * x * (1.0 + lax.erf(x * (2.0 ** -0.5)))


def _split_bf16(x):
    hi = x.astype(jnp.bfloat16)
    lo = (x - hi.astype(jnp.float32)).astype(jnp.bfloat16)
    return hi, lo


def _dot(a, b):
    return jnp.dot(a, b, preferred_element_type=jnp.float32)


def _dot_nt(a, b):
    return lax.dot_general(a, b, (((1,), (1,)), ((), ())), preferred_element_type=jnp.float32)


def _store_views(val, stage_ref, out_refs):
    chunks = ATTN_WIDTH // LANES
    for c in range(chunks):
        stage_ref[c] = val[:, c * LANES:(c + 1) * LANES]
    for (_, dilation), o_ref in zip(PATTERNS, out_refs):
        if dilation == 1:
            o_ref[0] = val.astype(jnp.bfloat16)
            continue
        rows = val.shape[0] // dilation
        for r in range(dilation):
            for c in range(chunks):
                lanes = pl.ds(r * ATTN_WIDTH + c * LANES, LANES)
                o_ref[0, :, lanes] = stage_ref[c, pl.ds(r, rows, stride=dilation), :].astype(jnp.bfloat16)


def _in_proj_kernel(x_ref, gmix_ref, win_ref, gq_ref, gk_ref, g64_ref, gvg_ref, wsp_ref, bsp_ref, gout_ref,
                    q1_ref, q2_ref, q3_ref, k1_ref, k2_ref, k3_ref, v1_ref, v2_ref, v3_ref, gated_ref,
                    gs_ref, stage_ref):
    x = x_ref[...]
    ms = jnp.mean(x * x, axis=-1, keepdims=True)
    hn = (x * lax.rsqrt(ms + RMS_EPS) * gmix_ref[...]).astype(jnp.bfloat16)

    def head_norm(t, g):
        hi, lo = _split_bf16(t * t)
        msq = (_dot(hi, g64_ref[...]) + _dot(lo, g64_ref[...])) * (1.0 / HEAD_DIM)
        return t * lax.rsqrt(msq + RMS_EPS) * g

    q = _dot(hn, win_ref[:, 0:ATTN_WIDTH])
    _store_views(head_norm(q, gq_ref[...]), stage_ref, (q1_ref, q2_ref, q3_ref))
    k = _dot(hn, win_ref[:, ATTN_WIDTH:2 * ATTN_WIDTH])
    _store_views(head_norm(k, gk_ref[...]), stage_ref, (k1_ref, k2_ref, k3_ref))
    _store_views(_dot(hn, win_ref[:, 2 * ATTN_WIDTH:3 * ATTN_WIDTH]), stage_ref, (v1_ref, v2_ref, v3_ref))

    u = _gelu(_dot(hn, win_ref[:, 3 * ATTN_WIDTH:3 * ATTN_WIDTH + GMLP_WIDTH]))
    gv = _gelu(_dot(hn, win_ref[:, 3 * ATTN_WIDTH + GMLP_WIDTH:IN_WIDTH]))
    row = lax.broadcasted_iota(jnp.int32, (GMLP_CHUNK, GMLP_CHUNK), 0)
    col = lax.broadcasted_iota(jnp.int32, (GMLP_CHUNK, GMLP_CHUNK), 1)
    causal = col <= row
    n_chunks = x.shape[0] // GMLP_CHUNK
    for g in range(GMLP_GROUPS):
        cs = slice(g * LANES, (g + 1) * LANES)
        vg = gv[:, cs]
        msg = jnp.mean(vg * vg, axis=-1, keepdims=True)
        vn = (vg * lax.rsqrt(msg + RMS_EPS) * gvg_ref[:, cs]).astype(jnp.bfloat16)
        w = jnp.where(causal, wsp_ref[g], 0.0).astype(jnp.bfloat16)
        for c in range(n_chunks):
            rs = slice(c * GMLP_CHUNK, (c + 1) * GMLP_CHUNK)
            z = _dot(w, vn[rs, :]) + bsp_ref[:, cs]
            gs_ref[rs, cs] = u[rs, cs] * z
    gated = gs_ref[...]
    msg = jnp.mean(gated * gated, axis=-1, keepdims=True)
    gated_ref[...] = (gated * lax.rsqrt(msg + RMS_EPS) * gout_ref[...]).astype(jnp.bfloat16)


def _view_spec(seq, tm, dilation):
    blocks_per_row = seq // tm
    return pl.BlockSpec((1, tm // dilation, dilation * ATTN_WIDTH),
                        lambda i: (i // blocks_per_row, i % blocks_per_row, 0))


def _view_shape(batch, seq, dilation, dtype):
    return jax.ShapeDtypeStruct((batch, seq // dilation, dilation * ATTN_WIDTH), dtype)


def _in_proj(x2d, batch, seq, gmix, win, gq, gk, g64, gvg, wsp, bsp, gout):
    n = x2d.shape[0]
    tm = TM_PROJ
    full = lambda shape: pl.BlockSpec(shape, lambda i: (0,) * len(shape))
    tok = lambda w: pl.BlockSpec((tm, w), lambda i: (i, 0))
    view_specs = [_view_spec(seq, tm, d) for _, d in PATTERNS] * 3
    view_shapes = [_view_shape(batch, seq, d, jnp.bfloat16) for _, d in PATTERNS] * 3
    return pl.pallas_call(
        _in_proj_kernel,
        grid=(n // tm,),
        in_specs=[tok(D_MODEL), full((1, D_MODEL)), full((D_MODEL, IN_WIDTH)), full((1, ATTN_WIDTH)),
                  full((1, ATTN_WIDTH)), full((ATTN_WIDTH, ATTN_WIDTH)), full((1, GMLP_WIDTH)),
                  full((GMLP_GROUPS, GMLP_CHUNK, GMLP_CHUNK)), full((GMLP_CHUNK, GMLP_WIDTH)), full((1, GMLP_WIDTH))],
        out_specs=view_specs + [tok(GMLP_WIDTH)],
        out_shape=view_shapes + [jax.ShapeDtypeStruct((n, GMLP_WIDTH), jnp.bfloat16)],
        scratch_shapes=[pltpu.VMEM((tm, GMLP_WIDTH), jnp.float32),
                        pltpu.VMEM((ATTN_WIDTH // LANES, tm, LANES), jnp.float32)],
        compiler_params=pltpu.CompilerParams(dimension_semantics=("arbitrary",), vmem_limit_bytes=VMEM_LIMIT),
        name="in_proj",
    )(x2d, gmix, win, gq, gk, g64, gvg, wsp, bsp, gout)


def _attn_kernel(q_ref, kp_ref, kc_ref, vp_ref, vc_ref, o_ref, l_ref):
    tq = q_ref.shape[1]
    have_prev = pl.program_id(2) > 0
    qi = lax.broadcasted_iota(jnp.int32, (ATTN_BLOCK, 2 * ATTN_BLOCK), 0)
    kj = lax.broadcasted_iota(jnp.int32, (ATTN_BLOCK, 2 * ATTN_BLOCK), 1)
    rel = qi + ATTN_BLOCK - kj
    band = (rel >= 0) & (rel <= N_BACK)
    lane = lax.broadcasted_iota(jnp.int32, (ATTN_BLOCK, LANES), 1)
    n_lane_tiles = q_ref.shape[2] // LANES
    for qb in range(tq // ATTN_BLOCK):
        rs = slice(qb * ATTN_BLOCK, (qb + 1) * ATTN_BLOCK)
        if qb == 0:
            mask = band & ((kj >= ATTN_BLOCK) | have_prev)
        else:
            mask = band
        for p in range(n_lane_tiles):
            cs = slice(p * LANES, (p + 1) * LANES)
            qp = q_ref[0, rs, cs]
            if qb == 0:
                kprev, vprev = kp_ref[0, :, cs], vp_ref[0, :, cs]
            else:
                ps = slice((qb - 1) * ATTN_BLOCK, qb * ATTN_BLOCK)
                kprev, vprev = kc_ref[0, ps, cs], vc_ref[0, ps, cs]
            keys = jnp.concatenate([kprev, kc_ref[0, rs, cs]], axis=0)
            vals = jnp.concatenate([vprev, vc_ref[0, rs, cs]], axis=0)
            out_pair = jnp.zeros((ATTN_BLOCK, LANES), jnp.float32)
            lse_pair = jnp.zeros((ATTN_BLOCK, LANES), jnp.float32)
            for hh in range(2):
                in_head = (lane >= hh * HEAD_DIM) & (lane < (hh + 1) * HEAD_DIM)
                s = _dot_nt(jnp.where(in_head, qp, jnp.zeros_like(qp)), keys)
                s = jnp.where(mask, s, NEG)
                m = jnp.max(s, axis=-1, keepdims=True)
                e = jnp.exp(s - m)
                den = jnp.sum(e, axis=-1, keepdims=True)
                o = _dot(e.astype(jnp.bfloat16), vals) / den
                out_pair = jnp.where(in_head, o, out_pair)
                lse_pair = jnp.where(in_head, m + jnp.log(den), lse_pair)
            o_ref[0, rs, cs] = out_pair
            l_ref[0, rs, cs] = lse_pair


def _attention(q, k, v, dilation):
    batch, L, _ = q.shape
    tq = min(TQ_ATTN, L)
    res = min(dilation, TQ_ATTN // tq)
    cur = pl.BlockSpec((1, tq, res * ATTN_WIDTH), lambda b, r, i: (b, i, r))
    prev = pl.BlockSpec((1, ATTN_BLOCK, res * ATTN_WIDTH),
                        lambda b, r, i: (b, jnp.maximum(i * (tq // ATTN_BLOCK) - 1, 0), r))
    out = jax.ShapeDtypeStruct((batch, L, dilation * ATTN_WIDTH), jnp.float32)
    return pl.pallas_call(
        _attn_kernel,
        grid=(batch, dilation // res, L // tq),
        in_specs=[cur, prev, cur, prev, cur],
        out_specs=[cur, cur],
        out_shape=[out, out],
        compiler_params=pltpu.CompilerParams(dimension_semantics=("arbitrary",) * 3, vmem_limit_bytes=VMEM_LIMIT),
        name=f"attention_d{dilation}",
    )(q, k, k, v, v)


def _load_view(v_ref, stage_ref, dilation):
    if dilation == 1:
        return v_ref[0]
    rows = v_ref.shape[1]
    chunks = ATTN_WIDTH // LANES
    for r in range(dilation):
        for c in range(chunks):
            stage_ref[c, pl.ds(r, rows, stride=dilation), :] = v_ref[0, :, pl.ds(r * ATTN_WIDTH + c * LANES, LANES)]
    return jnp.concatenate([stage_ref[c] for c in range(chunks)], axis=1)


def _store_token_tiles(val, o_ref):
    chunks = val.shape[1] // LANES
    for c in range(chunks):
        o_ref[pl.ds(c, val.shape[0], stride=chunks), :] = val[:, c * LANES:(c + 1) * LANES]


def _out_proj_kernel(o1_ref, o2_ref, o3_ref, l1_ref, l2_ref, l3_ref, gated_ref, x_ref, wout_ref, gattn_ref, x2_ref,
                     so2_ref, so3_ref, sl2_ref, sl3_ref):
    dil = [d for _, d in PATTERNS]
    l1 = _load_view(l1_ref, None, dil[0])
    l2 = _load_view(l2_ref, sl2_ref, dil[1])
    l3 = _load_view(l3_ref, sl3_ref, dil[2])
    o1 = _load_view(o1_ref, None, dil[0])
    o2 = _load_view(o2_ref, so2_ref, dil[1])
    o3 = _load_view(o3_ref, so3_ref, dil[2])
    m = jnp.maximum(jnp.maximum(l1, l2), l3)
    e1, e2, e3 = jnp.exp(l1 - m), jnp.exp(l2 - m), jnp.exp(l3 - m)
    attn = (e1 * o1 + e2 * o2 + e3 * o3) / (e1 + e2 + e3)
    ms = jnp.mean(attn * attn, axis=-1, keepdims=True)
    attn_n = (attn * lax.rsqrt(ms + RMS_EPS) * gattn_ref[...]).astype(jnp.bfloat16)
    y = _dot(attn_n, wout_ref[0:ATTN_WIDTH, :]) + _dot(gated_ref[...], wout_ref[ATTN_WIDTH:, :])
    x2_ref[...] = x_ref[...] + y


def _out_proj(outs, lses, gated, x2d, seq, wout, gattn):
    n = x2d.shape[0]
    tm = TM_PROJ
    tok = lambda w: pl.BlockSpec((tm, w), lambda i: (i, 0))
    full = lambda shape: pl.BlockSpec(shape, lambda i: (0,) * len(shape))
    view_specs = [_view_spec(seq, tm, d) for _, d in PATTERNS] * 2
    stage = pltpu.VMEM((ATTN_WIDTH // LANES, tm, LANES), jnp.float32)
    return pl.pallas_call(
        _out_proj_kernel,
        grid=(n // tm,),
        in_specs=view_specs + [tok(GMLP_WIDTH), tok(D_MODEL), full((D_MODEL, D_MODEL)), full((1, ATTN_WIDTH))],
        out_specs=tok(D_MODEL),
        out_shape=jax.ShapeDtypeStruct((n, D_MODEL), jnp.float32),
        scratch_shapes=[stage] * 4,
        compiler_params=pltpu.CompilerParams(dimension_semantics=("arbitrary",), vmem_limit_bytes=VMEM_LIMIT),
        name="out_proj",
    )(*outs, *lses, gated, x2d, wout, gattn)


CAND_J_COUNT = (16, 8, 5, 4, 3, 2, 2, 2)


def _top16_groups(vals, ids):
    vals = list(vals)
    out_v, out_i = [], []
    for k in range(PEER_TOPK):
        bv, bi = vals[0], ids[0]
        for v, i in zip(vals[1:], ids[1:]):
            gt = v > bv
            bv = jnp.where(gt, v, bv)
            bi = jnp.where(gt, i, bi)
        for sh in (4, 2, 1):
            rv, ri = pltpu.roll(bv, sh, axis=0), pltpu.roll(bi, sh, axis=0)
            better = (rv > bv) | ((rv == bv) & (ri < bi))
            bv = jnp.where(better, rv, bv)
            bi = jnp.where(better, ri, bi)
        out_v.append(bv)
        out_i.append(bi)
        if k + 1 < PEER_TOPK:
            vals = [jnp.where(i == bi, -jnp.inf, v) for v, i in zip(vals, ids)]
    return out_v, out_i


def _stack(reps, start):
    row = lax.broadcasted_iota(jnp.int32, reps[0].shape, 0)
    out = reps[start + SUBLANES - 1]
    for r in range(SUBLANES - 2, -1, -1):
        out = jnp.where(row == r, reps[start + r], out)
    return out


def _select_experts(sa, sb):
    t = sa.shape[1]
    row = lax.broadcasted_iota(jnp.int32, (SUBLANES, t), 0)
    groups = lambda s: [s[g * SUBLANES:(g + 1) * SUBLANES, :] for g in range(N_KEYS // SUBLANES)]
    key_ids = [row + g * SUBLANES for g in range(N_KEYS // SUBLANES)]
    va, ia = _top16_groups(groups(sa), key_ids)
    vb, ib = _top16_groups(groups(sb), key_ids)
    vb_lo, vb_hi, va_hi = _stack(vb, 0), _stack(vb, SUBLANES), _stack(va, SUBLANES)
    cand = [va[0] + vb_lo, va[0] + vb_hi]
    cand_ids = [row, row + SUBLANES]
    for i in range(1, SUBLANES):
        cand.append(jnp.where(row < CAND_J_COUNT[i], va[i] + vb_lo, -jnp.inf))
        cand_ids.append(row + i * PEER_TOPK)
    cand.append(va_hi + vb[0])
    cand_ids.append((row + SUBLANES) * PEER_TOPK)
    top_s, pos = _top16_groups(cand, cand_ids)
    idx_halves, e_halves = [], []
    for h in range(2):
        p = _stack(pos, h * SUBLANES)
        pa, pb = p >> 4, p & (PEER_TOPK - 1)
        ea, eb = jnp.zeros_like(p), jnp.zeros_like(p)
        for i in range(PEER_TOPK):
            ea = jnp.where(pa == i, ia[i], ea)
            eb = jnp.where(pb == i, ib[i], eb)
        idx_halves.append((ea * N_KEYS + eb) * TABLE_ROWS_PER_EXPERT)
        e_halves.append(jnp.exp(_stack(top_s, h * SUBLANES) - top_s[0]))
    den = e_halves[0] + e_halves[1]
    for sh in (4, 2, 1):
        den = den + pltpu.roll(den, sh, axis=0)
    return jnp.concatenate(idx_halves, axis=0), jnp.concatenate([e / den for e in e_halves], axis=0)


def _route_kernel(x2_ref, gffn_ref, wq_ref, ka_ref, kb_ref, xn_ref, idx_ref, gate_ref):
    x = x2_ref[...]
    ms = jnp.mean(x * x, axis=-1, keepdims=True)
    xn = x * lax.rsqrt(ms + RMS_EPS) * gffn_ref[...]
    _store_token_tiles(xn, xn_ref)
    qh = _dot(xn.astype(jnp.bfloat16), wq_ref[...])
    half = D_KEY // 2
    for lt in range(x.shape[0] // LANES):
        ts = slice(lt * LANES, (lt + 1) * LANES)
        idx_rows, gate_rows = [], []
        for h in range(PEER_HEADS):
            qa = qh[ts, h * D_KEY:h * D_KEY + half].astype(jnp.bfloat16)
            qb = qh[ts, h * D_KEY + half:(h + 1) * D_KEY].astype(jnp.bfloat16)
            idx_h, gate_h = _select_experts(_dot_nt(ka_ref[...], qa), _dot_nt(kb_ref[...], qb))
            idx_rows.append(idx_h)
            gate_rows.append(gate_h)
        idx_ref[ts, :] = jnp.concatenate(idx_rows, axis=0).T
        gate_ref[ts, :] = jnp.concatenate(gate_rows, axis=0).T


def _peer_route(x2, gffn, wq, ka, kb):
    n = x2.shape[0]
    tm = TM_ROUTE
    tok = lambda w: pl.BlockSpec((tm, w), lambda i: (i, 0))
    full = lambda shape: pl.BlockSpec(shape, lambda i: (0,) * len(shape))
    return pl.pallas_call(
        _route_kernel,
        grid=(n // tm,),
        in_specs=[tok(D_MODEL), full((1, D_MODEL)), full((D_MODEL, PEER_HEADS * D_KEY)),
                  full((N_KEYS, D_KEY // 2)), full((N_KEYS, D_KEY // 2))],
        out_specs=[pl.BlockSpec((tm * SUBLANES, LANES), lambda i: (i, 0)), tok(NJ), tok(NJ)],
        out_shape=[jax.ShapeDtypeStruct((n * SUBLANES, LANES), jnp.float32),
                   jax.ShapeDtypeStruct((n, NJ), jnp.int32),
                   jax.ShapeDtypeStruct((n, NJ), jnp.float32)],
        compiler_params=pltpu.CompilerParams(dimension_semantics=("arbitrary",), vmem_limit_bytes=VMEM_LIMIT),
        name="peer_route",
    )(x2, gffn, wq, ka, kb)


def _pack_kernel(t_ref, o_ref):
    x = t_ref[...]
    bits = lax.bitcast_convert_type(x.astype(jnp.bfloat16).astype(jnp.float32), jnp.uint32)
    half = D_MODEL // 2
    word = (bits[:, :half] >> 16) | (bits[:, half:] & jnp.uint32(0xFFFF0000))
    for s in range(TABLE_ROWS_PER_EXPERT):
        o_ref[pl.ds(s, x.shape[0], stride=TABLE_ROWS_PER_EXPERT), :] = word[:, s * LANES:(s + 1) * LANES]


def _pack_table(tab):
    e, te = tab.shape[0], TE_PACK
    return pl.pallas_call(
        _pack_kernel,
        grid=(e // te,),
        in_specs=[pl.BlockSpec((te, D_MODEL), lambda i: (i, 0))],
        out_specs=pl.BlockSpec((te * TABLE_ROWS_PER_EXPERT, LANES), lambda i: (i, 0)),
        out_shape=jax.ShapeDtypeStruct((e * TABLE_ROWS_PER_EXPERT, LANES), jnp.uint32),
        compiler_params=pltpu.CompilerParams(dimension_semantics=("arbitrary",), vmem_limit_bytes=VMEM_LIMIT),
        name="pack_table",
    )(tab)


def _unpack(w):
    lo = lax.bitcast_convert_type(w << 16, jnp.float32)
    hi = lax.bitcast_convert_type(w & jnp.uint32(0xFFFF0000), jnp.float32)
    return lo, hi


def _gather_rows(idx_ref, base, tab_ref, w_ref):
    for g in range(NJ // SUBLANES):
        window = idx_ref.at[pl.ds(base + g * SUBLANES, SUBLANES)]
        for r in range(SUBLANES):
            off = pl.multiple_of(window[r], TABLE_ROWS_PER_EXPERT)
            w_ref[pl.ds((g * SUBLANES + r) * TABLE_ROWS_PER_EXPERT, TABLE_ROWS_PER_EXPERT), :] = (
                tab_ref[pl.ds(off, TABLE_ROWS_PER_EXPERT), :])


def _group_chunk(w_ref, g, s):
    return _unpack(w_ref[pl.ds(g * SUBLANES * TABLE_ROWS_PER_EXPERT + s, SUBLANES, stride=TABLE_ROWS_PER_EXPERT), :])


def _peer_up_kernel(idx_ref, gate_ref, xn_ref, tab_ref, coef_ref, w_ref, act_ref, *, tb):
    def partial_dots(t):
        _gather_rows(idx_ref, t * NJ, tab_ref, w_ref)
        xt = xn_ref[t]
        xb = [jnp.broadcast_to(xt[c:c + 1, :], (SUBLANES, LANES)) for c in range(SUBLANES)]
        parts = []
        for g in range(NJ // SUBLANES):
            acc = None
            for s in range(4):
                lo, hi = _group_chunk(w_ref, g, s)
                term = lo * xb[s] + hi * xb[4 + s]
                acc = term if acc is None else acc + term
            parts.append(acc)
        return jnp.concatenate(parts, axis=0)

    def lane_sums(p, t):
        act_ref[pl.ds(t, 1), :] = jnp.sum(p.T, axis=0, keepdims=True)

    def body(i, p_prev):
        t0 = 2 * i
        lane_sums(p_prev, jnp.maximum(t0 - 1, 0))
        pa = partial_dots(t0)
        lane_sums(pa, t0)
        return partial_dots(t0 + 1)

    p_last = lax.fori_loop(0, tb // 2, body, jnp.zeros((NJ, LANES), jnp.float32))
    lane_sums(p_last, tb - 1)
    coef_ref[...] = gate_ref[...] * _gelu(act_ref[...])


def _peer_down_kernel(idx_ref, coef_ref, x2_ref, tab_ref, o_ref, w_ref, ca_ref, cb_ref, ot_ref, *, tb):
    def spread(t, c_ref):
        c_ref[...] = jnp.broadcast_to(coef_ref[pl.ds(t, 1), :], (NJ, LANES)).T

    def weighted_sum(t, c_ref):
        _gather_rows(idx_ref, t * NJ, tab_ref, w_ref)
        acc_lo, acc_hi = [None] * 4, [None] * 4
        for g in range(NJ // SUBLANES):
            cg = c_ref[pl.ds(g * SUBLANES, SUBLANES), :]
            for s in range(4):
                lo, hi = _group_chunk(w_ref, g, s)
                acc_lo[s] = lo * cg if acc_lo[s] is None else acc_lo[s] + lo * cg
                acc_hi[s] = hi * cg if acc_hi[s] is None else acc_hi[s] + hi * cg
        rows = [jnp.sum(a, axis=0, keepdims=True) for a in acc_lo + acc_hi]
        ot_ref[pl.ds(pl.multiple_of(t * SUBLANES, SUBLANES), SUBLANES), :] = x2_ref[t] + jnp.concatenate(rows, axis=0)

    spread(0, ca_ref)

    def body(i, carry):
        t0 = 2 * i
        spread(t0 + 1, cb_ref)
        weighted_sum(t0, ca_ref)
        spread(jnp.minimum(t0 + 2, tb - 1), ca_ref)
        weighted_sum(t0 + 1, cb_ref)
        return carry

    lax.fori_loop(0, tb // 2, body, 0)
    for c in range(D_MODEL // LANES):
        o_ref[:, c * LANES:(c + 1) * LANES] = ot_ref[pl.ds(c, tb, stride=D_MODEL // LANES), :]


def _peer_specs(n, table_rows, tb):
    idx_spec = pl.BlockSpec((tb * NJ,), lambda i: (i,), memory_space=pltpu.SMEM)
    row_spec = pl.BlockSpec((tb, NJ), lambda i: (i, 0))
    tok_spec = pl.BlockSpec((tb, SUBLANES, LANES), lambda i: (i, 0, 0))
    tab_spec = pl.BlockSpec((table_rows, LANES), lambda i: (0, 0), pipeline_mode=pl.Buffered(1))
    return idx_spec, row_spec, tok_spec, tab_spec


_W_SCRATCH = pltpu.VMEM((NJ * TABLE_ROWS_PER_EXPERT, LANES), jnp.uint32)
_SQUARE_SCRATCH = pltpu.VMEM((NJ, LANES), jnp.float32)


def _peer_up(idx, gate, xn, tab):
    n, tb = gate.shape[0], TB_PEER
    idx_spec, row_spec, tok_spec, tab_spec = _peer_specs(n, tab.shape[0], tb)
    return pl.pallas_call(
        functools.partial(_peer_up_kernel, tb=tb),
        grid=(n // tb,),
        in_specs=[idx_spec, row_spec, tok_spec, tab_spec],
        out_specs=row_spec,
        out_shape=jax.ShapeDtypeStruct((n, NJ), jnp.float32),
        scratch_shapes=[_W_SCRATCH, pltpu.VMEM((tb, NJ), jnp.float32)],
        compiler_params=pltpu.CompilerParams(dimension_semantics=("arbitrary",), vmem_limit_bytes=VMEM_LIMIT),
        name="peer_up",
    )(idx.reshape(-1), gate, xn.reshape(n, SUBLANES, LANES), tab)


def _peer_down(idx, coef, x2, tab):
    n, tb = coef.shape[0], TB_PEER
    idx_spec, row_spec, tok_spec, tab_spec = _peer_specs(n, tab.shape[0], tb)
    return pl.pallas_call(
        functools.partial(_peer_down_kernel, tb=tb),
        grid=(n // tb,),
        in_specs=[idx_spec, row_spec, tok_spec, tab_spec],
        out_specs=pl.BlockSpec((tb, D_MODEL), lambda i: (i, 0)),
        out_shape=jax.ShapeDtypeStruct((n, D_MODEL), jnp.float32),
        scratch_shapes=[_W_SCRATCH, _SQUARE_SCRATCH, _SQUARE_SCRATCH, pltpu.VMEM((tb * SUBLANES, LANES), jnp.float32)],
        compiler_params=pltpu.CompilerParams(dimension_semantics=("arbitrary",), vmem_limit_bytes=VMEM_LIMIT),
        name="peer_down",
    )(idx.reshape(-1), coef, x2.reshape(n, SUBLANES, LANES), tab)


def _layer(x2d, batch, seq, mix_norm_g, w_in, q_norm_g, k_norm_g, v_gate_norm_g, w_spatial, b_spatial, attn_out_g,
           gate_out_g, w_out, ffn_norm_g, w_query, sub_keys_a, sub_keys_b, expert_u, expert_v):
    bf16 = jnp.bfloat16
    heads = ATTN_WIDTH // HEAD_DIM
    lane = jnp.arange(ATTN_WIDTH)
    g64 = (lane[:, None] // HEAD_DIM == lane[None, :] // HEAD_DIM).astype(bf16)
    gq = (jnp.tile(q_norm_g, heads) * HEAD_DIM ** -0.5)[None, :]
    gk = jnp.tile(k_norm_g, heads)[None, :]
    bsp = jnp.repeat(b_spatial.T, GMLP_WIDTH // GMLP_GROUPS, axis=1)
    *qkv, gated = _in_proj(x2d, batch, seq, mix_norm_g[None, :], w_in.astype(bf16), gq, gk, g64,
                           v_gate_norm_g.reshape(1, GMLP_WIDTH), w_spatial, bsp, gate_out_g[None, :])
    n_pat = len(PATTERNS)
    outs, lses = [], []
    for p, (window, dilation) in enumerate(PATTERNS):
        assert window // dilation == N_BACK
        o, l = _attention(qkv[p], qkv[n_pat + p], qkv[2 * n_pat + p], dilation)
        outs.append(o)
        lses.append(l)
    x2 = _out_proj(outs, lses, gated, x2d, seq, w_out.astype(bf16), attn_out_g[None, :])
    xn, idx, gate = _peer_route(x2, ffn_norm_g[None, :], w_query.astype(bf16), sub_keys_a.astype(bf16),
                                sub_keys_b.astype(bf16))
    coef = _peer_up(idx, gate, xn, _pack_table(expert_u))
    return _peer_down(idx, coef, x2, _pack_table(expert_v))


def kernel(x, mix_norm_g, w_in, q_norm_g, k_norm_g, v_gate_norm_g, w_spatial, b_spatial, attn_out_g, gate_out_g,
           w_out, ffn_norm_g, w_query, sub_keys_a, sub_keys_b, expert_u, expert_v):
    batch, seq, d = x.shape
    assert d == D_MODEL and seq % TM_PROJ == 0 and (batch * seq) % TB_PEER == 0 and TB_PEER % 2 == 0
    for _, dilation in PATTERNS:
        sub_len = seq // dilation
        assert seq % dilation == 0 and sub_len % ATTN_BLOCK == 0 and sub_len % min(TQ_ATTN, sub_len) == 0
    x2d = x.reshape(batch * seq, d)
    for l in range(mix_norm_g.shape[0]):
        x2d = _layer(x2d, batch, seq, mix_norm_g[l], w_in[l], q_norm_g[l], k_norm_g[l], v_gate_norm_g[l],
                     w_spatial[l], b_spatial[l], attn_out_g[l], gate_out_g[l], w_out[l], ffn_norm_g[l],
                     w_query[l], sub_keys_a[l], sub_keys_b[l], expert_u[l], expert_v[l])
    return x2d.reshape(batch, seq, d)
```

```python
import functools

import jax
import jax.numpy as jnp
from jax import lax
from jax.experimental import pallas as pl
from jax.experimental.pallas import tpu as pltpu

D_MODEL = 1024
ATTN_WIDTH = 512
HEAD_DIM = 64
HEAD_PAIRS = ATTN_WIDTH // 128
PATTERNS = ((128, 1), (512, 4), (2048, 16))
N_BACK = 128
ATTN_BLOCK = 128
GMLP_WIDTH = 512
GMLP_CHUNK = 128
GMLP_GROUPS = 4
IN_WIDTH = 3 * ATTN_WIDTH + 2 * GMLP_WIDTH
PEER_HEADS = 8
PEER_TOPK = 16
N_KEYS = 128
D_KEY = 256
NJ = PEER_HEADS * PEER_TOPK
TABLE_ROWS_PER_EXPERT = 4
RMS_EPS = 1e-6
NEG = -1e30

LANES = 128
SUBLANES = 8
VMEM_LIMIT = 56 * 2 ** 20

TM_PROJ = 512
TQ_ATTN = 512
TM_ROUTE = 256
TB_PEER = 256
TE_PACK = 512


def _gelu(x):
    return 0.5 * x * (1.0 + lax.erf(x * (2.0 ** -0.5)))


def _dot(a, b):
    return jnp.dot(a, b, preferred_element_type=jnp.float32)


def _dot_nt(a, b):
    return lax.dot_general(a, b, (((1,), (1,)), ((), ())), preferred_element_type=jnp.float32)


def _store_views(val, stage_ref, out_refs):
    chunks = ATTN_WIDTH // LANES
    for c in range(chunks):
        stage_ref[c] = val[:, c * LANES:(c + 1) * LANES]
    for (_, dilation), o_ref in zip(PATTERNS, out_refs):
        if dilation == 1:
            o_ref[0] = val.astype(jnp.bfloat16)
            continue
        rows = val.shape[0] // dilation
        for r in range(dilation):
            for c in range(chunks):
                lanes = pl.ds(r * ATTN_WIDTH + c * LANES, LANES)
                o_ref[0, :, lanes] = stage_ref[c, pl.ds(r, rows, stride=dilation), :].astype(jnp.bfloat16)


def _in_proj_kernel(x_ref, gmix_ref, win_ref, gq_ref, gk_ref, g64_ref, gvg_ref, wsp_ref, bsp_ref, gout_ref,
                    q1_ref, q2_ref, q3_ref, k1_ref, k2_ref, k3_ref, v1_ref, v2_ref, v3_ref, gated_ref,
                    gs_ref, stage_ref):
    x = x_ref[...]
    ms = jnp.mean(x * x, axis=-1, keepdims=True)
    hn = (x * lax.rsqrt(ms + RMS_EPS) * gmix_ref[...]).astype(jnp.bfloat16)

    def head_norm(t, g):
        msq = _dot((t * t).astype(jnp.bfloat16), g64_ref[...]) * (1.0 / HEAD_DIM)
        return t * lax.rsqrt(msq + RMS_EPS) * g

    q = _dot(hn, win_ref[:, 0:ATTN_WIDTH])
    _store_views(head_norm(q, gq_ref[...]), stage_ref, (q1_ref, q2_ref, q3_ref))
    k = _dot(hn, win_ref[:, ATTN_WIDTH:2 * ATTN_WIDTH])
    _store_views(head_norm(k, gk_ref[...]), stage_ref, (k1_ref, k2_ref, k3_ref))
    _store_views(_dot(hn, win_ref[:, 2 * ATTN_WIDTH:3 * ATTN_WIDTH]), stage_ref, (v1_ref, v2_ref, v3_ref))

    u = _gelu(_dot(hn, win_ref[:, 3 * ATTN_WIDTH:3 * ATTN_WIDTH + GMLP_WIDTH]))
    gv = _gelu(_dot(hn, win_ref[:, 3 * ATTN_WIDTH + GMLP_WIDTH:IN_WIDTH]))
    row = lax.broadcasted_iota(jnp.int32, (GMLP_CHUNK, GMLP_CHUNK), 0)
    col = lax.broadcasted_iota(jnp.int32, (GMLP_CHUNK, GMLP_CHUNK), 1)
    causal = col <= row
    n_chunks = x.shape[0] // GMLP_CHUNK
    for g in range(GMLP_GROUPS):
        cs = slice(g * LANES, (g + 1) * LANES)
        vg = gv[:, cs]
        msg = jnp.mean(vg * vg, axis=-1, keepdims=True)
        vn = (vg * lax.rsqrt(msg + RMS_EPS) * gvg_ref[:, cs]).astype(jnp.bfloat16)
        w = jnp.where(causal, wsp_ref[g], 0.0).astype(jnp.bfloat16)
        for c in range(n_chunks):
            rs = slice(c * GMLP_CHUNK, (c + 1) * GMLP_CHUNK)
            z = _dot(w, vn[rs, :]) + bsp_ref[:, cs]
            gs_ref[rs, cs] = u[rs, cs] * z
    gated = gs_ref[...]
    msg = jnp.mean(gated * gated, axis=-1, keepdims=True)
    gated_ref[...] = (gated * lax.rsqrt(msg + RMS_EPS) * gout_ref[...]).astype(jnp.bfloat16)


def _view_spec(seq, tm, dilation):
    blocks_per_row = seq // tm
    return pl.BlockSpec((1, tm // dilation, dilation * ATTN_WIDTH),
                        lambda i: (i // blocks_per_row, i % blocks_per_row, 0))


def _view_shape(batch, seq, dilation, dtype):
    return jax.ShapeDtypeStruct((batch, seq // dilation, dilation * ATTN_WIDTH), dtype)


def _in_proj(x2d, batch, seq, gmix, win, gq, gk, g64, gvg, wsp, bsp, gout):
    n = x2d.shape[0]
    tm = TM_PROJ
    full = lambda shape: pl.BlockSpec(shape, lambda i: (0,) * len(shape))
    tok = lambda w: pl.BlockSpec((tm, w), lambda i: (i, 0))
    view_specs = [_view_spec(seq, tm, d) for _, d in PATTERNS] * 3
    view_shapes = [_view_shape(batch, seq, d, jnp.bfloat16) for _, d in PATTERNS] * 3
    return pl.pallas_call(
        _in_proj_kernel,
        grid=(n // tm,),
        in_specs=[tok(D_MODEL), full((1, D_MODEL)), full((D_MODEL, IN_WIDTH)), full((1, ATTN_WIDTH)),
                  full((1, ATTN_WIDTH)), full((ATTN_WIDTH, ATTN_WIDTH)), full((1, GMLP_WIDTH)),
                  full((GMLP_GROUPS, GMLP_CHUNK, GMLP_CHUNK)), full((GMLP_CHUNK, GMLP_WIDTH)), full((1, GMLP_WIDTH))],
        out_specs=view_specs + [tok(GMLP_WIDTH)],
        out_shape=view_shapes + [jax.ShapeDtypeStruct((n, GMLP_WIDTH), jnp.bfloat16)],
        scratch_shapes=[pltpu.VMEM((tm, GMLP_WIDTH), jnp.float32),
                        pltpu.VMEM((ATTN_WIDTH // LANES, tm, LANES), jnp.float32)],
        compiler_params=pltpu.CompilerParams(dimension_semantics=("arbitrary",), vmem_limit_bytes=VMEM_LIMIT),
        name="in_proj",
    )(x2d, gmix, win, gq, gk, g64, gvg, wsp, bsp, gout)


def _attn_kernel(q_ref, kp_ref, kc_ref, vp_ref, vc_ref, o_ref, l_ref):
    tq = q_ref.shape[1]
    have_prev = pl.program_id(2) > 0
    qi = lax.broadcasted_iota(jnp.int32, (ATTN_BLOCK, 2 * ATTN_BLOCK), 0)
    kj = lax.broadcasted_iota(jnp.int32, (ATTN_BLOCK, 2 * ATTN_BLOCK), 1)
    rel = qi + ATTN_BLOCK - kj
    band = (rel >= 0) & (rel <= N_BACK)
    lane = lax.broadcasted_iota(jnp.int32, (ATTN_BLOCK, LANES), 1)
    n_lane_tiles = q_ref.shape[2] // LANES
    for qb in range(tq // ATTN_BLOCK):
        rs = slice(qb * ATTN_BLOCK, (qb + 1) * ATTN_BLOCK)
        if qb == 0:
            mask = band & ((kj >= ATTN_BLOCK) | have_prev)
        else:
            mask = band
        for p in range(n_lane_tiles):
            cs = slice(p * LANES, (p + 1) * LANES)
            qp = q_ref[0, rs, cs]
            if qb == 0:
                kprev, vprev = kp_ref[0, :, cs], vp_ref[0, :, cs]
            else:
                ps = slice((qb - 1) * ATTN_BLOCK, qb * ATTN_BLOCK)
                kprev, vprev = kc_ref[0, ps, cs], vc_ref[0, ps, cs]
            keys = jnp.concatenate([kprev, kc_ref[0, rs, cs]], axis=0)
            vals = jnp.concatenate([vprev, vc_ref[0, rs, cs]], axis=0)
            out_pair = jnp.zeros((ATTN_BLOCK, LANES), jnp.float32)
            lse_pair = jnp.zeros((ATTN_BLOCK, LANES), jnp.float32)
            for hh in range(2):
                in_head = (lane >= hh * HEAD_DIM) & (lane < (hh + 1) * HEAD_DIM)
                s = _dot_nt(jnp.where(in_head, qp, jnp.zeros_like(qp)), keys)
                s = jnp.where(mask, s, NEG)
                m = jnp.max(s, axis=-1, keepdims=True)
                e = jnp.exp(s - m)
                den = jnp.sum(e, axis=-1, keepdims=True)
                o = _dot(e.astype(jnp.bfloat16), vals) / den
                out_pair = jnp.where(in_head, o, out_pair)
                lse_pair = jnp.where(in_head, m + jnp.log(den), lse_pair)
            o_ref[0, rs, cs] = out_pair.astype(o_ref.dtype)
            l_ref[0, rs, cs] = lse_pair


def _attention(q, k, v, dilation):
    batch, L, _ = q.shape
    tq = min(TQ_ATTN, L)
    res = min(dilation, TQ_ATTN // tq)
    cur = pl.BlockSpec((1, tq, res * ATTN_WIDTH), lambda b, r, i: (b, i, r))
    prev = pl.BlockSpec((1, ATTN_BLOCK, res * ATTN_WIDTH),
                        lambda b, r, i: (b, jnp.maximum(i * (tq // ATTN_BLOCK) - 1, 0), r))
    shape = (batch, L, dilation * ATTN_WIDTH)
    return pl.pallas_call(
        _attn_kernel,
        grid=(batch, dilation // res, L // tq),
        in_specs=[cur, prev, cur, prev, cur],
        out_specs=[cur, cur],
        out_shape=[jax.ShapeDtypeStruct(shape, jnp.bfloat16), jax.ShapeDtypeStruct(shape, jnp.float32)],
        compiler_params=pltpu.CompilerParams(dimension_semantics=("arbitrary",) * 3, vmem_limit_bytes=VMEM_LIMIT),
        name=f"attention_d{dilation}",
    )(q, k, k, v, v)


def _load_view(v_ref, stage_ref, dilation):
    if dilation == 1:
        return v_ref[0].astype(jnp.float32)
    rows = v_ref.shape[1]
    chunks = ATTN_WIDTH // LANES
    for r in range(dilation):
        for c in range(chunks):
            stage_ref[c, pl.ds(r, rows, stride=dilation), :] = (
                v_ref[0, :, pl.ds(r * ATTN_WIDTH + c * LANES, LANES)].astype(jnp.float32))
    return jnp.concatenate([stage_ref[c] for c in range(chunks)], axis=1)


def _store_token_tiles(val, o_ref):
    chunks = val.shape[1] // LANES
    for c in range(chunks):
        o_ref[pl.ds(c, val.shape[0], stride=chunks), :] = val[:, c * LANES:(c + 1) * LANES]


def _out_proj_kernel(o1_ref, o2_ref, o3_ref, l1_ref, l2_ref, l3_ref, gated_ref, x_ref, wout_ref, gattn_ref, x2_ref,
                     so2_ref, so3_ref, sl2_ref, sl3_ref):
    dil = [d for _, d in PATTERNS]
    l1 = _load_view(l1_ref, None, dil[0])
    l2 = _load_view(l2_ref, sl2_ref, dil[1])
    l3 = _load_view(l3_ref, sl3_ref, dil[2])
    o1 = _load_view(o1_ref, None, dil[0])
    o2 = _load_view(o2_ref, so2_ref, dil[1])
    o3 = _load_view(o3_ref, so3_ref, dil[2])
    m = jnp.maximum(jnp.maximum(l1, l2), l3)
    e1, e2, e3 = jnp.exp(l1 - m), jnp.exp(l2 - m), jnp.exp(l3 - m)
    attn = (e1 * o1 + e2 * o2 + e3 * o3) / (e1 + e2 + e3)
    ms = jnp.mean(attn * attn, axis=-1, keepdims=True)
    attn_n = (attn * lax.rsqrt(ms + RMS_EPS) * gattn_ref[...]).astype(jnp.bfloat16)
    y = _dot(attn_n, wout_ref[0:ATTN_WIDTH, :]) + _dot(gated_ref[...], wout_ref[ATTN_WIDTH:, :])
    x2_ref[...] = x_ref[...] + y


def _out_proj(outs, lses, gated, x2d, seq, wout, gattn):
    n = x2d.shape[0]
    tm = TM_PROJ
    tok = lambda w: pl.BlockSpec((tm, w), lambda i: (i, 0))
    full = lambda shape: pl.BlockSpec(shape, lambda i: (0,) * len(shape))
    view_specs = [_view_spec(seq, tm, d) for _, d in PATTERNS] * 2
    stage = pltpu.VMEM((ATTN_WIDTH // LANES, tm, LANES), jnp.float32)
    return pl.pallas_call(
        _out_proj_kernel,
        grid=(n // tm,),
        in_specs=view_specs + [tok(GMLP_WIDTH), tok(D_MODEL), full((D_MODEL, D_MODEL)), full((1, ATTN_WIDTH))],
        out_specs=tok(D_MODEL),
        out_shape=jax.ShapeDtypeStruct((n, D_MODEL), jnp.float32),
        scratch_shapes=[stage] * 4,
        compiler_params=pltpu.CompilerParams(dimension_semantics=("arbitrary",), vmem_limit_bytes=VMEM_LIMIT),
        name="out_proj",
    )(*outs, *lses, gated, x2d, wout, gattn)


CAND_J_COUNT = (16, 8, 5, 4, 3, 2, 2, 2)


def _top16_groups(vals, ids):
    vals = list(vals)
    out_v, out_i = [], []
    for k in range(PEER_TOPK):
        bv, bi = vals[0], ids[0]
        for v, i in zip(vals[1:], ids[1:]):
            gt = v > bv
            bv = jnp.where(gt, v, bv)
            bi = jnp.where(gt, i, bi)
        for sh in (4, 2, 1):
            rv, ri = pltpu.roll(bv, sh, axis=0), pltpu.roll(bi, sh, axis=0)
            better = (rv > bv) | ((rv == bv) & (ri < bi))
            bv = jnp.where(better, rv, bv)
            bi = jnp.where(better, ri, bi)
        out_v.append(bv)
        out_i.append(bi)
        if k + 1 < PEER_TOPK:
            vals = [jnp.where(i == bi, -jnp.inf, v) for v, i in zip(vals, ids)]
    return out_v, out_i


def _stack(reps, start):
    row = lax.broadcasted_iota(jnp.int32, reps[0].shape, 0)
    out = reps[start + SUBLANES - 1]
    for r in range(SUBLANES - 2, -1, -1):
        out = jnp.where(row == r, reps[start + r], out)
    return out


def _select_experts(sa, sb):
    t = sa.shape[1]
    row = lax.broadcasted_iota(jnp.int32, (SUBLANES, t), 0)
    groups = lambda s: [s[g * SUBLANES:(g + 1) * SUBLANES, :] for g in range(N_KEYS // SUBLANES)]
    key_ids = [row + g * SUBLANES for g in range(N_KEYS // SUBLANES)]
    va, ia = _top16_groups(groups(sa), key_ids)
    vb, ib = _top16_groups(groups(sb), key_ids)
    vb_lo, vb_hi, va_hi = _stack(vb, 0), _stack(vb, SUBLANES), _stack(va, SUBLANES)
    cand = [va[0] + vb_lo, va[0] + vb_hi]
    cand_ids = [row, row + SUBLANES]
    for i in range(1, SUBLANES):
        cand.append(jnp.where(row < CAND_J_COUNT[i], va[i] + vb_lo, -jnp.inf))
        cand_ids.append(row + i * PEER_TOPK)
    cand.append(va_hi + vb[0])
    cand_ids.append((row + SUBLANES) * PEER_TOPK)
    top_s, pos = _top16_groups(cand, cand_ids)
    idx_halves, e_halves = [], []
    for h in range(2):
        p = _stack(pos, h * SUBLANES)
        pa, pb = p >> 4, p & (PEER_TOPK - 1)
        ea, eb = jnp.zeros_like(p), jnp.zeros_like(p)
        for i in range(PEER_TOPK):
            ea = jnp.where(pa == i, ia[i], ea)
            eb = jnp.where(pb == i, ib[i], eb)
        idx_halves.append((ea * N_KEYS + eb) * TABLE_ROWS_PER_EXPERT)
        e_halves.append(jnp.exp(_stack(top_s, h * SUBLANES) - top_s[0]))
    den = e_halves[0] + e_halves[1]
    for sh in (4, 2, 1):
        den = den + pltpu.roll(den, sh, axis=0)
    return jnp.concatenate(idx_halves, axis=0), jnp.concatenate([e / den for e in e_halves], axis=0)


def _route_kernel(x2_ref, gffn_ref, wq_ref, ka_ref, kb_ref, xn_ref, idx_ref, gate_ref):
    x = x2_ref[...]
    ms = jnp.mean(x * x, axis=-1, keepdims=True)
    xn = x * lax.rsqrt(ms + RMS_EPS) * gffn_ref[...]
    _store_token_tiles(xn, xn_ref)
    qh = _dot(xn.astype(jnp.bfloat16), wq_ref[...])
    half = D_KEY // 2
    for lt in range(x.shape[0] // LANES):
        ts = slice(lt * LANES, (lt + 1) * LANES)
        idx_rows, gate_rows = [], []
        for h in range(PEER_HEADS):
            qa = qh[ts, h * D_KEY:h * D_KEY + half].astype(jnp.bfloat16)
            qb = qh[ts, h * D_KEY + half:(h + 1) * D_KEY].astype(jnp.bfloat16)
            idx_h, gate_h = _select_experts(_dot_nt(ka_ref[...], qa), _dot_nt(kb_ref[...], qb))
            idx_rows.append(idx_h)
            gate_rows.append(gate_h)
        idx_ref[ts, :] = jnp.concatenate(idx_rows, axis=0).T
        gate_ref[ts, :] = jnp.concatenate(gate_rows, axis=0).T


def _peer_route(x2, gffn, wq, ka, kb):
    n = x2.shape[0]
    tm = TM_ROUTE
    tok = lambda w: pl.BlockSpec((tm, w), lambda i: (i, 0))
    full = lambda shape: pl.BlockSpec(shape, lambda i: (0,) * len(shape))
    return pl.pallas_call(
        _route_kernel,
        grid=(n // tm,),
        in_specs=[tok(D_MODEL), full((1, D_MODEL)), full((D_MODEL, PEER_HEADS * D_KEY)),
                  full((N_KEYS, D_KEY // 2)), full((N_KEYS, D_KEY // 2))],
        out_specs=[pl.BlockSpec((tm * SUBLANES, LANES), lambda i: (i, 0)), tok(NJ), tok(NJ)],
        out_shape=[jax.ShapeDtypeStruct((n * SUBLANES, LANES), jnp.float32),
                   jax.ShapeDtypeStruct((n, NJ), jnp.int32),
                   jax.ShapeDtypeStruct((n, NJ), jnp.float32)],
        compiler_params=pltpu.CompilerParams(dimension_semantics=("arbitrary",), vmem_limit_bytes=VMEM_LIMIT),
        name="peer_route",
    )(x2, gffn, wq, ka, kb)


def _pack_kernel(t_ref, o_ref):
    x = t_ref[...]
    bits = lax.bitcast_convert_type(x.astype(jnp.bfloat16).astype(jnp.float32), jnp.uint32)
    half = D_MODEL // 2
    word = (bits[:, :half] >> 16) | (bits[:, half:] & jnp.uint32(0xFFFF0000))
    for s in range(TABLE_ROWS_PER_EXPERT):
        o_ref[pl.ds(s, x.shape[0], stride=TABLE_ROWS_PER_EXPERT), :] = word[:, s * LANES:(s + 1) * LANES]


def _pack_table(tab):
    e, te = tab.shape[0], TE_PACK
    return pl.pallas_call(
        _pack_kernel,
        grid=(e // te,),
        in_specs=[pl.BlockSpec((te, D_MODEL), lambda i: (i, 0))],
        out_specs=pl.BlockSpec((te * TABLE_ROWS_PER_EXPERT, LANES), lambda i: (i, 0)),
        out_shape=jax.ShapeDtypeStruct((e * TABLE_ROWS_PER_EXPERT, LANES), jnp.uint32),
        compiler_params=pltpu.CompilerParams(dimension_semantics=("arbitrary",), vmem_limit_bytes=VMEM_LIMIT),
        name="pack_table",
    )(tab)


def _unpack(w):
    lo = lax.bitcast_convert_type(w << 16, jnp.float32)
    hi = lax.bitcast_convert_type(w & jnp.uint32(0xFFFF0000), jnp.float32)
    return lo, hi


def _gather_rows(idx_ref, base, tab_ref, w_ref):
    for g in range(NJ // SUBLANES):
        window = idx_ref.at[pl.ds(base + g * SUBLANES, SUBLANES)]
        for r in range(SUBLANES):
            off = pl.multiple_of(window[r], TABLE_ROWS_PER_EXPERT)
            w_ref[pl.ds((g * SUBLANES + r) * TABLE_ROWS_PER_EXPERT, TABLE_ROWS_PER_EXPERT), :] = (
                tab_ref[pl.ds(off, TABLE_ROWS_PER_EXPERT), :])


def _group_chunk(w_ref, g, s):
    return _unpack(w_ref[pl.ds(g * SUBLANES * TABLE_ROWS_PER_EXPERT + s, SUBLANES, stride=TABLE_ROWS_PER_EXPERT), :])


def _peer_up_kernel(idx_ref, gate_ref, xn_ref, tab_ref, coef_ref, w_ref, act_ref, *, tb):
    def partial_dots(t):
        _gather_rows(idx_ref, t * NJ, tab_ref, w_ref)
        xt = xn_ref[t]
        xb = [jnp.broadcast_to(xt[c:c + 1, :], (SUBLANES, LANES)) for c in range(SUBLANES)]
        parts = []
        for g in range(NJ // SUBLANES):
            acc = None
            for s in range(4):
                lo, hi = _group_chunk(w_ref, g, s)
                term = lo * xb[s] + hi * xb[4 + s]
                acc = term if acc is None else acc + term
            parts.append(acc)
        return jnp.concatenate(parts, axis=0)

    def lane_sums(p, t):
        act_ref[pl.ds(t, 1), :] = jnp.sum(p.T, axis=0, keepdims=True)

    def body(i, p_prev):
        t0 = 2 * i
        lane_sums(p_prev, jnp.maximum(t0 - 1, 0))
        pa = partial_dots(t0)
        lane_sums(pa, t0)
        return partial_dots(t0 + 1)

    p_last = lax.fori_loop(0, tb // 2, body, jnp.zeros((NJ, LANES), jnp.float32))
    lane_sums(p_last, tb - 1)
    coef_ref[...] = gate_ref[...] * _gelu(act_ref[...])


def _peer_down_kernel(idx_ref, coef_ref, x2_ref, tab_ref, o_ref, w_ref, ca_ref, cb_ref, ot_ref, *, tb):
    def spread(t, c_ref):
        c_ref[...] = jnp.broadcast_to(coef_ref[pl.ds(t, 1), :], (NJ, LANES)).T

    def weighted_sum(t, c_ref):
        _gather_rows(idx_ref, t * NJ, tab_ref, w_ref)
        acc_lo, acc_hi = [None] * 4, [None] * 4
        for g in range(NJ // SUBLANES):
            cg = c_ref[pl.ds(g * SUBLANES, SUBLANES), :]
            for s in range(4):
                lo, hi = _group_chunk(w_ref, g, s)
                acc_lo[s] = lo * cg if acc_lo[s] is None else acc_lo[s] + lo * cg
                acc_hi[s] = hi * cg if acc_hi[s] is None else acc_hi[s] + hi * cg
        rows = [jnp.sum(a, axis=0, keepdims=True) for a in acc_lo + acc_hi]
        ot_ref[pl.ds(pl.multiple_of(t * SUBLANES, SUBLANES), SUBLANES), :] = x2_ref[t] + jnp.concatenate(rows, axis=0)

    spread(0, ca_ref)

    def body(i, carry):
        t0 = 2 * i
        spread(t0 + 1, cb_ref)
        weighted_sum(t0, ca_ref)
        spread(jnp.minimum(t0 + 2, tb - 1), ca_ref)
        weighted_sum(t0 + 1, cb_ref)
        return carry

    lax.fori_loop(0, tb // 2, body, 0)
    for c in range(D_MODEL // LANES):
        o_ref[:, c * LANES:(c + 1) * LANES] = ot_ref[pl.ds(c, tb, stride=D_MODEL // LANES), :]


def _peer_specs(n, table_rows, tb):
    idx_spec = pl.BlockSpec((tb * NJ,), lambda i: (i,), memory_space=pltpu.SMEM)
    row_spec = pl.BlockSpec((tb, NJ), lambda i: (i, 0))
    tok_spec = pl.BlockSpec((tb, SUBLANES, LANES), lambda i: (i, 0, 0))
    tab_spec = pl.BlockSpec((table_rows, LANES), lambda i: (0, 0), pipeline_mode=pl.Buffered(1))
    return idx_spec, row_spec, tok_spec, tab_spec


_W_SCRATCH = pltpu.VMEM((NJ * TABLE_ROWS_PER_EXPERT, LANES), jnp.uint32)
_SQUARE_SCRATCH = pltpu.VMEM((NJ, LANES), jnp.float32)


def _peer_up(idx, gate, xn, tab):
    n, tb = gate.shape[0], TB_PEER
    idx_spec, row_spec, tok_spec, tab_spec = _peer_specs(n, tab.shape[0], tb)
    return pl.pallas_call(
        functools.partial(_peer_up_kernel, tb=tb),
        grid=(n // tb,),
        in_specs=[idx_spec, row_spec, tok_spec, tab_spec],
        out_specs=row_spec,
        out_shape=jax.ShapeDtypeStruct((n, NJ), jnp.float32),
        scratch_shapes=[_W_SCRATCH, pltpu.VMEM((tb, NJ), jnp.float32)],
        compiler_params=pltpu.CompilerParams(dimension_semantics=("arbitrary",), vmem_limit_bytes=VMEM_LIMIT),
        name="peer_up",
    )(idx.reshape(-1), gate, xn.reshape(n, SUBLANES, LANES), tab)


def _peer_down(idx, coef, x2, tab):
    n, tb = coef.shape[0], TB_PEER
    idx_spec, row_spec, tok_spec, tab_spec = _peer_specs(n, tab.shape[0], tb)
    return pl.pallas_call(
        functools.partial(_peer_down_kernel, tb=tb),
        grid=(n // tb,),
        in_specs=[idx_spec, row_spec, tok_spec, tab_spec],
        out_specs=pl.BlockSpec((tb, D_MODEL), lambda i: (i, 0)),
        out_shape=jax.ShapeDtypeStruct((n, D_MODEL), jnp.float32),
        scratch_shapes=[_W_SCRATCH, _SQUARE_SCRATCH, _SQUARE_SCRATCH, pltpu.VMEM((tb * SUBLANES, LANES), jnp.float32)],
        compiler_params=pltpu.CompilerParams(dimension_semantics=("arbitrary",), vmem_limit_bytes=VMEM_LIMIT),
        name="peer_down",
    )(idx.reshape(-1), coef, x2.reshape(n, SUBLANES, LANES), tab)


def _layer(x2d, batch, seq, mix_norm_g, w_in, q_norm_g, k_norm_g, v_gate_norm_g, w_spatial, b_spatial, attn_out_g,
           gate_out_g, w_out, ffn_norm_g, w_query, sub_keys_a, sub_keys_b, expert_u, expert_v):
    bf16 = jnp.bfloat16
    heads = ATTN_WIDTH // HEAD_DIM
    lane = jnp.arange(ATTN_WIDTH)
    g64 = (lane[:, None] // HEAD_DIM == lane[None, :] // HEAD_DIM).astype(bf16)
    gq = (jnp.tile(q_norm_g, heads) * HEAD_DIM ** -0.5)[None, :]
    gk = jnp.tile(k_norm_g, heads)[None, :]
    bsp = jnp.repeat(b_spatial.T, GMLP_WIDTH // GMLP_GROUPS, axis=1)
    *qkv, gated = _in_proj(x2d, batch, seq, mix_norm_g[None, :], w_in.astype(bf16), gq, gk, g64,
                           v_gate_norm_g.reshape(1, GMLP_WIDTH), w_spatial, bsp, gate_out_g[None, :])
    n_pat = len(PATTERNS)
    outs, lses = [], []
    for p, (window, dilation) in enumerate(PATTERNS):
        assert window // dilation == N_BACK
        o, l = _attention(qkv[p], qkv[n_pat + p], qkv[2 * n_pat + p], dilation)
        outs.append(o)
        lses.append(l)
    x2 = _out_proj(outs, lses, gated, x2d, seq, w_out.astype(bf16), attn_out_g[None, :])
    xn, idx, gate = _peer_route(x2, ffn_norm_g[None, :], w_query.astype(bf16), sub_keys_a.astype(bf16),
                                sub_keys_b.astype(bf16))
    coef = _peer_up(idx, gate, xn, _pack_table(expert_u))
    return _peer_down(idx, coef, x2, _pack_table(expert_v))


def kernel(x, mix_norm_g, w_in, q_norm_g, k_norm_g, v_gate_norm_g, w_spatial, b_spatial, attn_out_g, gate_out_g,
           w_out, ffn_norm_g, w_query, sub_keys_a, sub_keys_b, expert_u, expert_v):
    batch, seq, d = x.shape
    assert d == D_MODEL and seq % TM_PROJ == 0 and (batch * seq) % TB_PEER == 0 and TB_PEER % 2 == 0
    for _, dilation in PATTERNS:
        sub_len = seq // dilation
        assert seq % dilation == 0 and sub_len % ATTN_BLOCK == 0 and sub_len % min(TQ_ATTN, sub_len) == 0
    x2d = x.reshape(batch * seq, d)
    for l in range(mix_norm_g.shape[0]):
        x2d = _layer(x2d, batch, seq, mix_norm_g[l], w_in[l], q_norm_g[l], k_norm_g[l], v_gate_norm_g[l],
                     w_spatial[l], b_spatial[l], attn_out_g[l], gate_out_g[l], w_out[l], ffn_norm_g[l],
                     w_query[l], sub_keys_a[l], sub_keys_b[l], expert_u[l], expert_v[l])
    return x2d.reshape(batch, seq, d)
```

```python
import functools

import jax
import jax.numpy as jnp
from jax import lax
from jax.experimental import pallas as pl
from jax.experimental.pallas import tpu as pltpu

D_MODEL = 1024
ATTN_WIDTH = 512
HEAD_DIM = 64
HEAD_PAIRS = ATTN_WIDTH // 128
PATTERNS = ((128, 1), (512, 4), (2048, 16))
N_BACK = 128
ATTN_BLOCK = 128
GMLP_WIDTH = 512
GMLP_CHUNK = 128
GMLP_GROUPS = 4
IN_WIDTH = 3 * ATTN_WIDTH + 2 * GMLP_WIDTH
PEER_HEADS = 8
PEER_TOPK = 16
N_KEYS = 128
D_KEY = 256
NJ = PEER_HEADS * PEER_TOPK
TABLE_ROWS_PER_EXPERT = 4
RMS_EPS = 1e-6
NEG = -1e30

LANES = 128
SUBLANES = 8
VMEM_LIMIT = 56 * 2 ** 20

TM_PROJ = 512
TQ_ATTN = 512
TM_ROUTE = 256
TB_PEER = 256
TE_PACK = 512


def _gelu(x):
    return 0.5 * x * (1.0 + lax.erf(x * (2.0 ** -0.5)))


def _dot(a, b):
    return jnp.dot(a, b, preferred_element_type=jnp.float32)


def _dot_nt(a, b):
    return lax.dot_general(a, b, (((1,), (1,)), ((), ())), preferred_element_type=jnp.float32)


def _store_views(val, stage_ref, out_refs):
    chunks = ATTN_WIDTH // LANES
    for c in range(chunks):
        stage_ref[c] = val[:, c * LANES:(c + 1) * LANES]
    for (_, dilation), o_ref in zip(PATTERNS, out_refs):
        if dilation == 1:
            o_ref[0] = val.astype(jnp.bfloat16)
            continue
        rows = val.shape[0] // dilation
        for r in range(dilation):
            for c in range(chunks):
                lanes = pl.ds(r * ATTN_WIDTH + c * LANES, LANES)
                o_ref[0, :, lanes] = stage_ref[c, pl.ds(r, rows, stride=dilation), :].astype(jnp.bfloat16)


def _in_proj_kernel(x_ref, gmix_ref, win_ref, gq_ref, gk_ref, g64_ref, gvg_ref, wsp_ref, bsp_ref, gout_ref,
                    q1_ref, q2_ref, q3_ref, k1_ref, k2_ref, k3_ref, v1_ref, v2_ref, v3_ref, gated_ref,
                    gs_ref, stage_ref):
    x = x_ref[...]
    ms = jnp.mean(x * x, axis=-1, keepdims=True)
    hn = (x * lax.rsqrt(ms + RMS_EPS) * gmix_ref[...]).astype(jnp.bfloat16)

    def head_norm(t, g):
        msq = _dot((t * t).astype(jnp.bfloat16), g64_ref[...]) * (1.0 / HEAD_DIM)
        return t * lax.rsqrt(msq + RMS_EPS) * g

    q = _dot(hn, win_ref[:, 0:ATTN_WIDTH])
    _store_views(head_norm(q, gq_ref[...]), stage_ref, (q1_ref, q2_ref, q3_ref))
    k = _dot(hn, win_ref[:, ATTN_WIDTH:2 * ATTN_WIDTH])
    _store_views(head_norm(k, gk_ref[...]), stage_ref, (k1_ref, k2_ref, k3_ref))
    _store_views(_dot(hn, win_ref[:, 2 * ATTN_WIDTH:3 * ATTN_WIDTH]), stage_ref, (v1_ref, v2_ref, v3_ref))

    u = _gelu(_dot(hn, win_ref[:, 3 * ATTN_WIDTH:3 * ATTN_WIDTH + GMLP_WIDTH]))
    gv = _gelu(_dot(hn, win_ref[:, 3 * ATTN_WIDTH + GMLP_WIDTH:IN_WIDTH]))
    row = lax.broadcasted_iota(jnp.int32, (GMLP_CHUNK, GMLP_CHUNK), 0)
    col = lax.broadcasted_iota(jnp.int32, (GMLP_CHUNK, GMLP_CHUNK), 1)
    causal = col <= row
    n_chunks = x.shape[0] // GMLP_CHUNK
    for g in range(GMLP_GROUPS):
        cs = slice(g * LANES, (g + 1) * LANES)
        vg = gv[:, cs]
        msg = jnp.mean(vg * vg, axis=-1, keepdims=True)
        vn = (vg * lax.rsqrt(msg + RMS_EPS) * gvg_ref[:, cs]).astype(jnp.bfloat16)
        w = jnp.where(causal, wsp_ref[g], 0.0).astype(jnp.bfloat16)
        for c in range(n_chunks):
            rs = slice(c * GMLP_CHUNK, (c + 1) * GMLP_CHUNK)
            z = _dot(w, vn[rs, :]) + bsp_ref[:, cs]
            gs_ref[rs, cs] = u[rs, cs] * z
    gated = gs_ref[...]
    msg = jnp.mean(gated * gated, axis=-1, keepdims=True)
    gated_ref[...] = (gated * lax.rsqrt(msg + RMS_EPS) * gout_ref[...]).astype(jnp.bfloat16)


def _view_spec(seq, tm, dilation):
    blocks_per_row = seq // tm
    return pl.BlockSpec((1, tm // dilation, dilation * ATTN_WIDTH),
                        lambda i: (i // blocks_per_row, i % blocks_per_row, 0))


def _view_shape(batch, seq, dilation, dtype):
    return jax.ShapeDtypeStruct((batch, seq // dilation, dilation * ATTN_WIDTH), dtype)


def _in_proj(x2d, batch, seq, gmix, win, gq, gk, g64, gvg, wsp, bsp, gout):
    n = x2d.shape[0]
    tm = TM_PROJ
    full = lambda shape: pl.BlockSpec(shape, lambda i: (0,) * len(shape))
    tok = lambda w: pl.BlockSpec((tm, w), lambda i: (i, 0))
    view_specs = [_view_spec(seq, tm, d) for _, d in PATTERNS] * 3
    view_shapes = [_view_shape(batch, seq, d, jnp.bfloat16) for _, d in PATTERNS] * 3
    return pl.pallas_call(
        _in_proj_kernel,
        grid=(n // tm,),
        in_specs=[tok(D_MODEL), full((1, D_MODEL)), full((D_MODEL, IN_WIDTH)), full((1, ATTN_WIDTH)),
                  full((1, ATTN_WIDTH)), full((ATTN_WIDTH, ATTN_WIDTH)), full((1, GMLP_WIDTH)),
                  full((GMLP_GROUPS, GMLP_CHUNK, GMLP_CHUNK)), full((GMLP_CHUNK, GMLP_WIDTH)), full((1, GMLP_WIDTH))],
        out_specs=view_specs + [tok(GMLP_WIDTH)],
        out_shape=view_shapes + [jax.ShapeDtypeStruct((n, GMLP_WIDTH), jnp.bfloat16)],
        scratch_shapes=[pltpu.VMEM((tm, GMLP_WIDTH), jnp.float32),
                        pltpu.VMEM((ATTN_WIDTH // LANES, tm, LANES), jnp.float32)],
        compiler_params=pltpu.CompilerParams(dimension_semantics=("arbitrary",), vmem_limit_bytes=VMEM_LIMIT),
        name="in_proj",
    )(x2d, gmix, win, gq, gk, g64, gvg, wsp, bsp, gout)


def _attn_kernel(q_ref, kp_ref, kc_ref, vp_ref, vc_ref, o_ref, l_ref):
    tq = q_ref.shape[1]
    have_prev = pl.program_id(2) > 0
    qi = lax.broadcasted_iota(jnp.int32, (ATTN_BLOCK, 2 * ATTN_BLOCK), 0)
    kj = lax.broadcasted_iota(jnp.int32, (ATTN_BLOCK, 2 * ATTN_BLOCK), 1)
    rel = qi + ATTN_BLOCK - kj
    band = (rel >= 0) & (rel <= N_BACK)
    lane = lax.broadcasted_iota(jnp.int32, (ATTN_BLOCK, LANES), 1)
    n_lane_tiles = q_ref.shape[2] // LANES
    for qb in range(tq // ATTN_BLOCK):
        rs = slice(qb * ATTN_BLOCK, (qb + 1) * ATTN_BLOCK)
        if qb == 0:
            mask = band & ((kj >= ATTN_BLOCK) | have_prev)
        else:
            mask = band
        for p in range(n_lane_tiles):
            cs = slice(p * LANES, (p + 1) * LANES)
            qp = q_ref[0, rs, cs]
            if qb == 0:
                kprev, vprev = kp_ref[0, :, cs], vp_ref[0, :, cs]
            else:
                ps = slice((qb - 1) * ATTN_BLOCK, qb * ATTN_BLOCK)
                kprev, vprev = kc_ref[0, ps, cs], vc_ref[0, ps, cs]
            keys = jnp.concatenate([kprev, kc_ref[0, rs, cs]], axis=0)
            vals = jnp.concatenate([vprev, vc_ref[0, rs, cs]], axis=0)
            out_pair = jnp.zeros((ATTN_BLOCK, LANES), jnp.float32)
            lse_pair = jnp.zeros((ATTN_BLOCK, LANES), jnp.float32)
            for hh in range(2):
                in_head = (lane >= hh * HEAD_DIM) & (lane < (hh + 1) * HEAD_DIM)
                s = _dot_nt(jnp.where(in_head, qp, jnp.zeros_like(qp)), keys)
                s = jnp.where(mask, s, NEG)
                m = jnp.max(s, axis=-1, keepdims=True)
                e = jnp.exp(s - m)
                den = jnp.sum(e, axis=-1, keepdims=True)
                o = _dot(e.astype(jnp.bfloat16), vals) / den
                out_pair = jnp.where(in_head, o, out_pair)
                lse_pair = jnp.where(in_head, m + jnp.log(den), lse_pair)
            o_ref[0, rs, cs] = out_pair.astype(o_ref.dtype)
            l_ref[0, rs, cs] = lse_pair


def _attention(q, k, v, dilation):
    batch, L, _ = q.shape
    tq = min(TQ_ATTN, L)
    res = min(dilation, TQ_ATTN // tq)
    cur = pl.BlockSpec((1, tq, res * ATTN_WIDTH), lambda b, r, i: (b, i, r))
    prev = pl.BlockSpec((1, ATTN_BLOCK, res * ATTN_WIDTH),
                        lambda b, r, i: (b, jnp.maximum(i * (tq // ATTN_BLOCK) - 1, 0), r))
    shape = (batch, L, dilation * ATTN_WIDTH)
    return pl.pallas_call(
        _attn_kernel,
        grid=(batch, dilation // res, L // tq),
        in_specs=[cur, prev, cur, prev, cur],
        out_specs=[cur, cur],
        out_shape=[jax.ShapeDtypeStruct(shape, jnp.bfloat16), jax.ShapeDtypeStruct(shape, jnp.float32)],
        compiler_params=pltpu.CompilerParams(dimension_semantics=("arbitrary",) * 3, vmem_limit_bytes=VMEM_LIMIT),
        name=f"attention_d{dilation}",
    )(q, k, k, v, v)


def _load_view(v_ref, stage_ref, dilation):
    if dilation == 1:
        return v_ref[0].astype(jnp.float32)
    rows = v_ref.shape[1]
    chunks = ATTN_WIDTH // LANES
    for r in range(dilation):
        for c in range(chunks):
            stage_ref[c, pl.ds(r, rows, stride=dilation), :] = (
                v_ref[0, :, pl.ds(r * ATTN_WIDTH + c * LANES, LANES)].astype(jnp.float32))
    return jnp.concatenate([stage_ref[c] for c in range(chunks)], axis=1)


def _store_token_tiles(val, o_ref):
    chunks = val.shape[1] // LANES
    for c in range(chunks):
        o_ref[pl.ds(c, val.shape[0], stride=chunks), :] = val[:, c * LANES:(c + 1) * LANES]


def _out_proj_kernel(o1_ref, o2_ref, o3_ref, l1_ref, l2_ref, l3_ref, gated_ref, x_ref, wout_ref, gattn_ref, x2_ref,
                     so2_ref, so3_ref, sl2_ref, sl3_ref):
    dil = [d for _, d in PATTERNS]
    l1 = _load_view(l1_ref, None, dil[0])
    l2 = _load_view(l2_ref, sl2_ref, dil[1])
    l3 = _load_view(l3_ref, sl3_ref, dil[2])
    o1 = _load_view(o1_ref, None, dil[0])
    o2 = _load_view(o2_ref, so2_ref, dil[1])
    o3 = _load_view(o3_ref, so3_ref, dil[2])
    m = jnp.maximum(jnp.maximum(l1, l2), l3)
    e1, e2, e3 = jnp.exp(l1 - m), jnp.exp(l2 - m), jnp.exp(l3 - m)
    attn = (e1 * o1 + e2 * o2 + e3 * o3) / (e1 + e2 + e3)
    ms = jnp.mean(attn * attn, axis=-1, keepdims=True)
    attn_n = (attn * lax.rsqrt(ms + RMS_EPS) * gattn_ref[...]).astype(jnp.bfloat16)
    y = _dot(attn_n, wout_ref[0:ATTN_WIDTH, :]) + _dot(gated_ref[...], wout_ref[ATTN_WIDTH:, :])
    x2_ref[...] = x_ref[...] + y


def _out_proj(outs, lses, gated, x2d, seq, wout, gattn):
    n = x2d.shape[0]
    tm = TM_PROJ
    tok = lambda w: pl.BlockSpec((tm, w), lambda i: (i, 0))
    full = lambda shape: pl.BlockSpec(shape, lambda i: (0,) * len(shape))
    view_specs = [_view_spec(seq, tm, d) for _, d in PATTERNS] * 2
    stage = pltpu.VMEM((ATTN_WIDTH // LANES, tm, LANES), jnp.float32)
    return pl.pallas_call(
        _out_proj_kernel,
        grid=(n // tm,),
        in_specs=view_specs + [tok(GMLP_WIDTH), tok(D_MODEL), full((D_MODEL, D_MODEL)), full((1, ATTN_WIDTH))],
        out_specs=tok(D_MODEL),
        out_shape=jax.ShapeDtypeStruct((n, D_MODEL), jnp.float32),
        scratch_shapes=[stage] * 4,
        compiler_params=pltpu.CompilerParams(dimension_semantics=("arbitrary",), vmem_limit_bytes=VMEM_LIMIT),
        name="out_proj",
    )(*outs, *lses, gated, x2d, wout, gattn)


CAND_J_COUNT = (16, 8, 5, 4, 3, 2, 2, 2)


def _top16_groups(vals, ids):
    vals = list(vals)
    out_v, out_i = [], []
    for k in range(PEER_TOPK):
        bv, bi = vals[0], ids[0]
        for v, i in zip(vals[1:], ids[1:]):
            gt = v > bv
            bv = jnp.where(gt, v, bv)
            bi = jnp.where(gt, i, bi)
        for sh in (4, 2, 1):
            rv, ri = pltpu.roll(bv, sh, axis=0), pltpu.roll(bi, sh, axis=0)
            better = (rv > bv) | ((rv == bv) & (ri < bi))
            bv = jnp.where(better, rv, bv)
            bi = jnp.where(better, ri, bi)
        out_v.append(bv)
        out_i.append(bi)
        if k + 1 < PEER_TOPK:
            vals = [jnp.where(i == bi, -jnp.inf, v) for v, i in zip(vals, ids)]
    return out_v, out_i


def _stack(reps, start):
    row = lax.broadcasted_iota(jnp.int32, reps[0].shape, 0)
    out = reps[start + SUBLANES - 1]
    for r in range(SUBLANES - 2, -1, -1):
        out = jnp.where(row == r, reps[start + r], out)
    return out


def _select_experts(sa, sb):
    t = sa.shape[1]
    row = lax.broadcasted_iota(jnp.int32, (SUBLANES, t), 0)
    groups = lambda s: [s[g * SUBLANES:(g + 1) * SUBLANES, :] for g in range(N_KEYS // SUBLANES)]
    key_ids = [row + g * SUBLANES for g in range(N_KEYS // SUBLANES)]
    va, ia = _top16_groups(groups(sa), key_ids)
    vb, ib = _top16_groups(groups(sb), key_ids)
    vb_lo, vb_hi, va_hi = _stack(vb, 0), _stack(vb, SUBLANES), _stack(va, SUBLANES)
    cand = [va[0] + vb_lo, va[0] + vb_hi]
    cand_ids = [row, row + SUBLANES]
    for i in range(1, SUBLANES):
        cand.append(jnp.where(row < CAND_J_COUNT[i], va[i] + vb_lo, -jnp.inf))
        cand_ids.append(row + i * PEER_TOPK)
    cand.append(va_hi + vb[0])
    cand_ids.append((row + SUBLANES) * PEER_TOPK)
    top_s, pos = _top16_groups(cand, cand_ids)
    idx_halves, e_halves = [], []
    for h in range(2):
        p = _stack(pos, h * SUBLANES)
        pa, pb = p >> 4, p & (PEER_TOPK - 1)
        ea, eb = jnp.zeros_like(p), jnp.zeros_like(p)
        for i in range(PEER_TOPK):
            ea = jnp.where(pa == i, ia[i], ea)
            eb = jnp.where(pb == i, ib[i], eb)
        idx_halves.append((ea * N_KEYS + eb) * TABLE_ROWS_PER_EXPERT)
        e_halves.append(jnp.exp(_stack(top_s, h * SUBLANES) - top_s[0]))
    den = e_halves[0] + e_halves[1]
    for sh in (4, 2, 1):
        den = den + pltpu.roll(den, sh, axis=0)
    return jnp.concatenate(idx_halves, axis=0), jnp.concatenate([e / den for e in e_halves], axis=0)


def _route_kernel(x2_ref, gffn_ref, wq_ref, ka_ref, kb_ref, xn_ref, idx_ref, gate_ref):
    x = x2_ref[...]
    ms = jnp.mean(x * x, axis=-1, keepdims=True)
    xn = x * lax.rsqrt(ms + RMS_EPS) * gffn_ref[...]
    _store_token_tiles(xn, xn_ref)
    qh = _dot(xn.astype(jnp.bfloat16), wq_ref[...])
    half = D_KEY // 2
    for lt in range(x.shape[0] // LANES):
        ts = slice(lt * LANES, (lt + 1) * LANES)
        idx_rows, gate_rows = [], []
        for h in range(PEER_HEADS):
            qa = qh[ts, h * D_KEY:h * D_KEY + half].astype(jnp.bfloat16)
            qb = qh[ts, h * D_KEY + half:(h + 1) * D_KEY].astype(jnp.bfloat16)
            idx_h, gate_h = _select_experts(_dot_nt(ka_ref[...], qa), _dot_nt(kb_ref[...], qb))
            idx_rows.append(idx_h)
            gate_rows.append(gate_h)
        idx_ref[ts, :] = jnp.concatenate(idx_rows, axis=0).T
        gate_ref[ts, :] = jnp.concatenate(gate_rows, axis=0).T


def _peer_route(x2, gffn, wq, ka, kb):
    n = x2.shape[0]
    tm = TM_ROUTE
    tok = lambda w: pl.BlockSpec((tm, w), lambda i: (i, 0))
    full = lambda shape: pl.BlockSpec(shape, lambda i: (0,) * len(shape))
    return pl.pallas_call(
        _route_kernel,
        grid=(n // tm,),
        in_specs=[tok(D_MODEL), full((1, D_MODEL)), full((D_MODEL, PEER_HEADS * D_KEY)),
                  full((N_KEYS, D_KEY // 2)), full((N_KEYS, D_KEY // 2))],
        out_specs=[pl.BlockSpec((tm * SUBLANES, LANES), lambda i: (i, 0)), tok(NJ), tok(NJ)],
        out_shape=[jax.ShapeDtypeStruct((n * SUBLANES, LANES), jnp.float32),
                   jax.ShapeDtypeStruct((n, NJ), jnp.int32),
                   jax.ShapeDtypeStruct((n, NJ), jnp.float32)],
        compiler_params=pltpu.CompilerParams(dimension_semantics=("arbitrary",), vmem_limit_bytes=VMEM_LIMIT),
        name="peer_route",
    )(x2, gffn, wq, ka, kb)


def _pack_kernel(t_ref, o_ref):
    x = t_ref[...]
    bits = lax.bitcast_convert_type(x.astype(jnp.bfloat16).astype(jnp.float32), jnp.uint32)
    half = D_MODEL // 2
    word = (bits[:, :half] >> 16) | (bits[:, half:] & jnp.uint32(0xFFFF0000))
    for s in range(TABLE_ROWS_PER_EXPERT):
        o_ref[pl.ds(s, x.shape[0], stride=TABLE_ROWS_PER_EXPERT), :] = word[:, s * LANES:(s + 1) * LANES]


def _pack_table(tab):
    e, te = tab.shape[0], TE_PACK
    return pl.pallas_call(
        _pack_kernel,
        grid=(e // te,),
        in_specs=[pl.BlockSpec((te, D_MODEL), lambda i: (i, 0))],
        out_specs=pl.BlockSpec((te * TABLE_ROWS_PER_EXPERT, LANES), lambda i: (i, 0)),
        out_shape=jax.ShapeDtypeStruct((e * TABLE_ROWS_PER_EXPERT, LANES), jnp.uint32),
        compiler_params=pltpu.CompilerParams(dimension_semantics=("arbitrary",), vmem_limit_bytes=VMEM_LIMIT),
        name="pack_table",
    )(tab)


def _unpack(w):
    lo = lax.bitcast_convert_type(w << 16, jnp.float32)
    hi = lax.bitcast_convert_type(w & jnp.uint32(0xFFFF0000), jnp.float32)
    return lo, hi


def _gather_rows(idx_ref, base, tab_ref, w_ref):
    for g in range(NJ // SUBLANES):
        window = idx_ref.at[pl.ds(base + g * SUBLANES, SUBLANES)]
        for r in range(SUBLANES):
            off = pl.multiple_of(window[r], TABLE_ROWS_PER_EXPERT)
            w_ref[pl.ds((g * SUBLANES + r) * TABLE_ROWS_PER_EXPERT, TABLE_ROWS_PER_EXPERT), :] = (
                tab_ref[pl.ds(off, TABLE_ROWS_PER_EXPERT), :])


def _group_chunk(w_ref, g, s):
    return _unpack(w_ref[pl.ds(g * SUBLANES * TABLE_ROWS_PER_EXPERT + s, SUBLANES, stride=TABLE_ROWS_PER_EXPERT), :])


def _peer_up_kernel(idx_ref, gate_ref, xn_ref, tab_ref, coef_ref, w_ref, act_ref, *, tb):
    def partial_dots(t):
        _gather_rows(idx_ref, t * NJ, tab_ref, w_ref)
        xt = xn_ref[t]
        xb = [jnp.broadcast_to(xt[c:c + 1, :], (SUBLANES, LANES)) for c in range(SUBLANES)]
        parts = []
        for g in range(NJ // SUBLANES):
            acc = None
            for s in range(4):
                lo, hi = _group_chunk(w_ref, g, s)
                term = lo * xb[s] + hi * xb[4 + s]
                acc = term if acc is None else acc + term
            parts.append(acc)
        return jnp.concatenate(parts, axis=0)

    def lane_sums(p, t):
        act_ref[pl.ds(t, 1), :] = jnp.sum(p.T, axis=0, keepdims=True)

    def body(t, p_prev):
        lane_sums(p_prev, jnp.maximum(t - 1, 0))
        return partial_dots(t)

    p_last = lax.fori_loop(0, tb, body, jnp.zeros((NJ, LANES), jnp.float32))
    lane_sums(p_last, tb - 1)
    coef_ref[...] = gate_ref[...] * _gelu(act_ref[...])


def _peer_down_kernel(idx_ref, coef_ref, x2_ref, tab_ref, o_ref, w_ref, ca_ref, cb_ref, ot_ref, *, tb):
    def spread(t, c_ref):
        c_ref[...] = jnp.broadcast_to(coef_ref[pl.ds(t, 1), :], (NJ, LANES)).T

    def weighted_sum(t, c_ref):
        _gather_rows(idx_ref, t * NJ, tab_ref, w_ref)
        acc_lo, acc_hi = [None] * 4, [None] * 4
        for g in range(NJ // SUBLANES):
            cg = c_ref[pl.ds(g * SUBLANES, SUBLANES), :]
            for s in range(4):
                lo, hi = _group_chunk(w_ref, g, s)
                acc_lo[s] = lo * cg if acc_lo[s] is None else acc_lo[s] + lo * cg
                acc_hi[s] = hi * cg if acc_hi[s] is None else acc_hi[s] + hi * cg
        rows = [jnp.sum(a, axis=0, keepdims=True) for a in acc_lo + acc_hi]
        ot_ref[pl.ds(pl.multiple_of(t * SUBLANES, SUBLANES), SUBLANES), :] = x2_ref[t] + jnp.concatenate(rows, axis=0)

    spread(0, ca_ref)

    def body(i, carry):
        t0 = 2 * i
        spread(t0 + 1, cb_ref)
        weighted_sum(t0, ca_ref)
        spread(jnp.minimum(t0 + 2, tb - 1), ca_ref)
        weighted_sum(t0 + 1, cb_ref)
        return carry

    lax.fori_loop(0, tb // 2, body, 0)
    for c in range(D_MODEL // LANES):
        o_ref[:, c * LANES:(c + 1) * LANES] = ot_ref[pl.ds(c, tb, stride=D_MODEL // LANES), :]


def _peer_specs(n, table_rows, tb):
    idx_spec = pl.BlockSpec((tb * NJ,), lambda i: (i,), memory_space=pltpu.SMEM)
    row_spec = pl.BlockSpec((tb, NJ), lambda i: (i, 0))
    tok_spec = pl.BlockSpec((tb, SUBLANES, LANES), lambda i: (i, 0, 0))
    tab_spec = pl.BlockSpec((table_rows, LANES), lambda i: (0, 0), pipeline_mode=pl.Buffered(1))
    return idx_spec, row_spec, tok_spec, tab_spec


_W_SCRATCH = pltpu.VMEM((NJ * TABLE_ROWS_PER_EXPERT, LANES), jnp.uint32)
_SQUARE_SCRATCH = pltpu.VMEM((NJ, LANES), jnp.float32)


def _peer_up(idx, gate, xn, tab):
    n, tb = gate.shape[0], TB_PEER
    idx_spec, row_spec, tok_spec, tab_spec = _peer_specs(n, tab.shape[0], tb)
    return pl.pallas_call(
        functools.partial(_peer_up_kernel, tb=tb),
        grid=(n // tb,),
        in_specs=[idx_spec, row_spec, tok_spec, tab_spec],
        out_specs=row_spec,
        out_shape=jax.ShapeDtypeStruct((n, NJ), jnp.float32),
        scratch_shapes=[_W_SCRATCH, pltpu.VMEM((tb, NJ), jnp.float32)],
        compiler_params=pltpu.CompilerParams(dimension_semantics=("arbitrary",), vmem_limit_bytes=VMEM_LIMIT),
        name="peer_up",
    )(idx.reshape(-1), gate, xn.reshape(n, SUBLANES, LANES), tab)


def _peer_down(idx, coef, x2, tab):
    n, tb = coef.shape[0], TB_PEER
    idx_spec, row_spec, tok_spec, tab_spec = _peer_specs(n, tab.shape[0], tb)
    return pl.pallas_call(
        functools.partial(_peer_down_kernel, tb=tb),
        grid=(n // tb,),
        in_specs=[idx_spec, row_spec, tok_spec, tab_spec],
        out_specs=pl.BlockSpec((tb, D_MODEL), lambda i: (i, 0)),
        out_shape=jax.ShapeDtypeStruct((n, D_MODEL), jnp.float32),
        scratch_shapes=[_W_SCRATCH, _SQUARE_SCRATCH, _SQUARE_SCRATCH, pltpu.VMEM((tb * SUBLANES, LANES), jnp.float32)],
        compiler_params=pltpu.CompilerParams(dimension_semantics=("arbitrary",), vmem_limit_bytes=VMEM_LIMIT),
        name="peer_down",
    )(idx.reshape(-1), coef, x2.reshape(n, SUBLANES, LANES), tab)


def _layer(x2d, batch, seq, mix_norm_g, w_in, q_norm_g, k_norm_g, v_gate_norm_g, w_spatial, b_spatial, attn_out_g,
           gate_out_g, w_out, ffn_norm_g, w_query, sub_keys_a, sub_keys_b, expert_u, expert_v):
    bf16 = jnp.bfloat16
    heads = ATTN_WIDTH // HEAD_DIM
    lane = jnp.arange(ATTN_WIDTH)
    g64 = (lane[:, None] // HEAD_DIM == lane[None, :] // HEAD_DIM).astype(bf16)
    gq = (jnp.tile(q_norm_g, heads) * HEAD_DIM ** -0.5)[None, :]
    gk = jnp.tile(k_norm_g, heads)[None, :]
    bsp = jnp.repeat(b_spatial.T, GMLP_WIDTH // GMLP_GROUPS, axis=1)
    *qkv, gated = _in_proj(x2d, batch, seq, mix_norm_g[None, :], w_in.astype(bf16), gq, gk, g64,
                           v_gate_norm_g.reshape(1, GMLP_WIDTH), w_spatial, bsp, gate_out_g[None, :])
    n_pat = len(PATTERNS)
    outs, lses = [], []
    for p, (window, dilation) in enumerate(PATTERNS):
        assert window // dilation == N_BACK
        o, l = _attention(qkv[p], qkv[n_pat + p], qkv[2 * n_pat + p], dilation)
        outs.append(o)
        lses.append(l)
    x2 = _out_proj(outs, lses, gated, x2d, seq, w_out.astype(bf16), attn_out_g[None, :])
    xn, idx, gate = _peer_route(x2, ffn_norm_g[None, :], w_query.astype(bf16), sub_keys_a.astype(bf16),
                                sub_keys_b.astype(bf16))
    coef = _peer_up(idx, gate, xn, _pack_table(expert_u))
    return _peer_down(idx, coef, x2, _pack_table(expert_v))


def kernel(x, mix_norm_g, w_in, q_norm_g, k_norm_g, v_gate_norm_g, w_spatial, b_spatial, attn_out_g, gate_out_g,
           w_out, ffn_norm_g, w_query, sub_keys_a, sub_keys_b, expert_u, expert_v):
    batch, seq, d = x.shape
    assert d == D_MODEL and seq % TM_PROJ == 0 and (batch * seq) % TB_PEER == 0 and TB_PEER % 2 == 0
    for _, dilation in PATTERNS:
        sub_len = seq // dilation
        assert seq % dilation == 0 and sub_len % ATTN_BLOCK == 0 and sub_len % min(TQ_ATTN, sub_len) == 0
    x2d = x.reshape(batch * seq, d)
    for l in range(mix_norm_g.shape[0]):
        x2d = _layer(x2d, batch, seq, mix_norm_g[l], w_in[l], q_norm_g[l], k_norm_g[l], v_gate_norm_g[l],
                     w_spatial[l], b_spatial[l], attn_out_g[l], gate_out_g[l], w_out[l], ffn_norm_g[l],
                     w_query[l], sub_keys_a[l], sub_keys_b[l], expert_u[l], expert_v[l])
    return x2d.reshape(batch, seq, d)
```

```python
import functools

import jax
import jax.numpy as jnp
from jax import lax
from jax.experimental import pallas as pl
from jax.experimental.pallas import tpu as pltpu

D_MODEL = 1024
ATTN_WIDTH = 512
HEAD_DIM = 64
HEAD_PAIRS = ATTN_WIDTH // 128
PATTERNS = ((128, 1), (512, 4), (2048, 16))
N_BACK = 128
ATTN_BLOCK = 128
GMLP_WIDTH = 512
GMLP_CHUNK = 128
GMLP_GROUPS = 4
IN_WIDTH = 3 * ATTN_WIDTH + 2 * GMLP_WIDTH
PEER_HEADS = 8
PEER_TOPK = 16
N_KEYS = 128
D_KEY = 256
NJ = PEER_HEADS * PEER_TOPK
TABLE_ROWS_PER_EXPERT = 4
RMS_EPS = 1e-6
NEG = -1e30

LANES = 128
SUBLANES = 8
VMEM_LIMIT = 56 * 2 ** 20

TM_PROJ = 512
TQ_ATTN = 512
TM_ROUTE = 256
TB_PEER = 256
TE_PACK = 512


def _gelu(x):
    return 0.5 * x * (1.0 + lax.erf(x * (2.0 ** -0.5)))


def _dot(a, b):
    return jnp.dot(a, b, preferred_element_type=jnp.float32)


def _dot_nt(a, b):
    return lax.dot_general(a, b, (((1,), (1,)), ((), ())), preferred_element_type=jnp.float32)


def _store_views(val, stage_ref, out_refs):
    chunks = ATTN_WIDTH // LANES
    for c in range(chunks):
        stage_ref[c] = val[:, c * LANES:(c + 1) * LANES]
    for (_, dilation), o_ref in zip(PATTERNS, out_refs):
        if dilation == 1:
            o_ref[0] = val.astype(jnp.bfloat16)
            continue
        rows = val.shape[0] // dilation
        for r in range(dilation):
            for c in range(chunks):
                lanes = pl.ds(r * ATTN_WIDTH + c * LANES, LANES)
                o_ref[0, :, lanes] = stage_ref[c, pl.ds(r, rows, stride=dilation), :].astype(jnp.bfloat16)


def _in_proj_kernel(x_ref, gmix_ref, win_ref, gq_ref, gk_ref, g64_ref, gvg_ref, wsp_ref, bsp_ref, gout_ref,
                    q1_ref, q2_ref, q3_ref, k1_ref, k2_ref, k3_ref, v1_ref, v2_ref, v3_ref, gated_ref,
                    gs_ref, stage_ref):
    x = x_ref[...]
    ms = jnp.mean(x * x, axis=-1, keepdims=True)
    hn = (x * lax.rsqrt(ms + RMS_EPS) * gmix_ref[...]).astype(jnp.bfloat16)

    def head_norm(t, g):
        msq = _dot((t * t).astype(jnp.bfloat16), g64_ref[...]) * (1.0 / HEAD_DIM)
        return t * lax.rsqrt(msq + RMS_EPS) * g

    q = _dot(hn, win_ref[:, 0:ATTN_WIDTH])
    _store_views(head_norm(q, gq_ref[...]), stage_ref, (q1_ref, q2_ref, q3_ref))
    k = _dot(hn, win_ref[:, ATTN_WIDTH:2 * ATTN_WIDTH])
    _store_views(head_norm(k, gk_ref[...]), stage_ref, (k1_ref, k2_ref, k3_ref))
    _store_views(_dot(hn, win_ref[:, 2 * ATTN_WIDTH:3 * ATTN_WIDTH]), stage_ref, (v1_ref, v2_ref, v3_ref))

    u = _gelu(_dot(hn, win_ref[:, 3 * ATTN_WIDTH:3 * ATTN_WIDTH + GMLP_WIDTH]))
    gv = _gelu(_dot(hn, win_ref[:, 3 * ATTN_WIDTH + GMLP_WIDTH:IN_WIDTH]))
    row = lax.broadcasted_iota(jnp.int32, (GMLP_CHUNK, GMLP_CHUNK), 0)
    col = lax.broadcasted_iota(jnp.int32, (GMLP_CHUNK, GMLP_CHUNK), 1)
    causal = col <= row
    n_chunks = x.shape[0] // GMLP_CHUNK
    for g in range(GMLP_GROUPS):
        cs = slice(g * LANES, (g + 1) * LANES)
        vg = gv[:, cs]
        msg = jnp.mean(vg * vg, axis=-1, keepdims=True)
        vn = (vg * lax.rsqrt(msg + RMS_EPS) * gvg_ref[:, cs]).astype(jnp.bfloat16)
        w = jnp.where(causal, wsp_ref[g], 0.0).astype(jnp.bfloat16)
        for c in range(n_chunks):
            rs = slice(c * GMLP_CHUNK, (c + 1) * GMLP_CHUNK)
            z = _dot(w, vn[rs, :]) + bsp_ref[:, cs]
            gs_ref[rs, cs] = u[rs, cs] * z
    gated = gs_ref[...]
    msg = jnp.mean(gated * gated, axis=-1, keepdims=True)
    gated_ref[...] = (gated * lax.rsqrt(msg + RMS_EPS) * gout_ref[...]).astype(jnp.bfloat16)


def _view_spec(seq, tm, dilation):
    blocks_per_row = seq // tm
    return pl.BlockSpec((1, tm // dilation, dilation * ATTN_WIDTH),
                        lambda i: (i // blocks_per_row, i % blocks_per_row, 0))


def _view_shape(batch, seq, dilation, dtype):
    return jax.ShapeDtypeStruct((batch, seq // dilation, dilation * ATTN_WIDTH), dtype)


def _in_proj(x2d, batch, seq, gmix, win, gq, gk, g64, gvg, wsp, bsp, gout):
    n = x2d.shape[0]
    tm = TM_PROJ
    full = lambda shape: pl.BlockSpec(shape, lambda i: (0,) * len(shape))
    tok = lambda w: pl.BlockSpec((tm, w), lambda i: (i, 0))
    view_specs = [_view_spec(seq, tm, d) for _, d in PATTERNS] * 3
    view_shapes = [_view_shape(batch, seq, d, jnp.bfloat16) for _, d in PATTERNS] * 3
    return pl.pallas_call(
        _in_proj_kernel,
        grid=(n // tm,),
        in_specs=[tok(D_MODEL), full((1, D_MODEL)), full((D_MODEL, IN_WIDTH)), full((1, ATTN_WIDTH)),
                  full((1, ATTN_WIDTH)), full((ATTN_WIDTH, ATTN_WIDTH)), full((1, GMLP_WIDTH)),
                  full((GMLP_GROUPS, GMLP_CHUNK, GMLP_CHUNK)), full((GMLP_CHUNK, GMLP_WIDTH)), full((1, GMLP_WIDTH))],
        out_specs=view_specs + [tok(GMLP_WIDTH)],
        out_shape=view_shapes + [jax.ShapeDtypeStruct((n, GMLP_WIDTH), jnp.bfloat16)],
        scratch_shapes=[pltpu.VMEM((tm, GMLP_WIDTH), jnp.float32),
                        pltpu.VMEM((ATTN_WIDTH // LANES, tm, LANES), jnp.float32)],
        compiler_params=pltpu.CompilerParams(dimension_semantics=("arbitrary",), vmem_limit_bytes=VMEM_LIMIT),
        name="in_proj",
    )(x2d, gmix, win, gq, gk, g64, gvg, wsp, bsp, gout)


def _attn_kernel(q_ref, kp_ref, kc_ref, vp_ref, vc_ref, o_ref, l_ref):
    tq = q_ref.shape[1]
    have_prev = pl.program_id(2) > 0
    qi = lax.broadcasted_iota(jnp.int32, (ATTN_BLOCK, 2 * ATTN_BLOCK), 0)
    kj = lax.broadcasted_iota(jnp.int32, (ATTN_BLOCK, 2 * ATTN_BLOCK), 1)
    rel = qi + ATTN_BLOCK - kj
    band = (rel >= 0) & (rel <= N_BACK)
    lane = lax.broadcasted_iota(jnp.int32, (ATTN_BLOCK, LANES), 1)
    n_lane_tiles = q_ref.shape[2] // LANES
    for qb in range(tq // ATTN_BLOCK):
        rs = slice(qb * ATTN_BLOCK, (qb + 1) * ATTN_BLOCK)
        if qb == 0:
            mask = band & ((kj >= ATTN_BLOCK) | have_prev)
        else:
            mask = band
        for p in range(n_lane_tiles):
            cs = slice(p * LANES, (p + 1) * LANES)
            qp = q_ref[0, rs, cs]
            if qb == 0:
                kprev, vprev = kp_ref[0, :, cs], vp_ref[0, :, cs]
            else:
                ps = slice((qb - 1) * ATTN_BLOCK, qb * ATTN_BLOCK)
                kprev, vprev = kc_ref[0, ps, cs], vc_ref[0, ps, cs]
            keys = jnp.concatenate([kprev, kc_ref[0, rs, cs]], axis=0)
            vals = jnp.concatenate([vprev, vc_ref[0, rs, cs]], axis=0)
            out_pair = jnp.zeros((ATTN_BLOCK, LANES), jnp.float32)
            lse_pair = jnp.zeros((ATTN_BLOCK, LANES), jnp.float32)
            for hh in range(2):
                in_head = (lane >= hh * HEAD_DIM) & (lane < (hh + 1) * HEAD_DIM)
                s = _dot_nt(jnp.where(in_head, qp, jnp.zeros_like(qp)), keys)
                s = jnp.where(mask, s, NEG)
                m = jnp.max(s, axis=-1, keepdims=True)
                e = jnp.exp(s - m)
                den = jnp.sum(e, axis=-1, keepdims=True)
                o = _dot(e.astype(jnp.bfloat16), vals) / den
                out_pair = jnp.where(in_head, o, out_pair)
                lse_pair = jnp.where(in_head, m + jnp.log(den), lse_pair)
            o_ref[0, rs, cs] = out_pair.astype(o_ref.dtype)
            l_ref[0, rs, cs] = lse_pair


def _attention(q, k, v, dilation):
    batch, L, _ = q.shape
    tq = min(TQ_ATTN, L)
    res = min(dilation, TQ_ATTN // tq)
    cur = pl.BlockSpec((1, tq, res * ATTN_WIDTH), lambda b, r, i: (b, i, r))
    prev = pl.BlockSpec((1, ATTN_BLOCK, res * ATTN_WIDTH),
                        lambda b, r, i: (b, jnp.maximum(i * (tq // ATTN_BLOCK) - 1, 0), r))
    shape = (batch, L, dilation * ATTN_WIDTH)
    return pl.pallas_call(
        _attn_kernel,
        grid=(batch, dilation // res, L // tq),
        in_specs=[cur, prev, cur, prev, cur],
        out_specs=[cur, cur],
        out_shape=[jax.ShapeDtypeStruct(shape, jnp.bfloat16), jax.ShapeDtypeStruct(shape, jnp.float32)],
        compiler_params=pltpu.CompilerParams(dimension_semantics=("arbitrary",) * 3, vmem_limit_bytes=VMEM_LIMIT),
        name=f"attention_d{dilation}",
    )(q, k, k, v, v)


def _load_view(v_ref, stage_ref, dilation):
    if dilation == 1:
        return v_ref[0].astype(jnp.float32)
    rows = v_ref.shape[1]
    chunks = ATTN_WIDTH // LANES
    for r in range(dilation):
        for c in range(chunks):
            stage_ref[c, pl.ds(r, rows, stride=dilation), :] = (
                v_ref[0, :, pl.ds(r * ATTN_WIDTH + c * LANES, LANES)].astype(jnp.float32))
    return jnp.concatenate([stage_ref[c] for c in range(chunks)], axis=1)


def _store_token_tiles(val, o_ref):
    chunks = val.shape[1] // LANES
    for c in range(chunks):
        o_ref[pl.ds(c, val.shape[0], stride=chunks), :] = val[:, c * LANES:(c + 1) * LANES]


def _out_proj_kernel(o1_ref, o2_ref, o3_ref, l1_ref, l2_ref, l3_ref, gated_ref, x_ref, wout_ref, gattn_ref, x2_ref,
                     so2_ref, so3_ref, sl2_ref, sl3_ref):
    dil = [d for _, d in PATTERNS]
    l1 = _load_view(l1_ref, None, dil[0])
    l2 = _load_view(l2_ref, sl2_ref, dil[1])
    l3 = _load_view(l3_ref, sl3_ref, dil[2])
    o1 = _load_view(o1_ref, None, dil[0])
    o2 = _load_view(o2_ref, so2_ref, dil[1])
    o3 = _load_view(o3_ref, so3_ref, dil[2])
    m = jnp.maximum(jnp.maximum(l1, l2), l3)
    e1, e2, e3 = jnp.exp(l1 - m), jnp.exp(l2 - m), jnp.exp(l3 - m)
    attn = (e1 * o1 + e2 * o2 + e3 * o3) / (e1 + e2 + e3)
    ms = jnp.mean(attn * attn, axis=-1, keepdims=True)
    attn_n = (attn * lax.rsqrt(ms + RMS_EPS) * gattn_ref[...]).astype(jnp.bfloat16)
    y = _dot(attn_n, wout_ref[0:ATTN_WIDTH, :]) + _dot(gated_ref[...], wout_ref[ATTN_WIDTH:, :])
    x2_ref[...] = x_ref[...] + y


def _out_proj(outs, lses, gated, x2d, seq, wout, gattn):
    n = x2d.shape[0]
    tm = TM_PROJ
    tok = lambda w: pl.BlockSpec((tm, w), lambda i: (i, 0))
    full = lambda shape: pl.BlockSpec(shape, lambda i: (0,) * len(shape))
    view_specs = [_view_spec(seq, tm, d) for _, d in PATTERNS] * 2
    stage = pltpu.VMEM((ATTN_WIDTH // LANES, tm, LANES), jnp.float32)
    return pl.pallas_call(
        _out_proj_kernel,
        grid=(n // tm,),
        in_specs=view_specs + [tok(GMLP_WIDTH), tok(D_MODEL), full((D_MODEL, D_MODEL)), full((1, ATTN_WIDTH))],
        out_specs=tok(D_MODEL),
        out_shape=jax.ShapeDtypeStruct((n, D_MODEL), jnp.float32),
        scratch_shapes=[stage] * 4,
        compiler_params=pltpu.CompilerParams(dimension_semantics=("arbitrary",), vmem_limit_bytes=VMEM_LIMIT),
        name="out_proj",
    )(*outs, *lses, gated, x2d, wout, gattn)


CAND_J_COUNT = (16, 8, 5, 4, 3, 2, 2, 2)


def _top16_groups(vals, ids):
    vals = list(vals)
    out_v, out_i = [], []
    for k in range(PEER_TOPK):
        bv, bi = vals[0], ids[0]
        for v, i in zip(vals[1:], ids[1:]):
            gt = v > bv
            bv = jnp.where(gt, v, bv)
            bi = jnp.where(gt, i, bi)
        for sh in (4, 2, 1):
            rv, ri = pltpu.roll(bv, sh, axis=0), pltpu.roll(bi, sh, axis=0)
            better = (rv > bv) | ((rv == bv) & (ri < bi))
            bv = jnp.where(better, rv, bv)
            bi = jnp.where(better, ri, bi)
        out_v.append(bv)
        out_i.append(bi)
        if k + 1 < PEER_TOPK:
            vals = [jnp.where(i == bi, -jnp.inf, v) for v, i in zip(vals, ids)]
    return out_v, out_i


def _stack(reps, start):
    row = lax.broadcasted_iota(jnp.int32, reps[0].shape, 0)
    out = reps[start + SUBLANES - 1]
    for r in range(SUBLANES - 2, -1, -1):
        out = jnp.where(row == r, reps[start + r], out)
    return out


def _select_experts(sa, sb):
    t = sa.shape[1]
    row = lax.broadcasted_iota(jnp.int32, (SUBLANES, t), 0)
    groups = lambda s: [s[g * SUBLANES:(g + 1) * SUBLANES, :] for g in range(N_KEYS // SUBLANES)]
    key_ids = [row + g * SUBLANES for g in range(N_KEYS // SUBLANES)]
    va, ia = _top16_groups(groups(sa), key_ids)
    vb, ib = _top16_groups(groups(sb), key_ids)
    vb_lo, vb_hi, va_hi = _stack(vb, 0), _stack(vb, SUBLANES), _stack(va, SUBLANES)
    cand = [va[0] + vb_lo, va[0] + vb_hi]
    cand_ids = [row, row + SUBLANES]
    for i in range(1, SUBLANES):
        cand.append(jnp.where(row < CAND_J_COUNT[i], va[i] + vb_lo, -jnp.inf))
        cand_ids.append(row + i * PEER_TOPK)
    cand.append(va_hi + vb[0])
    cand_ids.append((row + SUBLANES) * PEER_TOPK)
    top_s, pos = _top16_groups(cand, cand_ids)
    idx_halves, e_halves = [], []
    for h in range(2):
        p = _stack(pos, h * SUBLANES)
        pa, pb = p >> 4, p & (PEER_TOPK - 1)
        ea, eb = jnp.zeros_like(p), jnp.zeros_like(p)
        for i in range(PEER_TOPK):
            ea = jnp.where(pa == i, ia[i], ea)
            eb = jnp.where(pb == i, ib[i], eb)
        idx_halves.append((ea * N_KEYS + eb) * TABLE_ROWS_PER_EXPERT)
        e_halves.append(jnp.exp(_stack(top_s, h * SUBLANES) - top_s[0]))
    den = e_halves[0] + e_halves[1]
    for sh in (4, 2, 1):
        den = den + pltpu.roll(den, sh, axis=0)
    return jnp.concatenate(idx_halves, axis=0), jnp.concatenate([e / den for e in e_halves], axis=0)


def _route_kernel(x2_ref, gffn_ref, wq_ref, ka_ref, kb_ref, xn_ref, idx_ref, gate_ref):
    x = x2_ref[...]
    ms = jnp.mean(x * x, axis=-1, keepdims=True)
    xn = x * lax.rsqrt(ms + RMS_EPS) * gffn_ref[...]
    _store_token_tiles(xn, xn_ref)
    qh = _dot(xn.astype(jnp.bfloat16), wq_ref[...])
    half = D_KEY // 2
    for lt in range(x.shape[0] // LANES):
        ts = slice(lt * LANES, (lt + 1) * LANES)
        idx_rows, gate_rows = [], []
        for h in range(PEER_HEADS):
            qa = qh[ts, h * D_KEY:h * D_KEY + half].astype(jnp.bfloat16)
            qb = qh[ts, h * D_KEY + half:(h + 1) * D_KEY].astype(jnp.bfloat16)
            idx_h, gate_h = _select_experts(_dot_nt(ka_ref[...], qa), _dot_nt(kb_ref[...], qb))
            idx_rows.append(idx_h)
            gate_rows.append(gate_h)
        idx_ref[ts, :] = jnp.concatenate(idx_rows, axis=0).T
        gate_ref[ts, :] = jnp.concatenate(gate_rows, axis=0).T


def _peer_route(x2, gffn, wq, ka, kb):
    n = x2.shape[0]
    tm = TM_ROUTE
    tok = lambda w: pl.BlockSpec((tm, w), lambda i: (i, 0))
    full = lambda shape: pl.BlockSpec(shape, lambda i: (0,) * len(shape))
    return pl.pallas_call(
        _route_kernel,
        grid=(n // tm,),
        in_specs=[tok(D_MODEL), full((1, D_MODEL)), full((D_MODEL, PEER_HEADS * D_KEY)),
                  full((N_KEYS, D_KEY // 2)), full((N_KEYS, D_KEY // 2))],
        out_specs=[pl.BlockSpec((tm * SUBLANES, LANES), lambda i: (i, 0)), tok(NJ), tok(NJ)],
        out_shape=[jax.ShapeDtypeStruct((n * SUBLANES, LANES), jnp.float32),
                   jax.ShapeDtypeStruct((n, NJ), jnp.int32),
                   jax.ShapeDtypeStruct((n, NJ), jnp.float32)],
        compiler_params=pltpu.CompilerParams(dimension_semantics=("arbitrary",), vmem_limit_bytes=VMEM_LIMIT),
        name="peer_route",
    )(x2, gffn, wq, ka, kb)


def _pack_kernel(t_ref, o_ref):
    x = t_ref[...]
    bits = lax.bitcast_convert_type(x.astype(jnp.bfloat16).astype(jnp.float32), jnp.uint32)
    half = D_MODEL // 2
    word = (bits[:, :half] >> 16) | (bits[:, half:] & jnp.uint32(0xFFFF0000))
    for s in range(TABLE_ROWS_PER_EXPERT):
        o_ref[pl.ds(s, x.shape[0], stride=TABLE_ROWS_PER_EXPERT), :] = word[:, s * LANES:(s + 1) * LANES]


def _pack_table(tab):
    e, te = tab.shape[0], TE_PACK
    return pl.pallas_call(
        _pack_kernel,
        grid=(e // te,),
        in_specs=[pl.BlockSpec((te, D_MODEL), lambda i: (i, 0))],
        out_specs=pl.BlockSpec((te * TABLE_ROWS_PER_EXPERT, LANES), lambda i: (i, 0)),
        out_shape=jax.ShapeDtypeStruct((e * TABLE_ROWS_PER_EXPERT, LANES), jnp.uint32),
        compiler_params=pltpu.CompilerParams(dimension_semantics=("arbitrary",), vmem_limit_bytes=VMEM_LIMIT),
        name="pack_table",
    )(tab)


def _unpack(w):
    lo = lax.bitcast_convert_type(w << 16, jnp.float32)
    hi = lax.bitcast_convert_type(w & jnp.uint32(0xFFFF0000), jnp.float32)
    return lo, hi


def _gather_rows(idx_ref, base, tab_ref, w_ref):
    for g in range(NJ // SUBLANES):
        window = idx_ref.at[pl.ds(base + g * SUBLANES, SUBLANES)]
        for r in range(SUBLANES):
            off = pl.multiple_of(window[r], TABLE_ROWS_PER_EXPERT)
            w_ref[pl.ds((g * SUBLANES + r) * TABLE_ROWS_PER_EXPERT, TABLE_ROWS_PER_EXPERT), :] = (
                tab_ref[pl.ds(off, TABLE_ROWS_PER_EXPERT), :])


def _group_chunk(w_ref, g, s):
    return _unpack(w_ref[pl.ds(g * SUBLANES * TABLE_ROWS_PER_EXPERT + s, SUBLANES, stride=TABLE_ROWS_PER_EXPERT), :])


def _peer_up_kernel(idx_ref, gate_ref, xn_ref, tab_ref, coef_ref, w_ref, act_ref, *, tb):
    def partial_dots(t):
        _gather_rows(idx_ref, t * NJ, tab_ref, w_ref)
        xt = xn_ref[t]
        xb = [jnp.broadcast_to(xt[c:c + 1, :], (SUBLANES, LANES)) for c in range(SUBLANES)]
        parts = []
        for g in range(NJ // SUBLANES):
            acc = None
            for s in range(4):
                lo, hi = _group_chunk(w_ref, g, s)
                term = lo * xb[s] + hi * xb[4 + s]
                acc = term if acc is None else acc + term
            parts.append(acc)
        return jnp.concatenate(parts, axis=0)

    def lane_sums(p, t):
        act_ref[pl.ds(t, 1), :] = jnp.sum(p.T, axis=0, keepdims=True)

    def body(t, p_prev):
        lane_sums(p_prev, jnp.maximum(t - 1, 0))
        return partial_dots(t)

    p_last = lax.fori_loop(0, tb, body, jnp.zeros((NJ, LANES), jnp.float32))
    lane_sums(p_last, tb - 1)
    coef_ref[...] = gate_ref[...] * _gelu(act_ref[...])


def _peer_down_kernel(idx_ref, coef_ref, x2_ref, tab_ref, o_ref, w_ref, ot_ref, *, tb):
    def spread(t):
        return jnp.broadcast_to(coef_ref[pl.ds(t, 1), :], (NJ, LANES)).T

    def weighted_sum(t, cb):
        _gather_rows(idx_ref, t * NJ, tab_ref, w_ref)
        acc_lo, acc_hi = [None] * 4, [None] * 4
        for g in range(NJ // SUBLANES):
            cg = cb[g * SUBLANES:(g + 1) * SUBLANES, :]
            for s in range(4):
                lo, hi = _group_chunk(w_ref, g, s)
                acc_lo[s] = lo * cg if acc_lo[s] is None else acc_lo[s] + lo * cg
                acc_hi[s] = hi * cg if acc_hi[s] is None else acc_hi[s] + hi * cg
        rows = [jnp.sum(a, axis=0, keepdims=True) for a in acc_lo + acc_hi]
        ot_ref[pl.ds(pl.multiple_of(t * SUBLANES, SUBLANES), SUBLANES), :] = x2_ref[t] + jnp.concatenate(rows, axis=0)

    def body(t, cb):
        nxt = spread(jnp.minimum(t + 1, tb - 1))
        weighted_sum(t, cb)
        return nxt

    lax.fori_loop(0, tb, body, spread(0))
    for c in range(D_MODEL // LANES):
        o_ref[:, c * LANES:(c + 1) * LANES] = ot_ref[pl.ds(c, tb, stride=D_MODEL // LANES), :]


def _peer_specs(n, table_rows, tb):
    idx_spec = pl.BlockSpec((tb * NJ,), lambda i: (i,), memory_space=pltpu.SMEM)
    row_spec = pl.BlockSpec((tb, NJ), lambda i: (i, 0))
    tok_spec = pl.BlockSpec((tb, SUBLANES, LANES), lambda i: (i, 0, 0))
    tab_spec = pl.BlockSpec((table_rows, LANES), lambda i: (0, 0), pipeline_mode=pl.Buffered(1))
    return idx_spec, row_spec, tok_spec, tab_spec


_W_SCRATCH = pltpu.VMEM((NJ * TABLE_ROWS_PER_EXPERT, LANES), jnp.uint32)
_SQUARE_SCRATCH = pltpu.VMEM((NJ, LANES), jnp.float32)


def _peer_up(idx, gate, xn, tab):
    n, tb = gate.shape[0], TB_PEER
    idx_spec, row_spec, tok_spec, tab_spec = _peer_specs(n, tab.shape[0], tb)
    return pl.pallas_call(
        functools.partial(_peer_up_kernel, tb=tb),
        grid=(n // tb,),
        in_specs=[idx_spec, row_spec, tok_spec, tab_spec],
        out_specs=row_spec,
        out_shape=jax.ShapeDtypeStruct((n, NJ), jnp.float32),
        scratch_shapes=[_W_SCRATCH, pltpu.VMEM((tb, NJ), jnp.float32)],
        compiler_params=pltpu.CompilerParams(dimension_semantics=("arbitrary",), vmem_limit_bytes=VMEM_LIMIT),
        name="peer_up",
    )(idx.reshape(-1), gate, xn.reshape(n, SUBLANES, LANES), tab)


def _peer_down(idx, coef, x2, tab):
    n, tb = coef.shape[0], TB_PEER
    idx_spec, row_spec, tok_spec, tab_spec = _peer_specs(n, tab.shape[0], tb)
    return pl.pallas_call(
        functools.partial(_peer_down_kernel, tb=tb),
        grid=(n // tb,),
        in_specs=[idx_spec, row_spec, tok_spec, tab_spec],
        out_specs=pl.BlockSpec((tb, D_MODEL), lambda i: (i, 0)),
        out_shape=jax.ShapeDtypeStruct((n, D_MODEL), jnp.float32),
        scratch_shapes=[_W_SCRATCH, pltpu.VMEM((tb * SUBLANES, LANES), jnp.float32)],
        compiler_params=pltpu.CompilerParams(dimension_semantics=("arbitrary",), vmem_limit_bytes=VMEM_LIMIT),
        name="peer_down",
    )(idx.reshape(-1), coef, x2.reshape(n, SUBLANES, LANES), tab)


def _layer(x2d, batch, seq, mix_norm_g, w_in, q_norm_g, k_norm_g, v_gate_norm_g, w_spatial, b_spatial, attn_out_g,
           gate_out_g, w_out, ffn_norm_g, w_query, sub_keys_a, sub_keys_b, expert_u, expert_v):
    bf16 = jnp.bfloat16
    heads = ATTN_WIDTH // HEAD_DIM
    lane = jnp.arange(ATTN_WIDTH)
    g64 = (lane[:, None] // HEAD_DIM == lane[None, :] // HEAD_DIM).astype(bf16)
    gq = (jnp.tile(q_norm_g, heads) * HEAD_DIM ** -0.5)[None, :]
    gk = jnp.tile(k_norm_g, heads)[None, :]
    bsp = jnp.repeat(b_spatial.T, GMLP_WIDTH // GMLP_GROUPS, axis=1)
    *qkv, gated = _in_proj(x2d, batch, seq, mix_norm_g[None, :], w_in.astype(bf16), gq, gk, g64,
                           v_gate_norm_g.reshape(1, GMLP_WIDTH), w_spatial, bsp, gate_out_g[None, :])
    n_pat = len(PATTERNS)
    outs, lses = [], []
    for p, (window, dilation) in enumerate(PATTERNS):
        assert window // dilation == N_BACK
        o, l = _attention(qkv[p], qkv[n_pat + p], qkv[2 * n_pat + p], dilation)
        outs.append(o)
        lses.append(l)
    x2 = _out_proj(outs, lses, gated, x2d, seq, w_out.astype(bf16), attn_out_g[None, :])
    xn, idx, gate = _peer_route(x2, ffn_norm_g[None, :], w_query.astype(bf16), sub_keys_a.astype(bf16),
                                sub_keys_b.astype(bf16))
    coef = _peer_up(idx, gate, xn, _pack_table(expert_u))
    return _peer_down(idx, coef, x2, _pack_table(expert_v))


def kernel(x, mix_norm_g, w_in, q_norm_g, k_norm_g, v_gate_norm_g, w_spatial, b_spatial, attn_out_g, gate_out_g,
           w_out, ffn_norm_g, w_query, sub_keys_a, sub_keys_b, expert_u, expert_v):
    batch, seq, d = x.shape
    assert d == D_MODEL and seq % TM_PROJ == 0 and (batch * seq) % TB_PEER == 0 and TB_PEER % 2 == 0
    for _, dilation in PATTERNS:
        sub_len = seq // dilation
        assert seq % dilation == 0 and sub_len % ATTN_BLOCK == 0 and sub_len % min(TQ_ATTN, sub_len) == 0
    x2d = x.reshape(batch * seq, d)
    for l in range(mix_norm_g.shape[0]):
        x2d = _layer(x2d, batch, seq, mix_norm_g[l], w_in[l], q_norm_g[l], k_norm_g[l], v_gate_norm_g[l],
                     w_spatial[l], b_spatial[l], attn_out_g[l], gate_out_g[l], w_out[l], ffn_norm_g[l],
                     w_query[l], sub_keys_a[l], sub_keys_b[l], expert_u[l], expert_v[l])
    return x2d.reshape(batch, seq, d)
```

```python
import functools

import jax
import jax.numpy as jnp
from jax import lax
from jax.experimental import pallas as pl
from jax.experimental.pallas import tpu as pltpu

D_MODEL = 1024
ATTN_WIDTH = 512
HEAD_DIM = 64
HEAD_PAIRS = ATTN_WIDTH // 128
PATTERNS = ((128, 1), (512, 4), (2048, 16))
N_BACK = 128
ATTN_BLOCK = 128
GMLP_WIDTH = 512
GMLP_CHUNK = 128
GMLP_GROUPS = 4
IN_WIDTH = 3 * ATTN_WIDTH + 2 * GMLP_WIDTH
PEER_HEADS = 8
PEER_TOPK = 16
N_KEYS = 128
D_KEY = 256
NJ = PEER_HEADS * PEER_TOPK
TABLE_ROWS_PER_EXPERT = 4
RMS_EPS = 1e-6
NEG = -1e30

LANES = 128
SUBLANES = 8
VMEM_LIMIT = 56 * 2 ** 20

TM_PROJ = 512
TQ_ATTN = 512
TM_ROUTE = 256
TB_PEER = 512
TE_PACK = 512


def _gelu(x):
    return 0.5 * x * (1.0 + lax.erf(x * (2.0 ** -0.5)))


def _dot(a, b):
    return jnp.dot(a, b, preferred_element_type=jnp.float32)


def _dot_nt(a, b):
    return lax.dot_general(a, b, (((1,), (1,)), ((), ())), preferred_element_type=jnp.float32)


def _store_views(val, stage_ref, out_refs):
    chunks = ATTN_WIDTH // LANES
    for c in range(chunks):
        stage_ref[c] = val[:, c * LANES:(c + 1) * LANES]
    for (_, dilation), o_ref in zip(PATTERNS, out_refs):
        if dilation == 1:
            o_ref[0] = val.astype(jnp.bfloat16)
            continue
        rows = val.shape[0] // dilation
        for r in range(dilation):
            for c in range(chunks):
                lanes = pl.ds(r * ATTN_WIDTH + c * LANES, LANES)
                o_ref[0, :, lanes] = stage_ref[c, pl.ds(r, rows, stride=dilation), :].astype(jnp.bfloat16)


def _in_proj_kernel(x_ref, gmix_ref, win_ref, gq_ref, gk_ref, g64_ref, gvg_ref, wsp_ref, bsp_ref, gout_ref,
                    q1_ref, q2_ref, q3_ref, k1_ref, k2_ref, k3_ref, v1_ref, v2_ref, v3_ref, gated_ref,
                    gs_ref, stage_ref):
    x = x_ref[...]
    ms = jnp.mean(x * x, axis=-1, keepdims=True)
    hn = (x * lax.rsqrt(ms + RMS_EPS) * gmix_ref[...]).astype(jnp.bfloat16)

    def head_norm(t, g):
        msq = _dot((t * t).astype(jnp.bfloat16), g64_ref[...]) * (1.0 / HEAD_DIM)
        return t * lax.rsqrt(msq + RMS_EPS) * g

    q = _dot(hn, win_ref[:, 0:ATTN_WIDTH])
    _store_views(head_norm(q, gq_ref[...]), stage_ref, (q1_ref, q2_ref, q3_ref))
    k = _dot(hn, win_ref[:, ATTN_WIDTH:2 * ATTN_WIDTH])
    _store_views(head_norm(k, gk_ref[...]), stage_ref, (k1_ref, k2_ref, k3_ref))
    _store_views(_dot(hn, win_ref[:, 2 * ATTN_WIDTH:3 * ATTN_WIDTH]), stage_ref, (v1_ref, v2_ref, v3_ref))

    u = _gelu(_dot(hn, win_ref[:, 3 * ATTN_WIDTH:3 * ATTN_WIDTH + GMLP_WIDTH]))
    gv = _gelu(_dot(hn, win_ref[:, 3 * ATTN_WIDTH + GMLP_WIDTH:IN_WIDTH]))
    row = lax.broadcasted_iota(jnp.int32, (GMLP_CHUNK, GMLP_CHUNK), 0)
    col = lax.broadcasted_iota(jnp.int32, (GMLP_CHUNK, GMLP_CHUNK), 1)
    causal = col <= row
    n_chunks = x.shape[0] // GMLP_CHUNK
    for g in range(GMLP_GROUPS):
        cs = slice(g * LANES, (g + 1) * LANES)
        vg = gv[:, cs]
        msg = jnp.mean(vg * vg, axis=-1, keepdims=True)
        vn = (vg * lax.rsqrt(msg + RMS_EPS) * gvg_ref[:, cs]).astype(jnp.bfloat16)
        w = jnp.where(causal, wsp_ref[g], 0.0).astype(jnp.bfloat16)
        for c in range(n_chunks):
            rs = slice(c * GMLP_CHUNK, (c + 1) * GMLP_CHUNK)
            z = _dot(w, vn[rs, :]) + bsp_ref[:, cs]
            gs_ref[rs, cs] = u[rs, cs] * z
    gated = gs_ref[...]
    msg = jnp.mean(gated * gated, axis=-1, keepdims=True)
    gated_ref[...] = (gated * lax.rsqrt(msg + RMS_EPS) * gout_ref[...]).astype(jnp.bfloat16)


def _view_spec(seq, tm, dilation):
    blocks_per_row = seq // tm
    return pl.BlockSpec((1, tm // dilation, dilation * ATTN_WIDTH),
                        lambda i: (i // blocks_per_row, i % blocks_per_row, 0))


def _view_shape(batch, seq, dilation, dtype):
    return jax.ShapeDtypeStruct((batch, seq // dilation, dilation * ATTN_WIDTH), dtype)


def _in_proj(x2d, batch, seq, gmix, win, gq, gk, g64, gvg, wsp, bsp, gout):
    n = x2d.shape[0]
    tm = TM_PROJ
    full = lambda shape: pl.BlockSpec(shape, lambda i: (0,) * len(shape))
    tok = lambda w: pl.BlockSpec((tm, w), lambda i: (i, 0))
    view_specs = [_view_spec(seq, tm, d) for _, d in PATTERNS] * 3
    view_shapes = [_view_shape(batch, seq, d, jnp.bfloat16) for _, d in PATTERNS] * 3
    return pl.pallas_call(
        _in_proj_kernel,
        grid=(n // tm,),
        in_specs=[tok(D_MODEL), full((1, D_MODEL)), full((D_MODEL, IN_WIDTH)), full((1, ATTN_WIDTH)),
                  full((1, ATTN_WIDTH)), full((ATTN_WIDTH, ATTN_WIDTH)), full((1, GMLP_WIDTH)),
                  full((GMLP_GROUPS, GMLP_CHUNK, GMLP_CHUNK)), full((GMLP_CHUNK, GMLP_WIDTH)), full((1, GMLP_WIDTH))],
        out_specs=view_specs + [tok(GMLP_WIDTH)],
        out_shape=view_shapes + [jax.ShapeDtypeStruct((n, GMLP_WIDTH), jnp.bfloat16)],
        scratch_shapes=[pltpu.VMEM((tm, GMLP_WIDTH), jnp.float32),
                        pltpu.VMEM((ATTN_WIDTH // LANES, tm, LANES), jnp.float32)],
        compiler_params=pltpu.CompilerParams(dimension_semantics=("arbitrary",), vmem_limit_bytes=VMEM_LIMIT),
        name="in_proj",
    )(x2d, gmix, win, gq, gk, g64, gvg, wsp, bsp, gout)


def _attn_kernel(q_ref, kp_ref, kc_ref, vp_ref, vc_ref, o_ref, l_ref):
    tq = q_ref.shape[1]
    have_prev = pl.program_id(2) > 0
    qi = lax.broadcasted_iota(jnp.int32, (ATTN_BLOCK, 2 * ATTN_BLOCK), 0)
    kj = lax.broadcasted_iota(jnp.int32, (ATTN_BLOCK, 2 * ATTN_BLOCK), 1)
    rel = qi + ATTN_BLOCK - kj
    band = (rel >= 0) & (rel <= N_BACK)
    lane = lax.broadcasted_iota(jnp.int32, (ATTN_BLOCK, LANES), 1)
    n_lane_tiles = q_ref.shape[2] // LANES
    for qb in range(tq // ATTN_BLOCK):
        rs = slice(qb * ATTN_BLOCK, (qb + 1) * ATTN_BLOCK)
        if qb == 0:
            mask = band & ((kj >= ATTN_BLOCK) | have_prev)
        else:
            mask = band
        for p in range(n_lane_tiles):
            cs = slice(p * LANES, (p + 1) * LANES)
            qp = q_ref[0, rs, cs]
            if qb == 0:
                kprev, vprev = kp_ref[0, :, cs], vp_ref[0, :, cs]
            else:
                ps = slice((qb - 1) * ATTN_BLOCK, qb * ATTN_BLOCK)
                kprev, vprev = kc_ref[0, ps, cs], vc_ref[0, ps, cs]
            keys = jnp.concatenate([kprev, kc_ref[0, rs, cs]], axis=0)
            vals = jnp.concatenate([vprev, vc_ref[0, rs, cs]], axis=0)
            out_pair = jnp.zeros((ATTN_BLOCK, LANES), jnp.float32)
            lse_pair = jnp.zeros((ATTN_BLOCK, LANES), jnp.float32)
            for hh in range(2):
                in_head = (lane >= hh * HEAD_DIM) & (lane < (hh + 1) * HEAD_DIM)
                s = _dot_nt(jnp.where(in_head, qp, jnp.zeros_like(qp)), keys)
                s = jnp.where(mask, s, NEG)
                m = jnp.max(s, axis=-1, keepdims=True)
                e = jnp.exp(s - m)
                den = jnp.sum(e, axis=-1, keepdims=True)
                o = _dot(e.astype(jnp.bfloat16), vals) / den
                out_pair = jnp.where(in_head, o, out_pair)
                lse_pair = jnp.where(in_head, m + jnp.log(den), lse_pair)
            o_ref[0, rs, cs] = out_pair.astype(o_ref.dtype)
            l_ref[0, rs, cs] = lse_pair


def _attention(q, k, v, dilation):
    batch, L, _ = q.shape
    tq = min(TQ_ATTN, L)
    res = min(dilation, TQ_ATTN // tq)
    cur = pl.BlockSpec((1, tq, res * ATTN_WIDTH), lambda b, r, i: (b, i, r))
    prev = pl.BlockSpec((1, ATTN_BLOCK, res * ATTN_WIDTH),
                        lambda b, r, i: (b, jnp.maximum(i * (tq // ATTN_BLOCK) - 1, 0), r))
    shape = (batch, L, dilation * ATTN_WIDTH)
    return pl.pallas_call(
        _attn_kernel,
        grid=(batch, dilation // res, L // tq),
        in_specs=[cur, prev, cur, prev, cur],
        out_specs=[cur, cur],
        out_shape=[jax.ShapeDtypeStruct(shape, jnp.bfloat16), jax.ShapeDtypeStruct(shape, jnp.float32)],
        compiler_params=pltpu.CompilerParams(dimension_semantics=("arbitrary",) * 3, vmem_limit_bytes=VMEM_LIMIT),
        name=f"attention_d{dilation}",
    )(q, k, k, v, v)


def _load_view(v_ref, stage_ref, dilation):
    if dilation == 1:
        return v_ref[0].astype(jnp.float32)
    rows = v_ref.shape[1]
    chunks = ATTN_WIDTH // LANES
    for r in range(dilation):
        for c in range(chunks):
            stage_ref[c, pl.ds(r, rows, stride=dilation), :] = (
                v_ref[0, :, pl.ds(r * ATTN_WIDTH + c * LANES, LANES)].astype(jnp.float32))
    return jnp.concatenate([stage_ref[c] for c in range(chunks)], axis=1)


def _store_token_tiles(val, o_ref):
    chunks = val.shape[1] // LANES
    for c in range(chunks):
        o_ref[pl.ds(c, val.shape[0], stride=chunks), :] = val[:, c * LANES:(c + 1) * LANES]


def _out_proj_kernel(o1_ref, o2_ref, o3_ref, l1_ref, l2_ref, l3_ref, gated_ref, x_ref, wout_ref, gattn_ref, x2_ref,
                     so2_ref, so3_ref, sl2_ref, sl3_ref):
    dil = [d for _, d in PATTERNS]
    l1 = _load_view(l1_ref, None, dil[0])
    l2 = _load_view(l2_ref, sl2_ref, dil[1])
    l3 = _load_view(l3_ref, sl3_ref, dil[2])
    o1 = _load_view(o1_ref, None, dil[0])
    o2 = _load_view(o2_ref, so2_ref, dil[1])
    o3 = _load_view(o3_ref, so3_ref, dil[2])
    m = jnp.maximum(jnp.maximum(l1, l2), l3)
    e1, e2, e3 = jnp.exp(l1 - m), jnp.exp(l2 - m), jnp.exp(l3 - m)
    attn = (e1 * o1 + e2 * o2 + e3 * o3) / (e1 + e2 + e3)
    ms = jnp.mean(attn * attn, axis=-1, keepdims=True)
    attn_n = (attn * lax.rsqrt(ms + RMS_EPS) * gattn_ref[...]).astype(jnp.bfloat16)
    y = _dot(attn_n, wout_ref[0:ATTN_WIDTH, :]) + _dot(gated_ref[...], wout_ref[ATTN_WIDTH:, :])
    x2_ref[...] = x_ref[...] + y


def _out_proj(outs, lses, gated, x2d, seq, wout, gattn):
    n = x2d.shape[0]
    tm = TM_PROJ
    tok = lambda w: pl.BlockSpec((tm, w), lambda i: (i, 0))
    full = lambda shape: pl.BlockSpec(shape, lambda i: (0,) * len(shape))
    view_specs = [_view_spec(seq, tm, d) for _, d in PATTERNS] * 2
    stage = pltpu.VMEM((ATTN_WIDTH // LANES, tm, LANES), jnp.float32)
    return pl.pallas_call(
        _out_proj_kernel,
        grid=(n // tm,),
        in_specs=view_specs + [tok(GMLP_WIDTH), tok(D_MODEL), full((D_MODEL, D_MODEL)), full((1, ATTN_WIDTH))],
        out_specs=tok(D_MODEL),
        out_shape=jax.ShapeDtypeStruct((n, D_MODEL), jnp.float32),
        scratch_shapes=[stage] * 4,
        compiler_params=pltpu.CompilerParams(dimension_semantics=("arbitrary",), vmem_limit_bytes=VMEM_LIMIT),
        name="out_proj",
    )(*outs, *lses, gated, x2d, wout, gattn)


CAND_J_COUNT = (16, 8, 5, 4, 3, 2, 2, 2)


def _top16_groups(vals, ids):
    vals = list(vals)
    out_v, out_i = [], []
    for k in range(PEER_TOPK):
        bv, bi = vals[0], ids[0]
        for v, i in zip(vals[1:], ids[1:]):
            gt = v > bv
            bv = jnp.where(gt, v, bv)
            bi = jnp.where(gt, i, bi)
        for sh in (4, 2, 1):
            rv, ri = pltpu.roll(bv, sh, axis=0), pltpu.roll(bi, sh, axis=0)
            better = (rv > bv) | ((rv == bv) & (ri < bi))
            bv = jnp.where(better, rv, bv)
            bi = jnp.where(better, ri, bi)
        out_v.append(bv)
        out_i.append(bi)
        if k + 1 < PEER_TOPK:
            vals = [jnp.where(i == bi, -jnp.inf, v) for v, i in zip(vals, ids)]
    return out_v, out_i


def _stack(reps, start):
    row = lax.broadcasted_iota(jnp.int32, reps[0].shape, 0)
    out = reps[start + SUBLANES - 1]
    for r in range(SUBLANES - 2, -1, -1):
        out = jnp.where(row == r, reps[start + r], out)
    return out


def _select_experts(sa, sb):
    t = sa.shape[1]
    row = lax.broadcasted_iota(jnp.int32, (SUBLANES, t), 0)
    groups = lambda s: [s[g * SUBLANES:(g + 1) * SUBLANES, :] for g in range(N_KEYS // SUBLANES)]
    key_ids = [row + g * SUBLANES for g in range(N_KEYS // SUBLANES)]
    va, ia = _top16_groups(groups(sa), key_ids)
    vb, ib = _top16_groups(groups(sb), key_ids)
    vb_lo, vb_hi, va_hi = _stack(vb, 0), _stack(vb, SUBLANES), _stack(va, SUBLANES)
    cand = [va[0] + vb_lo, va[0] + vb_hi]
    cand_ids = [row, row + SUBLANES]
    for i in range(1, SUBLANES):
        cand.append(jnp.where(row < CAND_J_COUNT[i], va[i] + vb_lo, -jnp.inf))
        cand_ids.append(row + i * PEER_TOPK)
    cand.append(va_hi + vb[0])
    cand_ids.append((row + SUBLANES) * PEER_TOPK)
    top_s, pos = _top16_groups(cand, cand_ids)
    idx_halves, e_halves = [], []
    for h in range(2):
        p = _stack(pos, h * SUBLANES)
        pa, pb = p >> 4, p & (PEER_TOPK - 1)
        ea, eb = jnp.zeros_like(p), jnp.zeros_like(p)
        for i in range(PEER_TOPK):
            ea = jnp.where(pa == i, ia[i], ea)
            eb = jnp.where(pb == i, ib[i], eb)
        idx_halves.append((ea * N_KEYS + eb) * TABLE_ROWS_PER_EXPERT)
        e_halves.append(jnp.exp(_stack(top_s, h * SUBLANES) - top_s[0]))
    den = e_halves[0] + e_halves[1]
    for sh in (4, 2, 1):
        den = den + pltpu.roll(den, sh, axis=0)
    return jnp.concatenate(idx_halves, axis=0), jnp.concatenate([e / den for e in e_halves], axis=0)


def _route_kernel(x2_ref, gffn_ref, wq_ref, ka_ref, kb_ref, xn_ref, idx_ref, gate_ref):
    x = x2_ref[...]
    ms = jnp.mean(x * x, axis=-1, keepdims=True)
    xn = x * lax.rsqrt(ms + RMS_EPS) * gffn_ref[...]
    _store_token_tiles(xn, xn_ref)
    qh = _dot(xn.astype(jnp.bfloat16), wq_ref[...])
    half = D_KEY // 2
    for lt in range(x.shape[0] // LANES):
        ts = slice(lt * LANES, (lt + 1) * LANES)
        idx_rows, gate_rows = [], []
        for h in range(PEER_HEADS):
            qa = qh[ts, h * D_KEY:h * D_KEY + half].astype(jnp.bfloat16)
            qb = qh[ts, h * D_KEY + half:(h + 1) * D_KEY].astype(jnp.bfloat16)
            idx_h, gate_h = _select_experts(_dot_nt(ka_ref[...], qa), _dot_nt(kb_ref[...], qb))
            idx_rows.append(idx_h)
            gate_rows.append(gate_h)
        idx_ref[ts, :] = jnp.concatenate(idx_rows, axis=0).T
        gate_ref[ts, :] = jnp.concatenate(gate_rows, axis=0).T


def _peer_route(x2, gffn, wq, ka, kb):
    n = x2.shape[0]
    tm = TM_ROUTE
    tok = lambda w: pl.BlockSpec((tm, w), lambda i: (i, 0))
    full = lambda shape: pl.BlockSpec(shape, lambda i: (0,) * len(shape))
    return pl.pallas_call(
        _route_kernel,
        grid=(n // tm,),
        in_specs=[tok(D_MODEL), full((1, D_MODEL)), full((D_MODEL, PEER_HEADS * D_KEY)),
                  full((N_KEYS, D_KEY // 2)), full((N_KEYS, D_KEY // 2))],
        out_specs=[pl.BlockSpec((tm * SUBLANES, LANES), lambda i: (i, 0)), tok(NJ), tok(NJ)],
        out_shape=[jax.ShapeDtypeStruct((n * SUBLANES, LANES), jnp.float32),
                   jax.ShapeDtypeStruct((n, NJ), jnp.int32),
                   jax.ShapeDtypeStruct((n, NJ), jnp.float32)],
        compiler_params=pltpu.CompilerParams(dimension_semantics=("arbitrary",), vmem_limit_bytes=VMEM_LIMIT),
        name="peer_route",
    )(x2, gffn, wq, ka, kb)


def _pack_kernel(t_ref, o_ref):
    x = t_ref[...]
    bits = lax.bitcast_convert_type(x.astype(jnp.bfloat16).astype(jnp.float32), jnp.uint32)
    half = D_MODEL // 2
    word = (bits[:, :half] >> 16) | (bits[:, half:] & jnp.uint32(0xFFFF0000))
    for s in range(TABLE_ROWS_PER_EXPERT):
        o_ref[pl.ds(s, x.shape[0], stride=TABLE_ROWS_PER_EXPERT), :] = word[:, s * LANES:(s + 1) * LANES]


def _pack_table(tab):
    e, te = tab.shape[0], TE_PACK
    return pl.pallas_call(
        _pack_kernel,
        grid=(e // te,),
        in_specs=[pl.BlockSpec((te, D_MODEL), lambda i: (i, 0))],
        out_specs=pl.BlockSpec((te * TABLE_ROWS_PER_EXPERT, LANES), lambda i: (i, 0)),
        out_shape=jax.ShapeDtypeStruct((e * TABLE_ROWS_PER_EXPERT, LANES), jnp.uint32),
        compiler_params=pltpu.CompilerParams(dimension_semantics=("arbitrary",), vmem_limit_bytes=VMEM_LIMIT),
        name="pack_table",
    )(tab)


def _unpack(w):
    lo = lax.bitcast_convert_type(w << 16, jnp.float32)
    hi = lax.bitcast_convert_type(w & jnp.uint32(0xFFFF0000), jnp.float32)
    return lo, hi


def _gather_rows(idx_ref, base, tab_ref, w_ref):
    for g in range(NJ // SUBLANES):
        window = idx_ref.at[pl.ds(base + g * SUBLANES, SUBLANES)]
        for r in range(SUBLANES):
            off = pl.multiple_of(window[r], TABLE_ROWS_PER_EXPERT)
            w_ref[pl.ds((g * SUBLANES + r) * TABLE_ROWS_PER_EXPERT, TABLE_ROWS_PER_EXPERT), :] = (
                tab_ref[pl.ds(off, TABLE_ROWS_PER_EXPERT), :])


def _group_chunk(w_ref, g, s):
    return _unpack(w_ref[pl.ds(g * SUBLANES * TABLE_ROWS_PER_EXPERT + s, SUBLANES, stride=TABLE_ROWS_PER_EXPERT), :])


def _peer_up_kernel(idx_ref, gate_ref, xn_ref, tab_ref, coef_ref, w_ref, act_ref, *, tb):
    def partial_dots(t):
        _gather_rows(idx_ref, t * NJ, tab_ref, w_ref)
        xt = xn_ref[t]
        xb = [jnp.broadcast_to(xt[c:c + 1, :], (SUBLANES, LANES)) for c in range(SUBLANES)]
        parts = []
        for g in range(NJ // SUBLANES):
            acc = None
            for s in range(4):
                lo, hi = _group_chunk(w_ref, g, s)
                term = lo * xb[s] + hi * xb[4 + s]
                acc = term if acc is None else acc + term
            parts.append(acc)
        return jnp.concatenate(parts, axis=0)

    def lane_sums(p, t):
        act_ref[pl.ds(t, 1), :] = jnp.sum(p.T, axis=0, keepdims=True)

    def body(t, p_prev):
        lane_sums(p_prev, jnp.maximum(t - 1, 0))
        return partial_dots(t)

    p_last = lax.fori_loop(0, tb, body, jnp.zeros((NJ, LANES), jnp.float32))
    lane_sums(p_last, tb - 1)
    coef_ref[...] = gate_ref[...] * _gelu(act_ref[...])


def _peer_down_kernel(idx_ref, coef_ref, x2_ref, tab_ref, o_ref, w_ref, ot_ref, *, tb):
    def spread(t):
        return jnp.broadcast_to(coef_ref[pl.ds(t, 1), :], (NJ, LANES)).T

    def weighted_sum(t, cb):
        _gather_rows(idx_ref, t * NJ, tab_ref, w_ref)
        acc_lo, acc_hi = [None] * 4, [None] * 4
        for g in range(NJ // SUBLANES):
            cg = cb[g * SUBLANES:(g + 1) * SUBLANES, :]
            for s in range(4):
                lo, hi = _group_chunk(w_ref, g, s)
                acc_lo[s] = lo * cg if acc_lo[s] is None else acc_lo[s] + lo * cg
                acc_hi[s] = hi * cg if acc_hi[s] is None else acc_hi[s] + hi * cg
        rows = [jnp.sum(a, axis=0, keepdims=True) for a in acc_lo + acc_hi]
        ot_ref[pl.ds(pl.multiple_of(t * SUBLANES, SUBLANES), SUBLANES), :] = x2_ref[t] + jnp.concatenate(rows, axis=0)

    def body(t, cb):
        nxt = spread(jnp.minimum(t + 1, tb - 1))
        weighted_sum(t, cb)
        return nxt

    lax.fori_loop(0, tb, body, spread(0))
    for c in range(D_MODEL // LANES):
        o_ref[:, c * LANES:(c + 1) * LANES] = ot_ref[pl.ds(c, tb, stride=D_MODEL // LANES), :]


def _peer_specs(n, table_rows, tb):
    idx_spec = pl.BlockSpec((tb * NJ,), lambda i: (i,), memory_space=pltpu.SMEM)
    row_spec = pl.BlockSpec((tb, NJ), lambda i: (i, 0))
    tok_spec = pl.BlockSpec((tb, SUBLANES, LANES), lambda i: (i, 0, 0))
    tab_spec = pl.BlockSpec((table_rows, LANES), lambda i: (0, 0), pipeline_mode=pl.Buffered(1))
    return idx_spec, row_spec, tok_spec, tab_spec


_W_SCRATCH = pltpu.VMEM((NJ * TABLE_ROWS_PER_EXPERT, LANES), jnp.uint32)
_SQUARE_SCRATCH = pltpu.VMEM((NJ, LANES), jnp.float32)


def _peer_up(idx, gate, xn, tab):
    n, tb = gate.shape[0], TB_PEER
    idx_spec, row_spec, tok_spec, tab_spec = _peer_specs(n, tab.shape[0], tb)
    return pl.pallas_call(
        functools.partial(_peer_up_kernel, tb=tb),
        grid=(n // tb,),
        in_specs=[idx_spec, row_spec, tok_spec, tab_spec],
        out_specs=row_spec,
        out_shape=jax.ShapeDtypeStruct((n, NJ), jnp.float32),
        scratch_shapes=[_W_SCRATCH, pltpu.VMEM((tb, NJ), jnp.float32)],
        compiler_params=pltpu.CompilerParams(dimension_semantics=("arbitrary",), vmem_limit_bytes=VMEM_LIMIT),
        name="peer_up",
    )(idx.reshape(-1), gate, xn.reshape(n, SUBLANES, LANES), tab)


def _peer_down(idx, coef, x2, tab):
    n, tb = coef.shape[0], TB_PEER
    idx_spec, row_spec, tok_spec, tab_spec = _peer_specs(n, tab.shape[0], tb)
    return pl.pallas_call(
        functools.partial(_peer_down_kernel, tb=tb),
        grid=(n // tb,),
        in_specs=[idx_spec, row_spec, tok_spec, tab_spec],
        out_specs=pl.BlockSpec((tb, D_MODEL), lambda i: (i, 0)),
        out_shape=jax.ShapeDtypeStruct((n, D_MODEL), jnp.float32),
        scratch_shapes=[_W_SCRATCH, pltpu.VMEM((tb * SUBLANES, LANES), jnp.float32)],
        compiler_params=pltpu.CompilerParams(dimension_semantics=("arbitrary",), vmem_limit_bytes=VMEM_LIMIT),
        name="peer_down",
    )(idx.reshape(-1), coef, x2.reshape(n, SUBLANES, LANES), tab)


def _layer(x2d, batch, seq, mix_norm_g, w_in, q_norm_g, k_norm_g, v_gate_norm_g, w_spatial, b_spatial, attn_out_g,
           gate_out_g, w_out, ffn_norm_g, w_query, sub_keys_a, sub_keys_b, expert_u, expert_v):
    bf16 = jnp.bfloat16
    heads = ATTN_WIDTH // HEAD_DIM
    lane = jnp.arange(ATTN_WIDTH)
    g64 = (lane[:, None] // HEAD_DIM == lane[None, :] // HEAD_DIM).astype(bf16)
    gq = (jnp.tile(q_norm_g, heads) * HEAD_DIM ** -0.5)[None, :]
    gk = jnp.tile(k_norm_g, heads)[None, :]
    bsp = jnp.repeat(b_spatial.T, GMLP_WIDTH // GMLP_GROUPS, axis=1)
    *qkv, gated = _in_proj(x2d, batch, seq, mix_norm_g[None, :], w_in.astype(bf16), gq, gk, g64,
                           v_gate_norm_g.reshape(1, GMLP_WIDTH), w_spatial, bsp, gate_out_g[None, :])
    n_pat = len(PATTERNS)
    outs, lses = [], []
    for p, (window, dilation) in enumerate(PATTERNS):
        assert window // dilation == N_BACK
        o, l = _attention(qkv[p], qkv[n_pat + p], qkv[2 * n_pat + p], dilation)
        outs.append(o)
        lses.append(l)
    x2 = _out_proj(outs, lses, gated, x2d, seq, w_out.astype(bf16), attn_out_g[None, :])
    xn, idx, gate = _peer_route(x2, ffn_norm_g[None, :], w_query.astype(bf16), sub_keys_a.astype(bf16),
                                sub_keys_b.astype(bf16))
    coef = _peer_up(idx, gate, xn, _pack_table(expert_u))
    return _peer_down(idx, coef, x2, _pack_table(expert_v))


def kernel(x, mix_norm_g, w_in, q_norm_g, k_norm_g, v_gate_norm_g, w_spatial, b_spatial, attn_out_g, gate_out_g,
           w_out, ffn_norm_g, w_query, sub_keys_a, sub_keys_b, expert_u, expert_v):
    batch, seq, d = x.shape
    assert d == D_MODEL and seq % TM_PROJ == 0 and (batch * seq) % TB_PEER == 0 and TB_PEER % 2 == 0
    for _, dilation in PATTERNS:
        sub_len = seq // dilation
        assert seq % dilation == 0 and sub_len % ATTN_BLOCK == 0 and sub_len % min(TQ_ATTN, sub_len) == 0
    x2d = x.reshape(batch * seq, d)
    for l in range(mix_norm_g.shape[0]):
        x2d = _layer(x2d, batch, seq, mix_norm_g[l], w_in[l], q_norm_g[l], k_norm_g[l], v_gate_norm_g[l],
                     w_spatial[l], b_spatial[l], attn_out_g[l], gate_out_g[l], w_out[l], ffn_norm_g[l],
                     w_query[l], sub_keys_a[l], sub_keys_b[l], expert_u[l], expert_v[l])
    return x2d.reshape(batch, seq, d)
```
